```python
import math
import jax, jax.numpy as jnp
from jax import lax
import numpy as np

D_MODEL = 1024
BATCH = 1
SEQ = 16384
DEPTH = 2
DEC_BATCH = 32
DEC_SEQ = 4
PAST_LEN = 16384
PAGE_SIZE = 128

ATT_GROUPS = ((128, 1), (512, 4), (2048, 16))
N_GROUPS = 3
HEADS_PER_GROUP = 4
HEAD_DIM = 128
ATT_WIDTH = HEADS_PER_GROUP * HEAD_DIM
BAND_BLOCK = 128
GLA_HEADS = 4
GLA_DK = D_MODEL // 2
GLA_DV = D_MODEL
GLA_HK = GLA_DK // GLA_HEADS
GLA_HV = GLA_DV // GLA_HEADS
GLA_GATE_RANK = 16
GLA_TAU = 16.0
GLA_CHUNK = 64
DFF_DENSE = 2816
N_EXPERTS = 8
TOP_K = 2
DFF_EXPERT = 3584
N_DENSE = (DEPTH + 1) // 2
N_MOE = DEPTH // 2
DN_ALPHA = (2 * DEPTH) ** 0.25
DN_BETA = (8 * DEPTH) ** -0.25
LN_EPS = 1e-5
RMS_EPS = 1e-6
N_IN = N_GROUPS * 3 * ATT_WIDTH + 2 * GLA_DK + 2 * GLA_DV + GLA_GATE_RANK + 2 * D_MODEL

kernel_name = "hybrid_dilated_gla_deepnorm_step"


def _split_cols(h):
    sizes = [ATT_WIDTH] * (3 * N_GROUPS) + [GLA_DK, GLA_DK, GLA_DV, GLA_DV, GLA_GATE_RANK, D_MODEL, D_MODEL]
    out, start = [], 0
    for n in sizes:
        out.append(h[..., start:start + n])
        start += n
    return out


def _alibi_slopes():
    n = N_GROUPS * HEADS_PER_GROUP
    i = jnp.arange(1, n + 1, dtype=jnp.float32)
    return jnp.exp2(-8.0 * i / n).reshape(N_GROUPS, HEADS_PER_GROUP)


def _layer_norm(x, g, b):
    xf = x.astype(jnp.float32)
    mu = jnp.mean(xf, -1, keepdims=True)
    var = jnp.mean(jnp.square(xf - mu), -1, keepdims=True)
    return ((xf - mu) * lax.rsqrt(var + LN_EPS) * g.astype(jnp.float32) + b.astype(jnp.float32)).astype(x.dtype)


def _dilated_band_attention(q, k, v, slopes, window, dilation):
    B, S, H, Dh = q.shape
    span = window // dilation
    unit = dilation * BAND_BLOCK
    Sp = -(-S // unit) * unit
    L = Sp // dilation
    nb = L // BAND_BLOCK

    def to_sub(t):
        t = jnp.pad(t, ((0, 0), (0, Sp - S), (0, 0), (0, 0)))
        t = t.reshape(B, L, dilation, H, Dh).transpose(0, 2, 1, 3, 4)
        return t.reshape(B * dilation, nb, BAND_BLOCK, H, Dh)

    def with_prev(t):
        prev = jnp.pad(t, ((0, 0), (1, 0), (0, 0), (0, 0), (0, 0)))[:, :-1]
        return jnp.concatenate([prev, t], axis=2)

    qb = to_sub(q)
    kk = with_prev(to_sub(k))
    vv = with_prev(to_sub(v))
    s = jnp.einsum('znqhd,znkhd->znhqk', qb, kk).astype(jnp.float32) * (Dh ** -0.5)
    qi = jnp.arange(BAND_BLOCK)[:, None] + BAND_BLOCK
    ki = jnp.arange(2 * BAND_BLOCK)[None, :]
    rel = qi - ki
    blk = jnp.arange(nb)[:, None, None]
    valid = (rel >= 0) & (rel <= span) & (blk * BAND_BLOCK + ki - BAND_BLOCK >= 0)
    bias = -slopes[:, None, None] * (rel * dilation).astype(jnp.float32)[None]
    s = jnp.where(valid[None, :, None], s + bias[None, None], -jnp.inf)
    lse = jax.nn.logsumexp(s, axis=-1)
    p = jnp.exp(s - lse[..., None])
    o = jnp.einsum('znhqk,znkhd->znqhd', p.astype(vv.dtype), vv)
    o = o.reshape(B, dilation, L, H, Dh).transpose(0, 2, 1, 3, 4).reshape(B, Sp, H, Dh)[:, :S]
    lse = lse.transpose(0, 1, 3, 2).reshape(B, dilation, L, H).transpose(0, 2, 1, 3).reshape(B, Sp, H)[:, :S]
    return o, lse


def _dilated_cached_attention(q, k_new, v_new, buf, slopes, window, dilation):
    T, Dh = q.shape[1], q.shape[-1]
    Wb = buf.shape[1]
    span = window // dilation
    kk = jnp.concatenate([buf[:, :, 0].astype(k_new.dtype), k_new], axis=1)
    vv = jnp.concatenate([buf[:, :, 1].astype(v_new.dtype), v_new], axis=1)
    steps = jnp.arange(span + 1)
    idx = Wb + jnp.arange(T)[:, None] - steps[None, :] * dilation
    valid = idx >= 0
    idx = jnp.maximum(idx, 0)
    kg = kk[:, idx]
    vg = vv[:, idx]
    s = jnp.einsum('bthd,btjhd->bhtj', q, kg).astype(jnp.float32) * (Dh ** -0.5)
    s = s - slopes[:, None, None] * (steps * dilation).astype(jnp.float32)[None, None, :]
    s = jnp.where(valid[None, None], s, -jnp.inf)
    lse = jax.nn.logsumexp(s, axis=-1)
    p = jnp.exp(s - lse[..., None])
    o = jnp.einsum('bhtj,btjhd->bthd', p.astype(vg.dtype), vg)
    return o, lse.transpose(0, 2, 1)


def _gla_chunked(q, k, v, log_a, s0, chunk):
    B, L, H, dk = q.shape
    dv = v.shape[-1]
    n = L // chunk

    def blocks(t):
        return t.astype(jnp.float32).reshape(B, n, chunk, H, t.shape[-1]).transpose(1, 0, 3, 2, 4)

    qc, kc, vc, ac = blocks(q), blocks(k), blocks(v), blocks(log_a)
    causal = jnp.tril(jnp.ones((chunk, chunk), bool))[..., None]

    def step(S, xs):
        qi, ki, vi, ai = xs
        b = jnp.cumsum(ai, axis=2)
        inter = jnp.einsum('bhck,bhkv->bhcv', qi * jnp.exp(b), S)
        diff = b[:, :, :, None, :] - b[:, :, None, :, :]
        decay = jnp.exp(jnp.where(causal, diff, -jnp.inf))
        att = jnp.einsum('bhtk,bhsk,bhtsk->bhts', qi, ki, decay)
        intra = jnp.einsum('bhts,bhsv->bhtv', att, vi)
        bl = b[:, :, -1:, :]
        S_new = jnp.exp(bl[:, :, 0])[..., None] * S + jnp.einsum('bhsk,bhsv->bhkv', ki * jnp.exp(bl - b), vi)
        return S_new, inter + intra

    S, o = lax.scan(step, s0, (qc, kc, vc, ac))
    return o.transpose(1, 0, 3, 2, 4).reshape(B, L, H, dv), S


def _mixer(x, w_in, w_alpha2, b_alpha, gla_norm_g, w_pa, w_pb, w_out, kv_bufs, s0, chunk):
    B, L, _ = x.shape
    parts = _split_cols(x @ w_in)
    att_parts = parts[:3 * N_GROUPS]
    gq, gk, gv, gr, ga, gate_a, gate_b = parts[3 * N_GROUPS:]
    slopes = _alibi_slopes()
    outs, lses, new_kv = [], [], []
    for g, (window, dil) in enumerate(ATT_GROUPS):
        q, k, v = (t.reshape(B, L, HEADS_PER_GROUP, HEAD_DIM) for t in att_parts[3 * g:3 * g + 3])
        if kv_bufs is None:
            o, lse = _dilated_band_attention(q, k, v, slopes[g], window, dil)
            keep = min(window, L)
            new_kv.append(jnp.stack([k[:, L - keep:], v[:, L - keep:]], axis=2))
        else:
            o, lse = _dilated_cached_attention(q, k, v, kv_bufs[g], slopes[g], window, dil)
            new_kv.append(jnp.stack([k, v], axis=2))
        outs.append(o)
        lses.append(lse)
    wgt = jax.nn.softmax(jnp.stack(lses, 0), axis=0)
    att = jnp.einsum('gblh,gblhd->blhd', wgt.astype(outs[0].dtype), jnp.stack(outs, 0)).reshape(B, L, ATT_WIDTH)

    qg = gq.reshape(B, L, GLA_HEADS, GLA_HK) * (GLA_HK ** -0.5)
    kg = gk.reshape(B, L, GLA_HEADS, GLA_HK)
    vg = gv.reshape(B, L, GLA_HEADS, GLA_HV)
    log_a = (jax.nn.log_sigmoid((ga @ w_alpha2 + b_alpha).astype(jnp.float32)) / GLA_TAU).reshape(B, L, GLA_HEADS, GLA_HK)
    og, s_new = _gla_chunked(qg, kg, vg, log_a, s0, chunk)
    og = og * lax.rsqrt(jnp.mean(jnp.square(og), -1, keepdims=True) + RMS_EPS)
    og = og.reshape(B, L, GLA_DV) * gla_norm_g.astype(jnp.float32)
    og = og.astype(x.dtype) * jax.nn.silu(gr)

    ya = att @ w_pa
    yb = og @ w_pb
    merged = jax.nn.sigmoid(gate_a) * ya + jax.nn.sigmoid(gate_b) * yb
    return merged @ w_out, new_kv, s_new


def _swiglu(x, w_gu, w_down):
    g, u = jnp.split(x @ w_gu, 2, axis=-1)
    return (jax.nn.silu(g) * u) @ w_down


def _moe(x, w_router, w_gu, w_down):
    logits = (x @ w_router).astype(jnp.float32)
    top_v, top_i = lax.top_k(logits, TOP_K)
    gates = jax.nn.softmax(top_v, axis=-1)
    dense_gate = jnp.sum(jax.nn.one_hot(top_i, N_EXPERTS, dtype=jnp.float32) * gates[..., None], axis=-2)
    y = jnp.zeros_like(x)
    for e in range(N_EXPERTS):
        y = y + dense_gate[..., e:e + 1].astype(x.dtype) * _swiglu(x, w_gu[e], w_down[e])
    return y


def setup_inputs(seed: int = 0) -> dict:
    key = jax.random.key(seed)
    ks = jax.random.split(key, 24)
    f32 = jnp.float32

    def nrm(k, shape, scale):
        return jax.random.normal(k, shape, f32) * scale

    H, Dh = HEADS_PER_GROUP, HEAD_DIM
    return {
        "x_prompt": nrm(ks[0], (BATCH, SEQ, D_MODEL), 1.0),
        "x_sample": nrm(ks[1], (DEC_BATCH, DEC_SEQ, D_MODEL), 1.0),
        "cache_kv_w128": nrm(ks[2], (DEPTH, DEC_BATCH, min(ATT_GROUPS[0][0], PAST_LEN), 2, H, Dh), 1.0),
        "cache_kv_w512": nrm(ks[3], (DEPTH, DEC_BATCH, min(ATT_GROUPS[1][0], PAST_LEN), 2, H, Dh), 1.0),
        "cache_kv_w2048": nrm(ks[4], (DEPTH, DEC_BATCH, min(ATT_GROUPS[2][0], PAST_LEN), 2, H, Dh), 1.0),
        "state_gla": nrm(ks[5], (DEPTH, DEC_BATCH, GLA_HEADS, GLA_HK, GLA_HV), 1.0),
        "w_in": nrm(ks[6], (DEPTH, D_MODEL, N_IN), D_MODEL ** -0.5),
        "w_alpha2": nrm(ks[7], (DEPTH, GLA_GATE_RANK, GLA_DK), GLA_GATE_RANK ** -0.5),
        "b_alpha": nrm(ks[8], (DEPTH, GLA_DK), 0.1),
        "gla_norm_g": 1.0 + nrm(ks[9], (DEPTH, GLA_DV), 0.02),
        "w_pa": nrm(ks[10], (DEPTH, ATT_WIDTH, D_MODEL), ATT_WIDTH ** -0.5),
        "w_pb": nrm(ks[11], (DEPTH, GLA_DV, D_MODEL), GLA_DV ** -0.5),
        "w_out": nrm(ks[12], (DEPTH, D_MODEL, D_MODEL), D_MODEL ** -0.5 * DN_BETA),
        "ln_mix_g": 1.0 + nrm(ks[13], (DEPTH, D_MODEL), 0.02),
        "ln_mix_b": nrm(ks[14], (DEPTH, D_MODEL), 0.02),
        "ln_ffn_g": 1.0 + nrm(ks[15], (DEPTH, D_MODEL), 0.02),
        "ln_ffn_b": nrm(ks[16], (DEPTH, D_MODEL), 0.02),
        "w_ffn_gu": nrm(ks[17], (N_DENSE, D_MODEL, 2 * DFF_DENSE), D_MODEL ** -0.5),
        "w_ffn_down": nrm(ks[18], (N_DENSE, DFF_DENSE, D_MODEL), DFF_DENSE ** -0.5 * DN_BETA),
        "w_router": nrm(ks[19], (N_MOE, D_MODEL, N_EXPERTS), D_MODEL ** -0.5),
        "w_exp_gu": nrm(ks[20], (N_MOE, N_EXPERTS, D_MODEL, 2 * DFF_EXPERT), D_MODEL ** -0.5),
        "w_exp_down": nrm(ks[21], (N_MOE, N_EXPERTS, DFF_EXPERT, D_MODEL), DFF_EXPERT ** -0.5 * DN_BETA),
    }


def reference(x_prompt, x_sample, cache_kv_w128, cache_kv_w512, cache_kv_w2048, state_gla,
              w_in, w_alpha2, b_alpha, gla_norm_g, w_pa, w_pb, w_out,
              ln_mix_g, ln_mix_b, ln_ffn_g, ln_ffn_b,
              w_ffn_gu, w_ffn_down, w_router, w_exp_gu, w_exp_down):
    bufs = (cache_kv_w128, cache_kv_w512, cache_kv_w2048)
    xp, xs = x_prompt, x_sample
    Lp, Ls = xp.shape[1], xs.shape[1]
    kv_p = [[] for _ in range(N_GROUPS)]
    kv_s = [[] for _ in range(N_GROUPS)]
    st_p, st_s = [], []
    for l in range(DEPTH):
        mix_w = (w_in[l], w_alpha2[l], b_alpha[l], gla_norm_g[l], w_pa[l], w_pb[l], w_out[l])
        s0p = jnp.zeros((xp.shape[0], GLA_HEADS, GLA_HK, GLA_HV), jnp.float32)
        mp, nkp, nsp = _mixer(xp, *mix_w, None, s0p, min(GLA_CHUNK, Lp))
        ms, nks, nss = _mixer(xs, *mix_w, tuple(b[l] for b in bufs), state_gla[l].astype(jnp.float32), Ls)
        for g in range(N_GROUPS):
            kv_p[g].append(nkp[g])
            kv_s[g].append(nks[g])
        st_p.append(nsp.astype(state_gla.dtype))
        st_s.append(nss.astype(state_gla.dtype))
        xp = _layer_norm(DN_ALPHA * xp + mp, ln_mix_g[l], ln_mix_b[l])
        xs = _layer_norm(DN_ALPHA * xs + ms, ln_mix_g[l], ln_mix_b[l])
        if l % 2 == 0:
            fp = _swiglu(xp, w_ffn_gu[l // 2], w_ffn_down[l // 2])
            fs = _swiglu(xs, w_ffn_gu[l // 2], w_ffn_down[l // 2])
        else:
            fp = _moe(xp, w_router[l // 2], w_exp_gu[l // 2], w_exp_down[l // 2])
            fs = _moe(xs, w_router[l // 2], w_exp_gu[l // 2], w_exp_down[l // 2])
        xp = _layer_norm(DN_ALPHA * xp + fp, ln_ffn_g[l], ln_ffn_b[l])
        xs = _layer_norm(DN_ALPHA * xs + fs, ln_ffn_g[l], ln_ffn_b[l])
    return (xp, xs,
            jnp.stack(kv_p[0], 0), jnp.stack(kv_s[0], 0),
            jnp.stack(kv_p[1], 0), jnp.stack(kv_s[1], 0),
            jnp.stack(kv_p[2], 0), jnp.stack(kv_s[2], 0),
            jnp.stack(st_p, 0), jnp.stack(st_s, 0))
```

```python
import functools
import math

import jax
import jax.numpy as jnp
from jax import lax
from jax.experimental import pallas as pl
from jax.experimental.pallas import tpu as pltpu

F32 = jnp.float32
BF16 = jnp.bfloat16

ATT_GROUPS = ((128, 1), (512, 4), (2048, 16))
N_GROUPS = 3
HEADS = 4
HEAD_DIM = 128
ATT_WIDTH = HEADS * HEAD_DIM
BAND = 128
GLA_HEADS = 4
GLA_HK = 128
GLA_HV = 256
GLA_DK = GLA_HEADS * GLA_HK
GLA_DV = GLA_HEADS * GLA_HV
GLA_RANK = 16
GLA_TAU = 16.0
GLA_CHUNK = 64
GLA_SUB = 16
N_EXPERTS = 8
LN_EPS = 1e-5
RMS_EPS = 1e-6
NEG = -1e30
LANE = 128
VMEM_LIMIT = 56 * 1024 * 1024

_NT = (((1,), (1,)), ((), ()))
_TN = (((0,), (0,)), ((), ()))


def _alibi_slopes():
    n = N_GROUPS * HEADS
    return [[2.0 ** (-8.0 * (g * HEADS + h + 1) / n) for h in range(HEADS)] for g in range(N_GROUPS)]


def _cparams(n_axes):
    return pltpu.CompilerParams(dimension_semantics=("arbitrary",) * n_axes, vmem_limit_bytes=VMEM_LIMIT)


def _layer_norm_rows(r, g, b):
    mu = jnp.mean(r, axis=-1, keepdims=True)
    c = r - mu
    var = jnp.mean(c * c, axis=-1, keepdims=True)
    return c * lax.rsqrt(var + LN_EPS) * g + b


def _sigmoid(x):
    return 1.0 / (1.0 + jnp.exp(-x))


def _silu(x):
    return x * _sigmoid(x)


def _mm_kernel(x_ref, w_ref, o_ref):
    o_ref[...] = jnp.dot(x_ref[...], w_ref[...], preferred_element_type=F32)


def _mm(x, w, tm, tn):
    M, K = x.shape
    N = w.shape[1]
    assert N % tn == 0
    return pl.pallas_call(
        _mm_kernel,
        grid=(N // tn, pl.cdiv(M, tm)),
        in_specs=[pl.BlockSpec((tm, K), lambda j, i: (i, 0)),
                  pl.BlockSpec((K, tn), lambda j, i: (0, j))],
        out_specs=pl.BlockSpec((tm, tn), lambda j, i: (i, j)),
        out_shape=jax.ShapeDtypeStruct((M, N), F32),
        compiler_params=_cparams(2),
        name="in_proj",
    )(x, w)


def _attn_prompt_kernel(q_ref, k_ref, v_ref, kp_ref, vp_ref, o_ref, l_ref, kk_ref, vv_ref,
                        *, slopes, dil, nblk):
    n = pl.program_id(1)
    kk_ref[0:BAND, :] = kp_ref[...].astype(BF16)
    kk_ref[BAND:, :] = k_ref[...].astype(BF16)
    vv_ref[0:BAND, :] = vp_ref[...].astype(BF16)
    vv_ref[BAND:, :] = v_ref[...].astype(BF16)

    qi = lax.broadcasted_iota(jnp.int32, (BAND, 2 * BAND), 0)
    ci = lax.broadcasted_iota(jnp.int32, (BAND, 2 * BAND), 1)
    rel = qi + BAND - ci
    in_band = (rel >= 0) & (rel <= BAND)
    rel_f = rel.astype(F32)
    scale = HEAD_DIM ** -0.5

    def block(j, carry):
        r0 = pl.multiple_of(j * BAND, BAND)
        first_col = jnp.where((j > 0) | (n > 0), 0, BAND)
        negmask = jnp.where(in_band & (ci >= first_col), 0.0, NEG)
        for h in range(HEADS):
            cs = slice(h * HEAD_DIM, (h + 1) * HEAD_DIM)
            q = q_ref[pl.ds(r0, BAND), cs].astype(BF16)
            kk = kk_ref[pl.ds(r0, 2 * BAND), cs]
            vv = vv_ref[pl.ds(r0, 2 * BAND), cs]
            s = lax.dot_general(q, kk, _NT, preferred_element_type=F32) * scale
            s = s - rel_f * (slopes[h] * dil) + negmask
            m = jnp.max(s, axis=-1, keepdims=True)
            p = jnp.exp(s - m)
            l = jnp.sum(p, axis=-1, keepdims=True)
            o = jnp.dot(p.astype(BF16), vv, preferred_element_type=F32) / l
            o_ref[pl.ds(r0, BAND), cs] = o
            l_ref[pl.ds(r0, BAND), cs] = jnp.broadcast_to(m + jnp.log(l), (BAND, HEAD_DIM))
        return carry

    lax.fori_loop(0, nblk, block, 0)


def _attn_prompt(a_mat, g, seq):
    T = a_mat.shape[0]
    _, dil = ATT_GROUPS[g]
    assert seq % (dil * BAND) == 0 and T % dil == 0
    nb = seq // (dil * BAND)
    nblk = math.gcd(nb, 8)
    rows = nblk * BAND
    ncol = a_mat.shape[1] // ATT_WIDTH
    av = a_mat.reshape(T // dil, dil * a_mat.shape[1])
    c0 = 3 * g

    def cur(c):
        return pl.BlockSpec((rows, ATT_WIDTH), lambda r, n: (n, r * ncol + c0 + c))

    def prev(c):
        return pl.BlockSpec((BAND, ATT_WIDTH), lambda r, n: (jnp.maximum(n * nblk - 1, 0), r * ncol + c0 + c))

    out_spec = pl.BlockSpec((rows, ATT_WIDTH), lambda r, n: (n, r))
    out_sds = jax.ShapeDtypeStruct((T // dil, dil * ATT_WIDTH), F32)
    o, l = pl.pallas_call(
        functools.partial(_attn_prompt_kernel, slopes=_alibi_slopes()[g], dil=dil, nblk=nblk),
        grid=(dil, nb // nblk),
        in_specs=[cur(0), cur(1), cur(2), prev(1), prev(2)],
        out_specs=[out_spec, out_spec],
        out_shape=[out_sds, out_sds],
        scratch_shapes=[pltpu.VMEM((rows + BAND, ATT_WIDTH), BF16), pltpu.VMEM((rows + BAND, ATT_WIDTH), BF16)],
        compiler_params=_cparams(2),
        name=f"attn_prompt_g{g}",
    )(av, av, av, av, av)
    return o.reshape(T, ATT_WIDTH), l.reshape(T, ATT_WIDTH)


def _attn_sample_kernel(q_ref, k_ref, v_ref, c_ref, o_in, l_in, o_ref, l_ref,
                        *, slopes, window, dil, dec_seq, nsub):
    del o_in, l_in
    R = c_ref.shape[1]
    wb = window
    stride = wb // R
    rows8 = 2 * dec_seq
    scale = HEAD_DIM ** -0.5
    ncache = R * nsub

    ri = lax.broadcasted_iota(jnp.int32, (rows8, ncache), 0)
    ji = lax.broadcasted_iota(jnp.int32, (rows8, ncache), 1)
    t_row = ri % dec_seq
    c_true = (ji % R) * stride + ji // R
    dist_c = wb + t_row - c_true
    ok_c = ((dist_c & (dil - 1)) == 0) & (dist_c <= window)
    bias_c_base = jnp.where(ok_c, 0.0, NEG)
    dist_c_f = dist_c.astype(F32)

    rn = lax.broadcasted_iota(jnp.int32, (rows8, rows8), 0)
    jn = lax.broadcasted_iota(jnp.int32, (rows8, rows8), 1)
    dist_n = rn % dec_seq - jn % dec_seq
    ok_n = (rn // dec_seq == jn // dec_seq) & (dist_n >= 0) & ((dist_n & (dil - 1)) == 0)
    bias_n_base = jnp.where(ok_n, 0.0, NEG)
    dist_n_f = dist_n.astype(F32)
    row_seq = lax.broadcasted_iota(jnp.int32, (rows8, HEAD_DIM), 0) // dec_seq

    for h in range(HEADS):
        cs = slice(h * HEAD_DIM, (h + 1) * HEAD_DIM)
        q = q_ref[:, cs].astype(BF16)
        kn = k_ref[:, cs].astype(BF16)
        vn = v_ref[:, cs].astype(BF16)
        s_n = lax.dot_general(q, kn, _NT, preferred_element_type=F32) * scale - dist_n_f * slopes[h] + bias_n_base
        o_h = jnp.zeros((rows8, HEAD_DIM), F32)
        l_h = jnp.zeros((rows8, HEAD_DIM), F32)
        for bb in range(2):
            kc = jnp.concatenate(
                [c_ref[bb, :, u * 2 * ATT_WIDTH + h * HEAD_DIM: u * 2 * ATT_WIDTH + (h + 1) * HEAD_DIM]
                 for u in range(nsub)], axis=0).astype(BF16)
            vc = jnp.concatenate(
                [c_ref[bb, :, u * 2 * ATT_WIDTH + ATT_WIDTH + h * HEAD_DIM:
                       u * 2 * ATT_WIDTH + ATT_WIDTH + (h + 1) * HEAD_DIM]
                 for u in range(nsub)], axis=0).astype(BF16)
            s_c = lax.dot_general(q, kc, _NT, preferred_element_type=F32) * scale - dist_c_f * slopes[h] + bias_c_base
            m = jnp.maximum(jnp.max(s_c, axis=-1, keepdims=True), jnp.max(s_n, axis=-1, keepdims=True))
            p_c = jnp.exp(s_c - m)
            p_n = jnp.exp(s_n - m)
            l = jnp.sum(p_c, axis=-1, keepdims=True) + jnp.sum(p_n, axis=-1, keepdims=True)
            o = (jnp.dot(p_c.astype(BF16), vc, preferred_element_type=F32)
                 + jnp.dot(p_n.astype(BF16), vn, preferred_element_type=F32)) / l
            mine = row_seq == bb
            o_h = jnp.where(mine, o, o_h)
            l_h = jnp.where(mine, jnp.broadcast_to(m + jnp.log(l), (rows8, HEAD_DIM)), l_h)
        o_ref[:, cs] = o_h
        l_ref[:, cs] = l_h


def _attn_sample(a_mat, cache_l, o_prev, l_prev, g, seq, dec_seq):
    window, dil = ATT_GROUPS[g]
    B, wb = cache_l.shape[0], cache_l.shape[1]
    assert wb == window and B % 2 == 0 and dec_seq == 4 and seq % (2 * dec_seq) == 0
    rows8 = 2 * dec_seq
    R = min(wb, BAND)
    stride = wb // R
    nsub = min(stride, dec_seq)
    cv = cache_l.reshape(B, R, stride * 2 * ATT_WIDTH)
    base = seq // rows8
    c0 = 3 * g

    def new(c):
        return pl.BlockSpec((rows8, ATT_WIDTH), lambda i: (base + i, c0 + c))

    out_spec = pl.BlockSpec((rows8, ATT_WIDTH), lambda i: (base + i, 0))
    any_spec = pl.BlockSpec(memory_space=pl.ANY)
    return pl.pallas_call(
        functools.partial(_attn_sample_kernel, slopes=_alibi_slopes()[g], window=window, dil=dil,
                          dec_seq=dec_seq, nsub=nsub),
        grid=(B // 2,),
        in_specs=[new(0), new(1), new(2),
                  pl.BlockSpec((2, R, nsub * 2 * ATT_WIDTH), lambda i: (i, 0, 0)),
                  any_spec, any_spec],
        out_specs=[out_spec, out_spec],
        out_shape=[jax.ShapeDtypeStruct(o_prev.shape, F32), jax.ShapeDtypeStruct(l_prev.shape, F32)],
        input_output_aliases={4: 0, 5: 1},
        compiler_params=_cparams(1),
        name=f"attn_sample_g{g}",
    )(a_mat, a_mat, a_mat, cv, o_prev, l_prev)


def _log_decay(ga, wa, ba):
    z = jnp.dot(ga, wa, preferred_element_type=F32, precision=lax.Precision.HIGHEST) + ba
    return (jnp.minimum(z, 0.0) - jnp.log(1.0 + jnp.exp(-jnp.abs(z)))) * (1.0 / GLA_TAU)


def _gla_prompt_kernel(q_ref, k_ref, v_ref, ga_ref, wa_ref, ba_ref, o_ref, sfin_ref, st_ref, *, nchunk):
    step = pl.program_id(0)
    C, SB = GLA_CHUNK, GLA_SUB

    @pl.when(step == 0)
    def _():
        st_ref[...] = jnp.zeros_like(st_ref)

    r_i = lax.broadcasted_iota(jnp.int32, (C, C), 0)
    c_i = lax.broadcasted_iota(jnp.int32, (C, C), 1)
    tri = (r_i >= c_i).astype(F32)
    diag = (r_i // SB == c_i // SB) & (c_i <= r_i)
    rowblk = lax.broadcasted_iota(jnp.int32, (C, GLA_HK), 0) // SB
    nsb = C // SB

    def chunk(c, carry):
        r0 = pl.multiple_of(c * C, C)
        la = _log_decay(ga_ref[pl.ds(r0, C), :], wa_ref[...], ba_ref[...])
        b_all = jnp.dot(tri, la, preferred_element_type=F32, precision=lax.Precision.HIGHEST)
        for h in range(GLA_HEADS):
            ks = slice(h * GLA_HK, (h + 1) * GLA_HK)
            vs = slice(h * GLA_HV, (h + 1) * GLA_HV)
            b = b_all[:, ks]
            q = q_ref[pl.ds(r0, C), ks] * (GLA_HK ** -0.5)
            k = k_ref[pl.ds(r0, C), ks]
            v = v_ref[pl.ds(r0, C), vs].astype(BF16)
            st = st_ref[h]
            bl = b[C - 1:C, :]
            inter = lax.dot_general((q * jnp.exp(b)).astype(BF16), st.astype(BF16), _NT,
                                    preferred_element_type=F32)
            bstart = jnp.zeros_like(b)
            bend = jnp.broadcast_to(b[SB - 1:SB, :], b.shape)
            for i in range(1, nsb):
                bstart = jnp.where(rowblk == i, b[i * SB - 1:i * SB, :], bstart)
                bend = jnp.where(rowblk == i, b[(i + 1) * SB - 1:(i + 1) * SB, :], bend)
            qd = q * jnp.exp(b - bstart)
            kd = (k * jnp.exp(bstart - b)).astype(BF16)
            ke = k * jnp.exp(bend - b)
            att = jnp.where(diag, lax.dot_general(qd.astype(BF16), kd, _NT, preferred_element_type=F32), 0.0)
            for j in range(nsb - 1):
                bend_j = b[(j + 1) * SB - 1:(j + 1) * SB, :]
                dj = jnp.where(rowblk > j, jnp.exp(jnp.minimum(bstart - bend_j, 0.0)), 0.0)
                qj = (qd * dj).astype(BF16)
                kej = jnp.where(rowblk == j, ke, 0.0).astype(BF16)
                att = att + lax.dot_general(qj, kej, _NT, preferred_element_type=F32)
            intra = jnp.dot(att.astype(BF16), v, preferred_element_type=F32)
            o_ref[pl.ds(r0, C), vs] = inter + intra
            kdec = (k * jnp.exp(bl - b)).astype(BF16)
            upd = lax.dot_general(v, kdec, _TN, preferred_element_type=F32)
            st_ref[h] = st * jnp.exp(bl) + upd
        return carry

    lax.fori_loop(0, nchunk, chunk, 0)

    @pl.when(step == pl.num_programs(0) - 1)
    def _():
        for h in range(GLA_HEADS):
            sfin_ref[h] = st_ref[h].T


def _gla_prompt(b1, b2, wa_pad, ba, seq):
    T = b1.shape[0]
    rows = 512 if seq % 512 == 0 else GLA_CHUNK
    assert seq % rows == 0 and rows % GLA_CHUNK == 0
    og, sfin = pl.pallas_call(
        functools.partial(_gla_prompt_kernel, nchunk=rows // GLA_CHUNK),
        grid=(seq // rows,),
        in_specs=[pl.BlockSpec((rows, GLA_DK), lambda i: (i, 0)),
                  pl.BlockSpec((rows, GLA_DK), lambda i: (i, 1)),
                  pl.BlockSpec((rows, GLA_DV), lambda i: (i, 0)),
                  pl.BlockSpec((rows, LANE), lambda i: (i, 2 * GLA_DK // LANE)),
                  pl.BlockSpec((LANE, GLA_DK), lambda i: (0, 0)),
                  pl.BlockSpec((1, GLA_DK), lambda i: (0, 0))],
        out_specs=[pl.BlockSpec((rows, GLA_DV), lambda i: (i, 0)),
                   pl.BlockSpec((GLA_HEADS, GLA_HK, GLA_HV), lambda i: (0, 0, 0))],
        out_shape=[jax.ShapeDtypeStruct((T, GLA_DV), F32),
                   jax.ShapeDtypeStruct((GLA_HEADS, GLA_HK, GLA_HV), F32)],
        scratch_shapes=[pltpu.VMEM((GLA_HEADS, GLA_HV, GLA_HK), F32)],
        compiler_params=_cparams(1),
        name="gla_prompt",
    )(b2, b2, b1, b2, wa_pad, ba)
    return og, sfin


def _gla_sample_kernel(q_ref, k_ref, v_ref, ga_ref, wa_ref, ba_ref, s0_ref, og_in, o_ref, snew_ref, *, dec_seq):
    del og_in
    rows8 = 2 * dec_seq
    r_i = lax.broadcasted_iota(jnp.int32, (rows8, rows8), 0)
    c_i = lax.broadcasted_iota(jnp.int32, (rows8, rows8), 1)
    same = (r_i // dec_seq == c_i // dec_seq) & (c_i <= r_i)
    tri = same.astype(F32)
    la = _log_decay(ga_ref[...], wa_ref[...], ba_ref[...])
    b_all = jnp.dot(tri, la, preferred_element_type=F32, precision=lax.Precision.HIGHEST)
    row_seq_k = lax.broadcasted_iota(jnp.int32, (rows8, GLA_HK), 0) // dec_seq
    row_seq_v = lax.broadcasted_iota(jnp.int32, (rows8, GLA_HV), 0) // dec_seq
    for h in range(GLA_HEADS):
        ks = slice(h * GLA_HK, (h + 1) * GLA_HK)
        vs = slice(h * GLA_HV, (h + 1) * GLA_HV)
        b = b_all[:, ks]
        q = q_ref[:, ks] * (GLA_HK ** -0.5)
        k = k_ref[:, ks]
        v = v_ref[:, vs].astype(BF16)
        qd = (q * jnp.exp(b)).astype(BF16)
        kd = (k * jnp.exp(-b)).astype(BF16)
        att = jnp.where(same, lax.dot_general(qd, kd, _NT, preferred_element_type=F32), 0.0)
        o = jnp.dot(att.astype(BF16), v, preferred_element_type=F32)
        for bb in range(2):
            st = s0_ref[bb, h].T
            inter = lax.dot_general(qd, st.astype(BF16), _NT, preferred_element_type=F32)
            o = o + jnp.where(row_seq_v == bb, inter, 0.0)
            bl = b[(bb + 1) * dec_seq - 1:(bb + 1) * dec_seq, :]
            kdec = jnp.where(row_seq_k == bb, k * jnp.exp(bl - b), 0.0).astype(BF16)
            upd = lax.dot_general(v, kdec, _TN, preferred_element_type=F32)
            snew_ref[bb, h] = (st * jnp.exp(bl) + upd).T
        o_ref[:, vs] = o


def _gla_sample(b1, b2, wa_pad, ba, state_l, og_prev, seq, dec_seq):
    B = state_l.shape[0]
    rows8 = 2 * dec_seq
    base = seq // rows8
    st_spec = pl.BlockSpec((2, GLA_HEADS, GLA_HK, GLA_HV), lambda i: (i, 0, 0, 0))
    return pl.pallas_call(
        functools.partial(_gla_sample_kernel, dec_seq=dec_seq),
        grid=(B // 2,),
        in_specs=[pl.BlockSpec((rows8, GLA_DK), lambda i: (base + i, 0)),
                  pl.BlockSpec((rows8, GLA_DK), lambda i: (base + i, 1)),
                  pl.BlockSpec((rows8, GLA_DV), lambda i: (base + i, 0)),
                  pl.BlockSpec((rows8, LANE), lambda i: (base + i, 2 * GLA_DK // LANE)),
                  pl.BlockSpec((LANE, GLA_DK), lambda i: (0, 0)),
                  pl.BlockSpec((1, GLA_DK), lambda i: (0, 0)),
                  st_spec,
                  pl.BlockSpec(memory_space=pl.ANY)],
        out_specs=[pl.BlockSpec((rows8, GLA_DV), lambda i: (base + i, 0)), st_spec],
        out_shape=[jax.ShapeDtypeStruct(og_prev.shape, F32), jax.ShapeDtypeStruct(state_l.shape, F32)],
        input_output_aliases={7: 0},
        compiler_params=_cparams(1),
        name="gla_sample",
    )(b2, b2, b1, b2, wa_pad, ba, state_l, og_prev)


def _post_kernel(o1, o2, o3, l1, l2, l3, og_ref, gr_ref, ga_ref, gb_ref, x_ref,
                 wpa_ref, wpb_ref, wout_ref, gn_ref, lg_ref, lb_ref, xo_ref, xb_ref, *, alpha):
    m = jnp.maximum(jnp.maximum(l1[...], l2[...]), l3[...])
    e1 = jnp.exp(l1[...] - m)
    e2 = jnp.exp(l2[...] - m)
    e3 = jnp.exp(l3[...] - m)
    att = (e1 * o1[...] + e2 * o2[...] + e3 * o3[...]) / (e1 + e2 + e3)
    ya = jnp.dot(att.astype(BF16), wpa_ref[...], preferred_element_type=F32)
    parts = []
    for h in range(GLA_HEADS):
        vs = slice(h * GLA_HV, (h + 1) * GLA_HV)
        og = og_ref[:, vs]
        ms = jnp.mean(og * og, axis=-1, keepdims=True)
        parts.append((og * lax.rsqrt(ms + RMS_EPS) * gn_ref[:, vs] * _silu(gr_ref[:, vs])).astype(BF16))
    y = jnp.concatenate(parts, axis=-1)
    yb = jnp.dot(y, wpb_ref[...], preferred_element_type=F32)
    merged = _sigmoid(ga_ref[...]) * ya + _sigmoid(gb_ref[...]) * yb
    mix = jnp.dot(merged.astype(BF16), wout_ref[...], preferred_element_type=F32)
    xn = _layer_norm_rows(alpha * x_ref[...] + mix, lg_ref[...], lb_ref[...])
    xo_ref[...] = xn
    xb_ref[...] = xn.astype(BF16)


def _post(os_, ls_, og, b1, x, wpa, wpb, wout, gn, lg, lb, alpha, tm):
    T, D = x.shape
    row = lambda w: pl.BlockSpec((tm, w), lambda i: (i, 0))
    col = lambda c: pl.BlockSpec((tm, D), lambda i: (i, c))
    full = lambda a: pl.BlockSpec(a.shape, lambda i: (0,) * a.ndim)
    return pl.pallas_call(
        functools.partial(_post_kernel, alpha=alpha),
        grid=(pl.cdiv(T, tm),),
        in_specs=[row(ATT_WIDTH)] * 6 + [row(GLA_DV), col(1), col(2), col(3), row(D),
                                          full(wpa), full(wpb), full(wout), full(gn), full(lg), full(lb)],
        out_specs=[row(D), row(D)],
        out_shape=[jax.ShapeDtypeStruct((T, D), F32), jax.ShapeDtypeStruct((T, D), BF16)],
        compiler_params=_cparams(1),
        name="post_mixer",
    )(*os_, *ls_, og, b1, b1, b1, x, wpa, wpb, wout, gn, lg, lb)


def _ffn_kernel(xb_ref, x_ref, wg_ref, wu_ref, wd_ref, lg_ref, lb_ref, xo_ref, xbo_ref, acc_ref, *, alpha):
    j = pl.program_id(1)
    xb = xb_ref[...]
    g = jnp.dot(xb, wg_ref[...], preferred_element_type=F32)
    u = jnp.dot(xb, wu_ref[...], preferred_element_type=F32)
    part = jnp.dot((_silu(g) * u).astype(BF16), wd_ref[...], preferred_element_type=F32)

    @pl.when(j == 0)
    def _():
        acc_ref[...] = part

    @pl.when(j > 0)
    def _():
        acc_ref[...] += part

    @pl.when(j == pl.num_programs(1) - 1)
    def _():
        xn = _layer_norm_rows(alpha * x_ref[...] + acc_ref[...], lg_ref[...], lb_ref[...])
        xo_ref[...] = xn
        xbo_ref[...] = xn.astype(BF16)


def _ffn(xb, x, wgu, wd, lg, lb, alpha, tm, tc):
    T, D = x.shape
    dff = wd.shape[0]
    assert dff % tc == 0
    nj = dff // tc
    full = lambda a: pl.BlockSpec(a.shape, lambda i, j: (0,) * a.ndim)
    return pl.pallas_call(
        functools.partial(_ffn_kernel, alpha=alpha),
        grid=(pl.cdiv(T, tm), nj),
        in_specs=[pl.BlockSpec((tm, D), lambda i, j: (i, 0)),
                  pl.BlockSpec((tm, D), lambda i, j: (i, 0)),
                  pl.BlockSpec((D, tc), lambda i, j: (0, j)),
                  pl.BlockSpec((D, tc), lambda i, j: (0, nj + j)),
                  pl.BlockSpec((tc, D), lambda i, j: (j, 0)),
                  full(lg), full(lb)],
        out_specs=[pl.BlockSpec((tm, D), lambda i, j: (i, 0))] * 2,
        out_shape=[jax.ShapeDtypeStruct((T, D), F32), jax.ShapeDtypeStruct((T, D), BF16)],
        scratch_shapes=[pltpu.VMEM((tm, D), F32)],
        compiler_params=_cparams(2),
        name="ffn_dense",
    )(xb, x, wgu, wgu, wd, lg, lb)


def _router_kernel(x_ref, wr_ref, g_ref):
    logits = jnp.dot(x_ref[...], wr_ref[...], preferred_element_type=F32, precision=lax.Precision.HIGHEST)
    lane = lax.broadcasted_iota(jnp.int32, logits.shape, 1).astype(F32)
    lg = jnp.where(lane < N_EXPERTS, logits, NEG)
    m1 = jnp.max(lg, axis=-1, keepdims=True)
    i1 = jnp.min(jnp.where(lg == m1, lane, float(LANE)), axis=-1, keepdims=True)
    lg2 = jnp.where(lane == i1, NEG, lg)
    m2 = jnp.max(lg2, axis=-1, keepdims=True)
    i2 = jnp.min(jnp.where(lg2 == m2, lane, float(LANE)), axis=-1, keepdims=True)
    e2 = jnp.exp(m2 - m1)
    g1 = 1.0 / (1.0 + e2)
    g2 = e2 / (1.0 + e2)
    g_ref[...] = jnp.where(lane == i1, g1, jnp.where(lane == i2, g2, 0.0))


def _router(x, wr_pad, tm):
    T, D = x.shape
    return pl.pallas_call(
        _router_kernel,
        grid=(pl.cdiv(T, tm),),
        in_specs=[pl.BlockSpec((tm, D), lambda i: (i, 0)), pl.BlockSpec(wr_pad.shape, lambda i: (0, 0))],
        out_specs=pl.BlockSpec((tm, LANE), lambda i: (i, 0)),
        out_shape=jax.ShapeDtypeStruct((T, LANE), F32),
        compiler_params=_cparams(1),
        name="moe_router",
    )(x, wr_pad)


def _moe_kernel(xb_ref, x_ref, gate_ref, wg_ref, wu_ref, wd_ref, lg_ref, lb_ref, xo_ref, xbo_ref, acc_ref, *, alpha):
    e = pl.program_id(1)
    j = pl.program_id(2)
    xb = xb_ref[...]
    lane = lax.broadcasted_iota(jnp.int32, gate_ref.shape, 1)
    gate = jnp.sum(jnp.where(lane == e, gate_ref[...], 0.0), axis=-1, keepdims=True)
    g = jnp.dot(xb, wg_ref[...], preferred_element_type=F32)
    u = jnp.dot(xb, wu_ref[...], preferred_element_type=F32)
    part = gate * jnp.dot((_silu(g) * u).astype(BF16), wd_ref[...], preferred_element_type=F32)
    first = (e == 0) & (j == 0)

    @pl.when(first)
    def _():
        acc_ref[...] = part

    @pl.when(jnp.logical_not(first))
    def _():
        acc_ref[...] += part

    @pl.when((e == pl.num_programs(1) - 1) & (j == pl.num_programs(2) - 1))
    def _():
        xn = _layer_norm_rows(alpha * x_ref[...] + acc_ref[...], lg_ref[...], lb_ref[...])
        xo_ref[...] = xn
        xbo_ref[...] = xn.astype(BF16)


def _moe(xb, x, gates, wgu, wd, lg, lb, alpha, tm, tc):
    T, D = x.shape
    ne, dff = wd.shape[0], wd.shape[1]
    nj = dff // tc
    full = lambda a: pl.BlockSpec(a.shape, lambda i, e, j: (0,) * a.ndim)
    return pl.pallas_call(
        functools.partial(_moe_kernel, alpha=alpha),
        grid=(pl.cdiv(T, tm), ne, nj),
        in_specs=[pl.BlockSpec((tm, D), lambda i, e, j: (i, 0)),
                  pl.BlockSpec((tm, D), lambda i, e, j: (i, 0)),
                  pl.BlockSpec((tm, LANE), lambda i, e, j: (i, 0)),
                  pl.BlockSpec((None, D, tc), lambda i, e, j: (e, 0, j)),
                  pl.BlockSpec((None, D, tc), lambda i, e, j: (e, 0, nj + j)),
                  pl.BlockSpec((None, tc, D), lambda i, e, j: (e, j, 0)),
                  full(lg), full(lb)],
        out_specs=[pl.BlockSpec((tm, D), lambda i, e, j: (i, 0))] * 2,
        out_shape=[jax.ShapeDtypeStruct((T, D), F32), jax.ShapeDtypeStruct((T, D), BF16)],
        scratch_shapes=[pltpu.VMEM((tm, D), F32)],
        compiler_params=_cparams(3),
        name="moe_dense",
    )(xb, x, gates, wgu, wgu, wd, lg, lb)


def _split_w_in(w):
    a_end = N_GROUPS * 3 * ATT_WIDTH
    gq0, gk0, gv0 = a_end, a_end + GLA_DK, a_end + 2 * GLA_DK
    gr0 = gv0 + GLA_DV
    ga0 = gr0 + GLA_DV
    gta0 = ga0 + GLA_RANK
    wa = w[:, :a_end].astype(BF16)
    wb1 = jnp.concatenate([w[:, gv0:gr0], w[:, gr0:ga0], w[:, gta0:]], axis=1).astype(BF16)
    ga_pad = jnp.pad(w[:, ga0:gta0], ((0, 0), (0, LANE - GLA_RANK)))
    wb2 = jnp.concatenate([w[:, gq0:gk0], w[:, gk0:gv0], ga_pad], axis=1).astype(BF16)
    return wa, wb1, wb2


def kernel(x_prompt, x_sample, cache_kv_w128, cache_kv_w512, cache_kv_w2048, state_gla,
           w_in, w_alpha2, b_alpha, gla_norm_g, w_pa, w_pb, w_out,
           ln_mix_g, ln_mix_b, ln_ffn_g, ln_ffn_b,
           w_ffn_gu, w_ffn_down, w_router, w_exp_gu, w_exp_down):
    depth = w_in.shape[0]
    bp, seq, D = x_prompt.shape
    db, dec_seq, _ = x_sample.shape
    assert bp == 1
    ns = db * dec_seq
    T = seq + ns
    alpha = (2 * depth) ** 0.25
    caches = (cache_kv_w128, cache_kv_w512, cache_kv_w2048)

    x = jnp.concatenate([x_prompt.reshape(seq, D), x_sample.reshape(ns, D)], axis=0)
    xb = x.astype(BF16)
    kv_p = [[] for _ in range(N_GROUPS)]
    kv_s = [[] for _ in range(N_GROUPS)]
    st_p, st_s = [], []
    row2 = lambda v: v.reshape(1, -1)

    for l in range(depth):
        wa, wb1, wb2 = _split_w_in(w_in[l])
        a_mat = _mm(xb, wa, 1024, 1152)
        b1 = _mm(xb, wb1, 1024, 1024)
        b2 = _mm(xb, wb2, 1024, wb2.shape[1])

        os_, ls_ = [], []
        for g in range(N_GROUPS):
            o, lse = _attn_prompt(a_mat, g, seq)
            o, lse = _attn_sample(a_mat, caches[g][l], o, lse, g, seq, dec_seq)
            os_.append(o)
            ls_.append(lse)
            window = ATT_GROUPS[g][0]
            keep = min(window, seq)
            kcol = a_mat[:, (3 * g + 1) * ATT_WIDTH:(3 * g + 2) * ATT_WIDTH]
            vcol = a_mat[:, (3 * g + 2) * ATT_WIDTH:(3 * g + 3) * ATT_WIDTH]
            kv = jnp.stack([kcol.reshape(T, HEADS, HEAD_DIM), vcol.reshape(T, HEADS, HEAD_DIM)], axis=1)
            kv_p[g].append(kv[seq - keep:seq][None])
            kv_s[g].append(kv[seq:].reshape(db, dec_seq, 2, HEADS, HEAD_DIM))

        wa_pad = jnp.pad(w_alpha2[l], ((0, LANE - GLA_RANK), (0, 0)))
        og, s_p = _gla_prompt(b1, b2, wa_pad, row2(b_alpha[l]), seq)
        og, s_s = _gla_sample(b1, b2, wa_pad, row2(b_alpha[l]), state_gla[l], og, seq, dec_seq)
        st_p.append(s_p[None])
        st_s.append(s_s)

        x, xb = _post(os_, ls_, og, b1, x, w_pa[l].astype(BF16), w_pb[l].astype(BF16), w_out[l].astype(BF16),
                      row2(gla_norm_g[l]), row2(ln_mix_g[l]), row2(ln_mix_b[l]), alpha, 256)

        if l % 2 == 0:
            x, xb = _ffn(xb, x, w_ffn_gu[l // 2].astype(BF16), w_ffn_down[l // 2].astype(BF16),
                         row2(ln_ffn_g[l]), row2(ln_ffn_b[l]), alpha, 512, 1408)
        else:
            wr_pad = jnp.pad(w_router[l // 2], ((0, 0), (0, LANE - N_EXPERTS)))
            gates = _router(x, wr_pad, 512)
            x, xb = _moe(xb, x, gates, w_exp_gu[l // 2].astype(BF16), w_exp_down[l // 2].astype(BF16),
                         row2(ln_ffn_g[l]), row2(ln_ffn_b[l]), alpha, 512, 896)

    return (x[:seq].reshape(1, seq, D), x[seq:].reshape(db, dec_seq, D),
            jnp.stack(kv_p[0], 0), jnp.stack(kv_s[0], 0),
            jnp.stack(kv_p[1], 0), jnp.stack(kv_s[1], 0),
            jnp.stack(kv_p[2], 0), jnp.stack(kv_s[2], 0),
            jnp.stack(st_p, 0), jnp.stack(st_s, 0))
```

```python
import functools

import jax
import jax.numpy as jnp
from jax import lax
from jax.experimental import pallas as pl
from jax.experimental.pallas import tpu as pltpu

F32 = jnp.float32
BF16 = jnp.bfloat16

ATT_GROUPS = ((128, 1), (512, 4), (2048, 16))
N_GROUPS = 3
HEADS = 4
HEAD_DIM = 128
ATT_WIDTH = HEADS * HEAD_DIM
BAND = 128
GLA_HEADS = 4
GLA_HK = 128
GLA_HV = 256
GLA_DK = GLA_HEADS * GLA_HK
GLA_DV = GLA_HEADS * GLA_HV
GLA_RANK = 16
GLA_TAU = 16.0
GLA_CHUNK = 128
GLA_SUB = 32
N_EXPERTS = 8
LN_EPS = 1e-5
RMS_EPS = 1e-6
NEG = -1e30
LANE = 128
SUBLANES = 8
RUNS = 16
RUN_LEN = 128
CHUNK = RUNS * RUN_LEN
VMEM_LIMIT = 56 * 1024 * 1024

_NT = (((1,), (1,)), ((), ()))
_TN = (((0,), (0,)), ((), ()))


def _alibi_slopes():
    n = N_GROUPS * HEADS
    return [[2.0 ** (-8.0 * (g * HEADS + h + 1) / n) for h in range(HEADS)] for g in range(N_GROUPS)]


def _cparams(n_axes):
    return pltpu.CompilerParams(dimension_semantics=("arbitrary",) * n_axes, vmem_limit_bytes=VMEM_LIMIT)


def _layer_norm_rows(r, g, b):
    mu = jnp.mean(r, axis=-1, keepdims=True)
    c = r - mu
    var = jnp.mean(c * c, axis=-1, keepdims=True)
    return c * lax.rsqrt(var + LN_EPS) * g + b


def _sigmoid(x):
    return 1.0 / (1.0 + jnp.exp(-x))


def _silu(x):
    return x * _sigmoid(x)


def _interleave_rows(x):
    s, d = x.shape
    return x.reshape(s // CHUNK, RUN_LEN, RUNS, d).transpose(0, 2, 1, 3).reshape(s, d)


def _deinterleave_rows(x):
    s, d = x.shape
    return x.reshape(s // CHUNK, RUNS, RUN_LEN, d).transpose(0, 2, 1, 3).reshape(s, d)


def _mm_kernel(x_ref, w_ref, o_ref):
    o_ref[...] = jnp.dot(x_ref[...], w_ref[...], preferred_element_type=F32)


def _mm(x, w, tm, tn):
    M, K = x.shape
    N = w.shape[1]
    assert N % tn == 0
    return pl.pallas_call(
        _mm_kernel,
        grid=(N // tn, pl.cdiv(M, tm)),
        in_specs=[pl.BlockSpec((tm, K), lambda j, i: (i, 0)),
                  pl.BlockSpec((K, tn), lambda j, i: (0, j))],
        out_specs=pl.BlockSpec((tm, tn), lambda j, i: (i, j)),
        out_shape=jax.ShapeDtypeStruct((M, N), F32),
        compiler_params=_cparams(2),
        name="in_proj",
    )(x, w)


def _attn_prompt_kernel(slope_ref, q_ref, k_ref, v_ref, o_ref, l_ref, kp_ref, vp_ref, *, dil):
    c = pl.program_id(1)

    @pl.when(c < pl.num_programs(1) - 1)
    def _():
        _attn_prompt_chunk(slope_ref, q_ref, k_ref, v_ref, o_ref, l_ref, kp_ref, vp_ref, dil=dil)

    @pl.when(c == pl.num_programs(1) - 1)
    def _():
        o_ref[...] = jnp.zeros_like(o_ref)
        l_ref[...] = jnp.zeros_like(l_ref)


def _attn_prompt_chunk(slope_ref, q_ref, k_ref, v_ref, o_ref, l_ref, kp_ref, vp_ref, *, dil):
    h = pl.program_id(0)
    c = pl.program_id(1)
    G = RUNS // dil
    wq = BAND // G
    scale = HEAD_DIM ** -0.5

    @pl.when(c == 0)
    def _():
        kp_ref[...] = jnp.zeros_like(kp_ref)
        vp_ref[...] = jnp.zeros_like(vp_ref)

    qi = lax.broadcasted_iota(jnp.int32, (BAND, 2 * BAND), 0)
    ci = lax.broadcasted_iota(jnp.int32, (BAND, 2 * BAND), 1)
    e = ci % (2 * wq)
    rel = (G * (qi % wq) + qi // wq) - (G * (e - wq) + ci // (2 * wq))
    valid = (rel >= 0) & (rel <= BAND)
    bias = rel.astype(F32) * (-slope_ref[h] * dil)
    bias_in = jnp.where(valid, bias, NEG)
    first_e = jnp.where(c > 0, 0, wq)
    bias_first = jnp.where(valid & (e >= first_e), bias, NEG)

    for r in range(dil):
        runs = [r + dil * v for v in range(G)]
        for m in range(G):
            q = jnp.concatenate([q_ref[rho * RUN_LEN + wq * m: rho * RUN_LEN + wq * (m + 1), :] for rho in runs],
                                axis=0).astype(BF16)
            if m == 0:
                kparts, vparts = [], []
                for rho in runs:
                    kparts += [kp_ref[(rho + 1) * RUN_LEN - wq:(rho + 1) * RUN_LEN, :],
                               k_ref[rho * RUN_LEN: rho * RUN_LEN + wq, :]]
                    vparts += [vp_ref[(rho + 1) * RUN_LEN - wq:(rho + 1) * RUN_LEN, :],
                               v_ref[rho * RUN_LEN: rho * RUN_LEN + wq, :]]
            else:
                kparts = [k_ref[rho * RUN_LEN + wq * (m - 1): rho * RUN_LEN + wq * (m + 1), :] for rho in runs]
                vparts = [v_ref[rho * RUN_LEN + wq * (m - 1): rho * RUN_LEN + wq * (m + 1), :] for rho in runs]
            kk = jnp.concatenate(kparts, axis=0).astype(BF16)
            vv = jnp.concatenate(vparts, axis=0).astype(BF16)
            s = lax.dot_general(q, kk, _NT, preferred_element_type=F32) * scale
            s = s + (bias_first if m == 0 else bias_in)
            mx = jnp.max(s, axis=-1, keepdims=True)
            p = jnp.exp(s - mx)
            l = jnp.sum(p, axis=-1, keepdims=True)
            o = jnp.dot(p.astype(BF16), vv, preferred_element_type=F32) / l
            lse = jnp.broadcast_to(mx + jnp.log(l), (BAND, HEAD_DIM))
            for idx, rho in enumerate(runs):
                rows = slice(rho * RUN_LEN + wq * m, rho * RUN_LEN + wq * (m + 1))
                o_ref[rows, :] = o[idx * wq:(idx + 1) * wq, :]
                l_ref[rows, :] = lse[idx * wq:(idx + 1) * wq, :]

    kp_ref[...] = k_ref[...]
    vp_ref[...] = v_ref[...]


def _attn_prompt(a_mat, g, seq):
    T = a_mat.shape[0]
    _, dil = ATT_GROUPS[g]
    assert seq % CHUNK == 0 and RUNS % dil == 0
    slopes = jnp.asarray(_alibi_slopes()[g], F32)

    def col(c):
        return pl.BlockSpec((CHUNK, HEAD_DIM), lambda h, n: (n, (3 * g + c) * HEADS + h))

    out_spec = pl.BlockSpec((CHUNK, HEAD_DIM), lambda h, n: (n, h))
    out_sds = jax.ShapeDtypeStruct((T, ATT_WIDTH), F32)
    return pl.pallas_call(
        functools.partial(_attn_prompt_kernel, dil=dil),
        grid=(HEADS, seq // CHUNK + 1),
        in_specs=[pl.BlockSpec(memory_space=pltpu.SMEM), col(0), col(1), col(2)],
        out_specs=[out_spec, out_spec],
        out_shape=[out_sds, out_sds],
        scratch_shapes=[pltpu.VMEM((CHUNK, HEAD_DIM), F32), pltpu.VMEM((CHUNK, HEAD_DIM), F32)],
        compiler_params=_cparams(2),
        name=f"attn_prompt_g{g}",
    )(slopes, a_mat, a_mat, a_mat)


def _attn_sample_kernel(q_ref, k_ref, v_ref, c_ref, o_in, l_in, o_ref, l_ref,
                        *, slopes, window, dil, dec_seq, nsub):
    del o_in, l_in
    R = c_ref.shape[1]
    wb = window
    stride = wb // R
    rows8 = 2 * dec_seq
    scale = HEAD_DIM ** -0.5
    ncache = R * nsub

    ri = lax.broadcasted_iota(jnp.int32, (rows8, ncache), 0)
    ji = lax.broadcasted_iota(jnp.int32, (rows8, ncache), 1)
    t_row = ri % dec_seq
    c_true = (ji % R) * stride + ji // R
    dist_c = wb + t_row - c_true
    ok_c = ((dist_c & (dil - 1)) == 0) & (dist_c <= window)
    bias_c_base = jnp.where(ok_c, 0.0, NEG)
    dist_c_f = dist_c.astype(F32)

    rn = lax.broadcasted_iota(jnp.int32, (rows8, rows8), 0)
    jn = lax.broadcasted_iota(jnp.int32, (rows8, rows8), 1)
    dist_n = rn % dec_seq - jn % dec_seq
    ok_n = (rn // dec_seq == jn // dec_seq) & (dist_n >= 0) & ((dist_n & (dil - 1)) == 0)
    bias_n_base = jnp.where(ok_n, 0.0, NEG)
    dist_n_f = dist_n.astype(F32)
    row_seq = lax.broadcasted_iota(jnp.int32, (rows8, HEAD_DIM), 0) // dec_seq

    for h in range(HEADS):
        cs = slice(h * HEAD_DIM, (h + 1) * HEAD_DIM)
        q = q_ref[:, cs].astype(BF16)
        kn = k_ref[:, cs].astype(BF16)
        vn = v_ref[:, cs].astype(BF16)
        s_n = lax.dot_general(q, kn, _NT, preferred_element_type=F32) * scale - dist_n_f * slopes[h] + bias_n_base
        o_h = jnp.zeros((rows8, HEAD_DIM), F32)
        l_h = jnp.zeros((rows8, HEAD_DIM), F32)
        for bb in range(2):
            kc = jnp.concatenate([c_ref[bb, :, u * 2 * HEADS + h, :] for u in range(nsub)], axis=0).astype(BF16)
            vc = jnp.concatenate([c_ref[bb, :, u * 2 * HEADS + HEADS + h, :] for u in range(nsub)],
                                 axis=0).astype(BF16)
            s_c = lax.dot_general(q, kc, _NT, preferred_element_type=F32) * scale - dist_c_f * slopes[h] + bias_c_base
            m = jnp.maximum(jnp.max(s_c, axis=-1, keepdims=True), jnp.max(s_n, axis=-1, keepdims=True))
            p_c = jnp.exp(s_c - m)
            p_n = jnp.exp(s_n - m)
            l = jnp.sum(p_c, axis=-1, keepdims=True) + jnp.sum(p_n, axis=-1, keepdims=True)
            o = (jnp.dot(p_c.astype(BF16), vc, preferred_element_type=F32)
                 + jnp.dot(p_n.astype(BF16), vn, preferred_element_type=F32)) / l
            mine = row_seq == bb
            o_h = jnp.where(mine, o, o_h)
            l_h = jnp.where(mine, jnp.broadcast_to(m + jnp.log(l), (rows8, HEAD_DIM)), l_h)
        o_ref[:, cs] = o_h
        l_ref[:, cs] = l_h


def _attn_sample(a_mat, cache, layer, o_prev, l_prev, g, seq, dec_seq):
    window, dil = ATT_GROUPS[g]
    depth, B, wb = cache.shape[0], cache.shape[1], cache.shape[2]
    assert wb == window and B % 2 == 0 and dec_seq == 4 and seq % (2 * dec_seq) == 0
    rows8 = 2 * dec_seq
    R = min(wb, BAND)
    stride = wb // R
    nsub = min(stride, dec_seq)
    cv = cache.reshape(depth, B, R, stride * 2 * HEADS, HEAD_DIM)
    base = seq // rows8
    c0 = 3 * g

    def new(c):
        return pl.BlockSpec((rows8, ATT_WIDTH), lambda i: (base + i, c0 + c))

    out_spec = pl.BlockSpec((rows8, ATT_WIDTH), lambda i: (base + i, 0))
    any_spec = pl.BlockSpec(memory_space=pl.ANY)
    return pl.pallas_call(
        functools.partial(_attn_sample_kernel, slopes=_alibi_slopes()[g], window=window, dil=dil,
                          dec_seq=dec_seq, nsub=nsub),
        grid=(B // 2,),
        in_specs=[new(0), new(1), new(2),
                  pl.BlockSpec((None, 2, R, nsub * 2 * HEADS, HEAD_DIM), lambda i: (layer, i, 0, 0, 0)),
                  any_spec, any_spec],
        out_specs=[out_spec, out_spec],
        out_shape=[jax.ShapeDtypeStruct(o_prev.shape, F32), jax.ShapeDtypeStruct(l_prev.shape, F32)],
        input_output_aliases={4: 0, 5: 1},
        compiler_params=_cparams(1),
        name=f"attn_sample_g{g}",
    )(a_mat, a_mat, a_mat, cv, o_prev, l_prev)


def _log_decay(ga, wa, ba):
    z = jnp.dot(ga, wa, preferred_element_type=F32, precision=lax.Precision.HIGHEST) + ba
    return (jnp.minimum(z, 0.0) - jnp.log(1.0 + jnp.exp(-jnp.abs(z)))) * (1.0 / GLA_TAU)


def _gla_prompt_kernel(q_ref, k_ref, v_ref, ga_ref, wa_ref, ba_ref, o_ref, sfin_ref, st_ref, la_ref):
    c = pl.program_id(1)

    @pl.when(c < pl.num_programs(1) - 1)
    def _():
        _gla_prompt_chunk(q_ref, k_ref, v_ref, ga_ref, wa_ref, ba_ref, o_ref, sfin_ref, st_ref, la_ref)

    @pl.when(c == pl.num_programs(1) - 1)
    def _():
        o_ref[...] = jnp.zeros_like(o_ref)


def _gla_prompt_chunk(q_ref, k_ref, v_ref, ga_ref, wa_ref, ba_ref, o_ref, sfin_ref, st_ref, la_ref):
    c = pl.program_id(1)
    C = GLA_CHUNK
    per_run = C // RUNS
    nsb = C // GLA_SUB
    sub_w = GLA_SUB // RUNS

    @pl.when(c == 0)
    def _():
        st_ref[...] = jnp.zeros_like(st_ref)

    la_ref[...] = _log_decay(ga_ref[...], wa_ref[...], ba_ref[...])

    r_i = lax.broadcasted_iota(jnp.int32, (C, C), 0)
    c_i = lax.broadcasted_iota(jnp.int32, (C, C), 1)
    tok_r = RUNS * (r_i % per_run) + r_i // per_run
    tok_c = RUNS * (c_i % per_run) + c_i // per_run
    tri = (tok_c <= tok_r).astype(F32)
    diag = (tok_r // GLA_SUB == tok_c // GLA_SUB) & (tok_c <= tok_r)
    rowblk = (lax.broadcasted_iota(jnp.int32, (C, GLA_HK), 0) % per_run) // sub_w
    last_run = (RUNS - 1) * per_run

    def gather(ref, j):
        return jnp.concatenate(
            [ref[pl.ds(pl.multiple_of(r * RUN_LEN + j * per_run, per_run), per_run), :] for r in range(RUNS)], axis=0)

    def step(j, carry):
        la = gather(la_ref, j)
        b = jnp.dot(tri, la, preferred_element_type=F32, precision=lax.Precision.HIGHEST)
        q = gather(q_ref, j) * (GLA_HK ** -0.5)
        k = gather(k_ref, j)
        v = gather(v_ref, j).astype(BF16)
        st = st_ref[...]
        bl = b[C - 1:C, :]
        inter = lax.dot_general((q * jnp.exp(b)).astype(BF16), st.astype(BF16), _NT, preferred_element_type=F32)
        bstart = jnp.zeros_like(b)
        bend = jnp.zeros_like(b)
        for i in range(nsb):
            end_row = last_run + (i + 1) * sub_w - 1
            bend = jnp.where(rowblk == i, b[end_row:end_row + 1, :], bend)
            if i > 0:
                bstart = jnp.where(rowblk == i, b[end_row - sub_w:end_row - sub_w + 1, :], bstart)
        qd = q * jnp.exp(b - bstart)
        kd = (k * jnp.exp(bstart - b)).astype(BF16)
        ke = k * jnp.exp(bend - b)
        att = jnp.where(diag, lax.dot_general(qd.astype(BF16), kd, _NT, preferred_element_type=F32), 0.0)
        for i in range(nsb - 1):
            end_row = last_run + (i + 1) * sub_w - 1
            dj = jnp.where(rowblk > i, jnp.exp(jnp.minimum(bstart - b[end_row:end_row + 1, :], 0.0)), 0.0)
            qj = (qd * dj).astype(BF16)
            kej = jnp.where(rowblk == i, ke, 0.0).astype(BF16)
            att = att + lax.dot_general(qj, kej, _NT, preferred_element_type=F32)
        o = inter + jnp.dot(att.astype(BF16), v, preferred_element_type=F32)
        for r in range(RUNS):
            o_ref[pl.ds(pl.multiple_of(r * RUN_LEN + j * per_run, per_run), per_run), :] = \
                o[r * per_run:(r + 1) * per_run, :]
        kdec = (k * jnp.exp(bl - b)).astype(BF16)
        st_ref[...] = st * jnp.exp(bl) + lax.dot_general(v, kdec, _TN, preferred_element_type=F32)
        return carry

    lax.fori_loop(0, CHUNK // C, step, 0, unroll=2)

    @pl.when(c == pl.num_programs(1) - 2)
    def _():
        sfin_ref[...] = st_ref[...].T


def _gla_prompt(b1, b2, wa_pad, ba, seq):
    T = b1.shape[0]
    assert seq % CHUNK == 0
    nq = GLA_DK // GLA_HK
    og, sfin = pl.pallas_call(
        _gla_prompt_kernel,
        grid=(GLA_HEADS, seq // CHUNK + 1),
        in_specs=[pl.BlockSpec((CHUNK, GLA_HK), lambda h, n: (n, h)),
                  pl.BlockSpec((CHUNK, GLA_HK), lambda h, n: (n, nq + h)),
                  pl.BlockSpec((CHUNK, GLA_HV), lambda h, n: (n, h)),
                  pl.BlockSpec((CHUNK, LANE), lambda h, n: (n, 2 * nq)),
                  pl.BlockSpec((LANE, GLA_HK), lambda h, n: (0, h)),
                  pl.BlockSpec((1, GLA_HK), lambda h, n: (0, h))],
        out_specs=[pl.BlockSpec((CHUNK, GLA_HV), lambda h, n: (n, h)),
                   pl.BlockSpec((None, GLA_HK, GLA_HV), lambda h, n: (h, 0, 0))],
        out_shape=[jax.ShapeDtypeStruct((T, GLA_DV), F32),
                   jax.ShapeDtypeStruct((GLA_HEADS, GLA_HK, GLA_HV), F32)],
        scratch_shapes=[pltpu.VMEM((GLA_HV, GLA_HK), F32), pltpu.VMEM((CHUNK, GLA_HK), F32)],
        compiler_params=_cparams(2),
        name="gla_prompt",
    )(b2, b2, b1, b2, wa_pad, ba)
    return og, sfin


def _gla_sample_kernel(q_ref, k_ref, v_ref, ga_ref, wa_ref, ba_ref, s0_ref, og_in, o_ref, snew_ref, *, dec_seq):
    del og_in
    rows8 = 2 * dec_seq
    r_i = lax.broadcasted_iota(jnp.int32, (rows8, rows8), 0)
    c_i = lax.broadcasted_iota(jnp.int32, (rows8, rows8), 1)
    same = (r_i // dec_seq == c_i // dec_seq) & (c_i <= r_i)
    tri = same.astype(F32)
    la = _log_decay(ga_ref[...], wa_ref[...], ba_ref[...])
    b_all = jnp.dot(tri, la, preferred_element_type=F32, precision=lax.Precision.HIGHEST)
    row_seq_k = lax.broadcasted_iota(jnp.int32, (rows8, GLA_HK), 0) // dec_seq
    row_seq_v = lax.broadcasted_iota(jnp.int32, (rows8, GLA_HV), 0) // dec_seq
    for h in range(GLA_HEADS):
        ks = slice(h * GLA_HK, (h + 1) * GLA_HK)
        vs = slice(h * GLA_HV, (h + 1) * GLA_HV)
        b = b_all[:, ks]
        q = q_ref[:, ks] * (GLA_HK ** -0.5)
        k = k_ref[:, ks]
        v = v_ref[:, vs].astype(BF16)
        qd = (q * jnp.exp(b)).astype(BF16)
        kd = (k * jnp.exp(-b)).astype(BF16)
        att = jnp.where(same, lax.dot_general(qd, kd, _NT, preferred_element_type=F32), 0.0)
        o = jnp.dot(att.astype(BF16), v, preferred_element_type=F32)
        for bb in range(2):
            st = s0_ref[bb, h].T
            inter = lax.dot_general(qd, st.astype(BF16), _NT, preferred_element_type=F32)
            o = o + jnp.where(row_seq_v == bb, inter, 0.0)
            bl = b[(bb + 1) * dec_seq - 1:(bb + 1) * dec_seq, :]
            kdec = jnp.where(row_seq_k == bb, k * jnp.exp(bl - b), 0.0).astype(BF16)
            upd = lax.dot_general(v, kdec, _TN, preferred_element_type=F32)
            snew_ref[bb, h] = (st * jnp.exp(bl) + upd).T
        o_ref[:, vs] = o


def _gla_sample(b1, b2, wa_pad, ba, state, layer, og_prev, seq, dec_seq):
    B = state.shape[1]
    rows8 = 2 * dec_seq
    base = seq // rows8
    nq = GLA_DK // GLA_HK
    return pl.pallas_call(
        functools.partial(_gla_sample_kernel, dec_seq=dec_seq),
        grid=(B // 2,),
        in_specs=[pl.BlockSpec((rows8, GLA_DK), lambda i: (base + i, 0)),
                  pl.BlockSpec((rows8, GLA_DK), lambda i: (base + i, 1)),
                  pl.BlockSpec((rows8, GLA_DV), lambda i: (base + i, 0)),
                  pl.BlockSpec((rows8, LANE), lambda i: (base + i, 2 * nq)),
                  pl.BlockSpec((LANE, GLA_DK), lambda i: (0, 0)),
                  pl.BlockSpec((1, GLA_DK), lambda i: (0, 0)),
                  pl.BlockSpec((None, 2, GLA_HEADS, GLA_HK, GLA_HV), lambda i: (layer, i, 0, 0, 0)),
                  pl.BlockSpec(memory_space=pl.ANY)],
        out_specs=[pl.BlockSpec((rows8, GLA_DV), lambda i: (base + i, 0)),
                   pl.BlockSpec((2, GLA_HEADS, GLA_HK, GLA_HV), lambda i: (i, 0, 0, 0))],
        out_shape=[jax.ShapeDtypeStruct(og_prev.shape, F32), jax.ShapeDtypeStruct(state.shape[1:], F32)],
        input_output_aliases={7: 0},
        compiler_params=_cparams(1),
        name="gla_sample",
    )(b2, b2, b1, b2, wa_pad, ba, state, og_prev)


def _post_kernel(o1, o2, o3, l1, l2, l3, og_ref, gr_ref, ga_ref, gb_ref, x_ref,
                 wpa_ref, wpb_ref, wout_ref, gn_ref, lg_ref, lb_ref, xo_ref, xb_ref, *, alpha):
    m = jnp.maximum(jnp.maximum(l1[...], l2[...]), l3[...])
    e1 = jnp.exp(l1[...] - m)
    e2 = jnp.exp(l2[...] - m)
    e3 = jnp.exp(l3[...] - m)
    att = (e1 * o1[...] + e2 * o2[...] + e3 * o3[...]) / (e1 + e2 + e3)
    ya = jnp.dot(att.astype(BF16), wpa_ref[...], preferred_element_type=F32)
    parts = []
    for h in range(GLA_HEADS):
        vs = slice(h * GLA_HV, (h + 1) * GLA_HV)
        og = og_ref[:, vs]
        ms = jnp.mean(og * og, axis=-1, keepdims=True)
        parts.append((og * lax.rsqrt(ms + RMS_EPS) * gn_ref[:, vs] * _silu(gr_ref[:, vs])).astype(BF16))
    y = jnp.concatenate(parts, axis=-1)
    yb = jnp.dot(y, wpb_ref[...], preferred_element_type=F32)
    merged = _sigmoid(ga_ref[...]) * ya + _sigmoid(gb_ref[...]) * yb
    mix = jnp.dot(merged.astype(BF16), wout_ref[...], preferred_element_type=F32)
    xn = _layer_norm_rows(alpha * x_ref[...] + mix, lg_ref[...], lb_ref[...])
    xo_ref[...] = xn
    xb_ref[...] = xn.astype(BF16)


def _post(os_, ls_, og, b1, x, wpa, wpb, wout, gn, lg, lb, alpha, tm):
    T, D = x.shape
    row = lambda w: pl.BlockSpec((tm, w), lambda i: (i, 0))
    col = lambda c: pl.BlockSpec((tm, D), lambda i: (i, c))
    full = lambda a: pl.BlockSpec(a.shape, lambda i: (0,) * a.ndim)
    return pl.pallas_call(
        functools.partial(_post_kernel, alpha=alpha),
        grid=(pl.cdiv(T, tm),),
        in_specs=[row(ATT_WIDTH)] * 6 + [row(GLA_DV), col(1), col(2), col(3), row(D),
                                          full(wpa), full(wpb), full(wout), full(gn), full(lg), full(lb)],
        out_specs=[row(D), row(D)],
        out_shape=[jax.ShapeDtypeStruct((T, D), F32), jax.ShapeDtypeStruct((T, D), BF16)],
        compiler_params=_cparams(1),
        name="post_mixer",
    )(*os_, *ls_, og, b1, b1, b1, x, wpa, wpb, wout, gn, lg, lb)


def _ffn_kernel(xb_ref, x_ref, wg_ref, wu_ref, wd_ref, lg_ref, lb_ref, xo_ref, xbo_ref, acc_ref, *, alpha):
    j = pl.program_id(1)
    xb = xb_ref[...]
    g = jnp.dot(xb, wg_ref[...], preferred_element_type=F32)
    u = jnp.dot(xb, wu_ref[...], preferred_element_type=F32)
    part = jnp.dot((_silu(g) * u).astype(BF16), wd_ref[...], preferred_element_type=F32)

    @pl.when(j == 0)
    def _():
        acc_ref[...] = part

    @pl.when(j > 0)
    def _():
        acc_ref[...] += part

    @pl.when(j == pl.num_programs(1) - 1)
    def _():
        xn = _layer_norm_rows(alpha * x_ref[...] + acc_ref[...], lg_ref[...], lb_ref[...])
        xo_ref[...] = xn
        xbo_ref[...] = xn.astype(BF16)


def _ffn(xb, x, wgu, wd, lg, lb, alpha, tm, tc):
    T, D = x.shape
    dff = wd.shape[0]
    assert dff % tc == 0
    nj = dff // tc
    full = lambda a: pl.BlockSpec(a.shape, lambda i, j: (0,) * a.ndim)
    return pl.pallas_call(
        functools.partial(_ffn_kernel, alpha=alpha),
        grid=(pl.cdiv(T, tm), nj),
        in_specs=[pl.BlockSpec((tm, D), lambda i, j: (i, 0)),
                  pl.BlockSpec((tm, D), lambda i, j: (i, 0)),
                  pl.BlockSpec((D, tc), lambda i, j: (0, j)),
                  pl.BlockSpec((D, tc), lambda i, j: (0, nj + j)),
                  pl.BlockSpec((tc, D), lambda i, j: (j, 0)),
                  full(lg), full(lb)],
        out_specs=[pl.BlockSpec((tm, D), lambda i, j: (i, 0))] * 2,
        out_shape=[jax.ShapeDtypeStruct((T, D), F32), jax.ShapeDtypeStruct((T, D), BF16)],
        scratch_shapes=[pltpu.VMEM((tm, D), F32)],
        compiler_params=_cparams(2),
        name="ffn_dense",
    )(xb, x, wgu, wgu, wd, lg, lb)


MOE_TOKENS = 512
MOE_PIECE = 16
MOE_ROWS = -(-(2 * MOE_TOKENS + N_EXPERTS * (MOE_PIECE - 1)) // MOE_PIECE) * MOE_PIECE
MOE_BLOCK = 1024
MOE_HALF = 512
MOE_SLOTS = MOE_BLOCK // MOE_PIECE


def _moe_route_kernel(x_ref, xb_ref, wrt_ref, xs_ref, meta_ref, cnt_ref, *, n_tokens):
    i = pl.program_id(0)
    ts = x_ref.shape[0]
    row_ok = (lax.broadcasted_iota(jnp.int32, (ts, 1), 0) + i * ts) < n_tokens
    x = jnp.where(row_ok, x_ref[...], 0.0)
    xb = jnp.where(row_ok, xb_ref[...], jnp.zeros((), BF16))
    lt = lax.dot_general(wrt_ref[...], x, _NT, preferred_element_type=F32, precision=lax.Precision.HIGHEST)
    sub = lax.broadcasted_iota(jnp.int32, (N_EXPERTS, ts), 0).astype(F32)
    tok_ok = (lax.broadcasted_iota(jnp.int32, (1, ts), 1) + i * ts) < n_tokens
    m1 = jnp.max(lt, axis=0, keepdims=True)
    i1 = jnp.min(jnp.where(lt == m1, sub, float(N_EXPERTS)), axis=0, keepdims=True)
    lt2 = jnp.where(sub == i1, NEG, lt)
    m2 = jnp.max(lt2, axis=0, keepdims=True)
    i2 = jnp.min(jnp.where(lt2 == m2, sub, float(N_EXPERTS)), axis=0, keepdims=True)
    e2 = jnp.exp(m2 - m1)
    g1 = 1.0 / (1.0 + e2)
    g2 = e2 / (1.0 + e2)
    oh1 = (sub == i1) & tok_ok
    oh2 = (sub == i2) & tok_ok
    assign = jnp.where(oh1 | oh2, 1.0, 0.0)
    before = (lax.broadcasted_iota(jnp.int32, (ts, ts), 0) < lax.broadcasted_iota(jnp.int32, (ts, ts), 1))
    rank = jnp.dot(assign.astype(BF16), jnp.where(before, 1.0, 0.0).astype(BF16), preferred_element_type=F32)
    cnt = jnp.sum(assign, axis=1, keepdims=True)
    padded = jnp.floor((cnt + (MOE_PIECE - 1)) * (1.0 / MOE_PIECE)) * MOE_PIECE
    lower = (lax.broadcasted_iota(jnp.int32, (N_EXPERTS, N_EXPERTS), 1)
             < lax.broadcasted_iota(jnp.int32, (N_EXPERTS, N_EXPERTS), 0)).astype(F32)
    seg_off = jnp.dot(lower, jnp.broadcast_to(padded, (N_EXPERTS, ts)), preferred_element_type=F32,
                      precision=lax.Precision.HIGHEST)
    dest = seg_off + rank
    d1 = jnp.where(tok_ok, jnp.sum(jnp.where(oh1, dest, 0.0), axis=0, keepdims=True), -1.0)
    d2 = jnp.where(tok_ok, jnp.sum(jnp.where(oh2, dest, 0.0), axis=0, keepdims=True), -1.0)
    p_i = lax.broadcasted_iota(jnp.int32, (MOE_ROWS, ts), 0).astype(F32)
    perm = jnp.where((p_i == d1) | (p_i == d2), 1.0, 0.0).astype(BF16)
    xs_ref[...] = jnp.dot(perm, xb, preferred_element_type=F32).astype(BF16)
    sub8 = lax.broadcasted_iota(jnp.int32, (SUBLANES, ts), 0)
    meta_ref[...] = jnp.where(sub8 == 0, d1, jnp.where(sub8 == 1, d2, jnp.where(sub8 == 2, g1,
                              jnp.where(sub8 == 3, g2, 0.0))))
    cnt_ref[...] = jnp.broadcast_to(cnt, (N_EXPERTS, LANE))


def _moe_route(x, xb, wr_t):
    T, D = x.shape
    nt = pl.cdiv(T, MOE_TOKENS)
    return pl.pallas_call(
        functools.partial(_moe_route_kernel, n_tokens=T),
        grid=(nt,),
        in_specs=[pl.BlockSpec((MOE_TOKENS, D), lambda i: (i, 0)),
                  pl.BlockSpec((MOE_TOKENS, D), lambda i: (i, 0)),
                  pl.BlockSpec(wr_t.shape, lambda i: (0, 0))],
        out_specs=[pl.BlockSpec((MOE_ROWS, D), lambda i: (i, 0)),
                   pl.BlockSpec((None, SUBLANES, MOE_TOKENS), lambda i: (i, 0, 0)),
                   pl.BlockSpec((None, N_EXPERTS, LANE), lambda i: (i, 0, 0))],
        out_shape=[jax.ShapeDtypeStruct((nt * MOE_ROWS, D), BF16),
                   jax.ShapeDtypeStruct((nt, SUBLANES, MOE_TOKENS), F32),
                   jax.ShapeDtypeStruct((nt, N_EXPERTS, LANE), F32)],
        compiler_params=_cparams(1),
        name="moe_route",
    )(x, xb, wr_t)


def _moe_plan(cnt, n_blocks, n_pieces):
    nt = cnt.shape[0]
    padded = (cnt + MOE_PIECE - 1) // MOE_PIECE * MOE_PIECE
    seg_row = jnp.cumsum(padded, axis=1) - padded + (jnp.arange(nt, dtype=jnp.int32) * MOE_ROWS)[:, None]
    n_seg = (padded // MOE_PIECE).T.reshape(-1)
    seg_row = seg_row.T.reshape(-1)
    seg_end = jnp.cumsum(n_seg)
    idx = jnp.arange(n_pieces, dtype=jnp.int32)
    seg = jnp.minimum(jnp.searchsorted(seg_end, idx, side="right"), N_EXPERTS * nt - 1).astype(jnp.int32)
    live = idx < seg_end[-1]
    row = seg_row[seg] + MOE_PIECE * (idx - (seg_end - n_seg)[seg])
    e = seg // nt
    per_e = n_seg.reshape(N_EXPERTS, nt).sum(axis=1)
    e_start = jnp.cumsum(per_e) - per_e
    blocks_e = (per_e + MOE_SLOTS - 1) // MOE_SLOTS
    blk_end = jnp.cumsum(blocks_e)
    blk_start = blk_end - blocks_e
    pos = idx - e_start[e]
    flat = jnp.where(live, (blk_start[e] + pos // MOE_SLOTS) * MOE_SLOTS + pos % MOE_SLOTS, n_blocks * MOE_SLOTS)
    rows = jnp.zeros((n_blocks * MOE_SLOTS,), jnp.int32).at[flat].set(row.astype(jnp.int32), mode="drop")
    w = jnp.arange(n_blocks, dtype=jnp.int32)
    ew = jnp.minimum(jnp.searchsorted(blk_end, w, side="right"), N_EXPERTS - 1).astype(jnp.int32)
    n_w = jnp.where(w < blk_end[-1], jnp.clip(per_e[ew] - (w - blk_start[ew]) * MOE_SLOTS, 0, MOE_SLOTS), 0)
    return ew, n_w.astype(jnp.int32), rows


def _moe_expert_kernel(ew_ref, n_ref, rows_ref, xs_hbm, wg_ref, wu_ref, wd_ref, ys_in, ys_hbm,
                       lhs_ref, acc_ref, out_ref, sem_in, sem_out):
    del ew_ref, ys_in
    w = pl.program_id(0)
    j = pl.program_id(1)
    n = n_ref[w]

    def piece_in(q):
        src = pl.multiple_of(rows_ref[w * MOE_SLOTS + q], MOE_PIECE)
        dst = pl.multiple_of(q * MOE_PIECE, MOE_PIECE)
        return pltpu.make_async_copy(xs_hbm.at[pl.ds(src, MOE_PIECE), :], lhs_ref.at[pl.ds(dst, MOE_PIECE), :], sem_in)

    def piece_out(q):
        dst = pl.multiple_of(rows_ref[w * MOE_SLOTS + q], MOE_PIECE)
        src = pl.multiple_of(q * MOE_PIECE, MOE_PIECE)
        return pltpu.make_async_copy(out_ref.at[pl.ds(src, MOE_PIECE), :], ys_hbm.at[pl.ds(dst, MOE_PIECE), :], sem_out)

    @pl.when((w == 0) & (j == 0))
    def _():
        lhs_ref[...] = jnp.zeros_like(lhs_ref)

    @pl.when(j == 0)
    def _():
        lax.fori_loop(0, n, lambda q, c: (piece_in(q).start(), c)[1], 0)
        lax.fori_loop(0, n, lambda q, c: (piece_in(q).wait(), c)[1], 0)

    for s in range(MOE_BLOCK // MOE_HALF):
        @pl.when(n > s * (MOE_HALF // MOE_PIECE))
        def _(s=s):
            rows = slice(s * MOE_HALF, (s + 1) * MOE_HALF)
            xb = lhs_ref[rows, :]
            g = jnp.dot(xb, wg_ref[...], preferred_element_type=F32)
            u = jnp.dot(xb, wu_ref[...], preferred_element_type=F32)
            part = jnp.dot((_silu(g) * u).astype(BF16), wd_ref[...], preferred_element_type=F32)

            @pl.when(j == 0)
            def _():
                acc_ref[rows, :] = part

            @pl.when(j > 0)
            def _():
                acc_ref[rows, :] += part

            @pl.when(j == pl.num_programs(1) - 1)
            def _():
                out_ref[rows, :] = acc_ref[rows, :].astype(BF16)

    @pl.when(j == pl.num_programs(1) - 1)
    def _():
        lax.fori_loop(0, n, lambda q, c: (piece_out(q).start(), c)[1], 0)
        lax.fori_loop(0, n, lambda q, c: (piece_out(q).wait(), c)[1], 0)


def _moe_experts(xs, plan, wgu, wd, tc):
    ew, n_w, rows = plan
    n_blocks = ew.shape[0]
    D = xs.shape[1]
    dff = wd.shape[1]
    assert dff % tc == 0
    nj = dff // tc
    any_spec = pl.BlockSpec(memory_space=pl.ANY)
    return pl.pallas_call(
        _moe_expert_kernel,
        grid_spec=pltpu.PrefetchScalarGridSpec(
            num_scalar_prefetch=3,
            grid=(n_blocks, nj),
            in_specs=[any_spec,
                      pl.BlockSpec((None, D, tc), lambda w, j, ew, n, r: (ew[w], 0, j)),
                      pl.BlockSpec((None, D, tc), lambda w, j, ew, n, r: (ew[w], 0, nj + j)),
                      pl.BlockSpec((None, tc, D), lambda w, j, ew, n, r: (ew[w], j, 0)),
                      any_spec],
            out_specs=any_spec,
            scratch_shapes=[pltpu.VMEM((MOE_BLOCK, D), BF16), pltpu.VMEM((MOE_BLOCK, D), F32),
                            pltpu.VMEM((MOE_BLOCK, D), BF16),
                            pltpu.SemaphoreType.DMA(()), pltpu.SemaphoreType.DMA(())]),
        out_shape=jax.ShapeDtypeStruct(xs.shape, BF16),
        input_output_aliases={7: 0},
        compiler_params=_cparams(2),
        name="moe_experts",
    )(ew, n_w, rows, xs, wgu, wgu, wd, jnp.zeros(xs.shape, BF16))


def _moe_combine_kernel(ys_ref, meta_ref, x_ref, lg_ref, lb_ref, xo_ref, xbo_ref, *, alpha):
    ts = x_ref.shape[0]
    meta = meta_ref[...]
    d1, d2, g1, g2 = meta[0:1, :], meta[1:2, :], meta[2:3, :], meta[3:4, :]
    p_i = lax.broadcasted_iota(jnp.int32, (MOE_ROWS, ts), 0).astype(F32)
    back = (jnp.where(p_i == d1, g1, 0.0) + jnp.where(p_i == d2, g2, 0.0)).astype(BF16)
    f = lax.dot_general(back, ys_ref[...], _TN, preferred_element_type=F32)
    xn = _layer_norm_rows(alpha * x_ref[...] + f, lg_ref[...], lb_ref[...])
    xo_ref[...] = xn
    xbo_ref[...] = xn.astype(BF16)


def _moe_combine(ys, meta, x, lg, lb, alpha):
    T, D = x.shape
    nt = meta.shape[0]
    full = lambda a: pl.BlockSpec(a.shape, lambda i: (0,) * a.ndim)
    return pl.pallas_call(
        functools.partial(_moe_combine_kernel, alpha=alpha),
        grid=(nt,),
        in_specs=[pl.BlockSpec((MOE_ROWS, D), lambda i: (i, 0)),
                  pl.BlockSpec((None, SUBLANES, MOE_TOKENS), lambda i: (i, 0, 0)),
                  pl.BlockSpec((MOE_TOKENS, D), lambda i: (i, 0)),
                  full(lg), full(lb)],
        out_specs=[pl.BlockSpec((MOE_TOKENS, D), lambda i: (i, 0))] * 2,
        out_shape=[jax.ShapeDtypeStruct((T, D), F32), jax.ShapeDtypeStruct((T, D), BF16)],
        compiler_params=_cparams(1),
        name="moe_combine",
    )(ys, meta, x, lg, lb)


def _moe(x, xb, w_router, wgu, wd, lg, lb, alpha):
    T = x.shape[0]
    nt = pl.cdiv(T, MOE_TOKENS)
    xs, meta, cnt = _moe_route(x, xb, w_router.T)
    n_pieces = (2 * T + nt * N_EXPERTS * (MOE_PIECE - 1)) // MOE_PIECE + 1
    n_blocks = pl.cdiv(n_pieces, MOE_SLOTS) + N_EXPERTS
    plan = _moe_plan(cnt[:, :, 0].astype(jnp.int32), n_blocks, n_pieces)
    ys = _moe_experts(xs, plan, wgu, wd, 896)
    return _moe_combine(ys, meta, x, lg, lb, alpha)


def _split_w_in(w):
    a_end = N_GROUPS * 3 * ATT_WIDTH
    gq0, gk0, gv0 = a_end, a_end + GLA_DK, a_end + 2 * GLA_DK
    gr0 = gv0 + GLA_DV
    ga0 = gr0 + GLA_DV
    gta0 = ga0 + GLA_RANK
    wa = w[:, :a_end].astype(BF16)
    wb1 = jnp.concatenate([w[:, gv0:gr0], w[:, gr0:ga0], w[:, gta0:]], axis=1).astype(BF16)
    ga_pad = jnp.pad(w[:, ga0:gta0], ((0, 0), (0, LANE - GLA_RANK)))
    wb2 = jnp.concatenate([w[:, gq0:gk0], w[:, gk0:gv0], ga_pad], axis=1).astype(BF16)
    return wa, wb1, wb2


def _new_kv_rows(a_mat, g, seq, db, dec_seq):
    keep = min(ATT_GROUPS[g][0], seq)
    assert keep <= CHUNK and keep % RUNS == 0
    kv_cols = slice((3 * g + 1) * ATT_WIDTH, (3 * g + 3) * ATT_WIDTH)
    last = a_mat[seq - CHUNK:seq, kv_cols].reshape(RUNS, RUN_LEN, 2, HEADS, HEAD_DIM)
    kv_p = last[:, RUN_LEN - keep // RUNS:].transpose(1, 0, 2, 3, 4).reshape(1, keep, 2, HEADS, HEAD_DIM)
    kv_s = a_mat[seq:, kv_cols].reshape(db, dec_seq, 2, HEADS, HEAD_DIM)
    return kv_p, kv_s


def kernel(x_prompt, x_sample, cache_kv_w128, cache_kv_w512, cache_kv_w2048, state_gla,
           w_in, w_alpha2, b_alpha, gla_norm_g, w_pa, w_pb, w_out,
           ln_mix_g, ln_mix_b, ln_ffn_g, ln_ffn_b,
           w_ffn_gu, w_ffn_down, w_router, w_exp_gu, w_exp_down):
    depth = w_in.shape[0]
    bp, seq, D = x_prompt.shape
    db, dec_seq, _ = x_sample.shape
    assert bp == 1
    ns = db * dec_seq
    alpha = (2 * depth) ** 0.25
    caches = (cache_kv_w128, cache_kv_w512, cache_kv_w2048)

    x = jnp.concatenate([_interleave_rows(x_prompt.reshape(seq, D)), x_sample.reshape(ns, D)], axis=0)
    xb = x.astype(BF16)
    kv_p = [[] for _ in range(N_GROUPS)]
    kv_s = [[] for _ in range(N_GROUPS)]
    st_p, st_s = [], []
    row2 = lambda v: v.reshape(1, -1)

    for l in range(depth):
        wa, wb1, wb2 = _split_w_in(w_in[l])
        a_mat = _mm(xb, wa, 1024, 1152)
        b1 = _mm(xb, wb1, 1024, 1024)
        b2 = _mm(xb, wb2, 1024, wb2.shape[1])

        os_, ls_ = [], []
        for g in range(N_GROUPS):
            o, lse = _attn_prompt(a_mat, g, seq)
            o, lse = _attn_sample(a_mat, caches[g], l, o, lse, g, seq, dec_seq)
            os_.append(o)
            ls_.append(lse)
            p_rows, s_rows = _new_kv_rows(a_mat, g, seq, db, dec_seq)
            kv_p[g].append(p_rows)
            kv_s[g].append(s_rows)

        wa_pad = jnp.pad(w_alpha2[l], ((0, LANE - GLA_RANK), (0, 0)))
        og, s_p = _gla_prompt(b1, b2, wa_pad, row2(b_alpha[l]), seq)
        og, s_s = _gla_sample(b1, b2, wa_pad, row2(b_alpha[l]), state_gla, l, og, seq, dec_seq)
        st_p.append(s_p[None])
        st_s.append(s_s)

        x, xb = _post(os_, ls_, og, b1, x, w_pa[l].astype(BF16), w_pb[l].astype(BF16), w_out[l].astype(BF16),
                      row2(gla_norm_g[l]), row2(ln_mix_g[l]), row2(ln_mix_b[l]), alpha, 256)

        if l % 2 == 0:
            x, xb = _ffn(xb, x, w_ffn_gu[l // 2].astype(BF16), w_ffn_down[l // 2].astype(BF16),
                         row2(ln_ffn_g[l]), row2(ln_ffn_b[l]), alpha, 512, 1408)
        else:
            x, xb = _moe(x, xb, w_router[l // 2], w_exp_gu[l // 2].astype(BF16), w_exp_down[l // 2].astype(BF16),
                         row2(ln_ffn_g[l]), row2(ln_ffn_b[l]), alpha)

    return (_deinterleave_rows(x[:seq]).reshape(1, seq, D), x[seq:].reshape(db, dec_seq, D),
            jnp.stack(kv_p[0], 0), jnp.stack(kv_s[0], 0),
            jnp.stack(kv_p[1], 0), jnp.stack(kv_s[1], 0),
            jnp.stack(kv_p[2], 0), jnp.stack(kv_s[2], 0),
            jnp.stack(st_p, 0), jnp.stack(st_s, 0))
```

```python
import functools

import jax
import jax.numpy as jnp
from jax import lax
from jax.experimental import pallas as pl
from jax.experimental.pallas import tpu as pltpu

F32 = jnp.float32
BF16 = jnp.bfloat16

ATT_GROUPS = ((128, 1), (512, 4), (2048, 16))
N_GROUPS = 3
HEADS = 4
HEAD_DIM = 128
ATT_WIDTH = HEADS * HEAD_DIM
BAND = 128
ATT_GROUP = 4
GLA_HEADS = 4
GLA_HK = 128
GLA_HV = 256
GLA_DK = GLA_HEADS * GLA_HK
GLA_DV = GLA_HEADS * GLA_HV
GLA_RANK = 16
GLA_TAU = 16.0
GLA_CHUNK = 128
GLA_SUB = 32
GLA_GROUP = 8
N_EXPERTS = 8
LN_EPS = 1e-5
RMS_EPS = 1e-6
NEG = -1e30
LANE = 128
SUBLANES = 8
RUNS = 16
RUN_LEN = 128
CHUNK = RUNS * RUN_LEN
VMEM_LIMIT = 56 * 1024 * 1024

_NT = (((1,), (1,)), ((), ()))
_TN = (((0,), (0,)), ((), ()))


def _alibi_slopes():
    n = N_GROUPS * HEADS
    return [[2.0 ** (-8.0 * (g * HEADS + h + 1) / n) for h in range(HEADS)] for g in range(N_GROUPS)]


def _cparams(n_axes):
    return pltpu.CompilerParams(dimension_semantics=("arbitrary",) * n_axes, vmem_limit_bytes=VMEM_LIMIT)


def _layer_norm_rows(r, g, b):
    mu = jnp.mean(r, axis=-1, keepdims=True)
    c = r - mu
    var = jnp.mean(c * c, axis=-1, keepdims=True)
    return c * lax.rsqrt(var + LN_EPS) * g + b


def _sigmoid(x):
    return 1.0 / (1.0 + jnp.exp(-x))


def _silu(x):
    return x * _sigmoid(x)


def _interleave_rows(x):
    s, d = x.shape
    return x.reshape(s // CHUNK, RUN_LEN, RUNS, d).transpose(0, 2, 1, 3).reshape(s, d)


def _deinterleave_rows(x):
    s, d = x.shape
    return x.reshape(s // CHUNK, RUNS, RUN_LEN, d).transpose(0, 2, 1, 3).reshape(s, d)


def _mm_kernel(x_ref, w_ref, o_ref):
    o_ref[...] = jnp.dot(x_ref[...], w_ref[...], preferred_element_type=F32)


def _mm(x, w, tm, tn):
    M, K = x.shape
    N = w.shape[1]
    assert N % tn == 0
    return pl.pallas_call(
        _mm_kernel,
        grid=(N // tn, pl.cdiv(M, tm)),
        in_specs=[pl.BlockSpec((tm, K), lambda j, i: (i, 0)),
                  pl.BlockSpec((K, tn), lambda j, i: (0, j))],
        out_specs=pl.BlockSpec((tm, tn), lambda j, i: (i, j)),
        out_shape=jax.ShapeDtypeStruct((M, N), F32),
        compiler_params=_cparams(2),
        name="in_proj",
    )(x, w)


def _attn_prompt_kernel(slope_ref, q_ref, k_ref, v_ref, o_ref, l_ref, kp_ref, vp_ref, *, dil):
    c = pl.program_id(1)

    @pl.when(c < pl.num_programs(1) - 1)
    def _():
        _attn_prompt_chunk(slope_ref, q_ref, k_ref, v_ref, o_ref, l_ref, kp_ref, vp_ref, dil=dil)

    @pl.when(c == pl.num_programs(1) - 1)
    def _():
        o_ref[...] = jnp.zeros_like(o_ref)
        l_ref[...] = jnp.zeros_like(l_ref)


def _attn_prompt_chunk(slope_ref, q_ref, k_ref, v_ref, o_ref, l_ref, kp_ref, vp_ref, *, dil):
    h = pl.program_id(0)
    c = pl.program_id(1)
    G = RUNS // dil
    wq = BAND // G
    scale = HEAD_DIM ** -0.5

    @pl.when(c == 0)
    def _():
        kp_ref[...] = jnp.zeros_like(kp_ref)
        vp_ref[...] = jnp.zeros_like(vp_ref)

    qi = lax.broadcasted_iota(jnp.int32, (BAND, 2 * BAND), 0)
    ci = lax.broadcasted_iota(jnp.int32, (BAND, 2 * BAND), 1)
    e = ci % (2 * wq)
    rel = (G * (qi % wq) + qi // wq) - (G * (e - wq) + ci // (2 * wq))
    valid = (rel >= 0) & (rel <= BAND)
    bias = rel.astype(F32) * (-slope_ref[h] * dil)
    bias_in = jnp.where(valid, bias, NEG)
    first_e = jnp.where(c > 0, 0, wq)
    bias_first = jnp.where(valid & (e >= first_e), bias, NEG)

    def keys(cur_ref, prev_ref, runs, m):
        if m == 0:
            parts = []
            for rho in runs:
                parts += [prev_ref[(rho + 1) * RUN_LEN - wq:(rho + 1) * RUN_LEN, :],
                          cur_ref[rho * RUN_LEN: rho * RUN_LEN + wq, :]]
        else:
            parts = [cur_ref[rho * RUN_LEN + wq * (m - 1): rho * RUN_LEN + wq * (m + 1), :] for rho in runs]
        return jnp.concatenate(parts, axis=0).astype(BF16)

    blocks = [([r + dil * v for v in range(G)], m) for r in range(dil) for m in range(G)]
    for g0 in range(0, len(blocks), ATT_GROUP):
        grp = blocks[g0:g0 + ATT_GROUP]
        q = [jnp.concatenate([q_ref[rho * RUN_LEN + wq * m: rho * RUN_LEN + wq * (m + 1), :] for rho in runs],
                             axis=0).astype(BF16) for runs, m in grp]
        s = [lax.dot_general(q[t], keys(k_ref, kp_ref, runs, m), _NT, preferred_element_type=F32) * scale
             + (bias_first if m == 0 else bias_in) for t, (runs, m) in enumerate(grp)]
        mx = [jnp.max(x, axis=-1, keepdims=True) for x in s]
        p = [jnp.exp(s[t] - mx[t]) for t in range(len(grp))]
        l = [jnp.sum(x, axis=-1, keepdims=True) for x in p]
        o = [jnp.dot(p[t].astype(BF16), keys(v_ref, vp_ref, runs, m), preferred_element_type=F32) / l[t]
             for t, (runs, m) in enumerate(grp)]
        for t, (runs, m) in enumerate(grp):
            lse = jnp.broadcast_to(mx[t] + jnp.log(l[t]), (BAND, HEAD_DIM))
            for idx, rho in enumerate(runs):
                rows = slice(rho * RUN_LEN + wq * m, rho * RUN_LEN + wq * (m + 1))
                o_ref[rows, :] = o[t][idx * wq:(idx + 1) * wq, :]
                l_ref[rows, :] = lse[idx * wq:(idx + 1) * wq, :]

    kp_ref[...] = k_ref[...]
    vp_ref[...] = v_ref[...]


def _attn_prompt(a_mat, g, seq):
    T = a_mat.shape[0]
    _, dil = ATT_GROUPS[g]
    assert seq % CHUNK == 0 and RUNS % dil == 0
    slopes = jnp.asarray(_alibi_slopes()[g], F32)

    def col(c):
        return pl.BlockSpec((CHUNK, HEAD_DIM), lambda h, n: (n, (3 * g + c) * HEADS + h))

    out_spec = pl.BlockSpec((CHUNK, HEAD_DIM), lambda h, n: (n, h))
    out_sds = jax.ShapeDtypeStruct((T, ATT_WIDTH), F32)
    return pl.pallas_call(
        functools.partial(_attn_prompt_kernel, dil=dil),
        grid=(HEADS, seq // CHUNK + 1),
        in_specs=[pl.BlockSpec(memory_space=pltpu.SMEM), col(0), col(1), col(2)],
        out_specs=[out_spec, out_spec],
        out_shape=[out_sds, out_sds],
        scratch_shapes=[pltpu.VMEM((CHUNK, HEAD_DIM), F32), pltpu.VMEM((CHUNK, HEAD_DIM), F32)],
        compiler_params=_cparams(2),
        name=f"attn_prompt_g{g}",
    )(slopes, a_mat, a_mat, a_mat)


def _attn_sample_kernel(q_ref, k_ref, v_ref, c_ref, o_in, l_in, o_ref, l_ref,
                        *, slopes, window, dil, dec_seq, nsub):
    del o_in, l_in
    R = c_ref.shape[1]
    wb = window
    stride = wb // R
    rows8 = 2 * dec_seq
    scale = HEAD_DIM ** -0.5
    ncache = R * nsub

    ri = lax.broadcasted_iota(jnp.int32, (rows8, ncache), 0)
    ji = lax.broadcasted_iota(jnp.int32, (rows8, ncache), 1)
    t_row = ri % dec_seq
    c_true = (ji % R) * stride + ji // R
    dist_c = wb + t_row - c_true
    ok_c = ((dist_c & (dil - 1)) == 0) & (dist_c <= window)
    bias_c_base = jnp.where(ok_c, 0.0, NEG)
    dist_c_f = dist_c.astype(F32)

    rn = lax.broadcasted_iota(jnp.int32, (rows8, rows8), 0)
    jn = lax.broadcasted_iota(jnp.int32, (rows8, rows8), 1)
    dist_n = rn % dec_seq - jn % dec_seq
    ok_n = (rn // dec_seq == jn // dec_seq) & (dist_n >= 0) & ((dist_n & (dil - 1)) == 0)
    bias_n_base = jnp.where(ok_n, 0.0, NEG)
    dist_n_f = dist_n.astype(F32)
    row_seq = lax.broadcasted_iota(jnp.int32, (rows8, HEAD_DIM), 0) // dec_seq

    for h in range(HEADS):
        cs = slice(h * HEAD_DIM, (h + 1) * HEAD_DIM)
        q = q_ref[:, cs].astype(BF16)
        kn = k_ref[:, cs].astype(BF16)
        vn = v_ref[:, cs].astype(BF16)
        s_n = lax.dot_general(q, kn, _NT, preferred_element_type=F32) * scale - dist_n_f * slopes[h] + bias_n_base
        o_h = jnp.zeros((rows8, HEAD_DIM), F32)
        l_h = jnp.zeros((rows8, HEAD_DIM), F32)
        for bb in range(2):
            kc = jnp.concatenate([c_ref[bb, :, u * 2 * HEADS + h, :] for u in range(nsub)], axis=0).astype(BF16)
            vc = jnp.concatenate([c_ref[bb, :, u * 2 * HEADS + HEADS + h, :] for u in range(nsub)],
                                 axis=0).astype(BF16)
            s_c = lax.dot_general(q, kc, _NT, preferred_element_type=F32) * scale - dist_c_f * slopes[h] + bias_c_base
            m = jnp.maximum(jnp.max(s_c, axis=-1, keepdims=True), jnp.max(s_n, axis=-1, keepdims=True))
            p_c = jnp.exp(s_c - m)
            p_n = jnp.exp(s_n - m)
            l = jnp.sum(p_c, axis=-1, keepdims=True) + jnp.sum(p_n, axis=-1, keepdims=True)
            o = (jnp.dot(p_c.astype(BF16), vc, preferred_element_type=F32)
                 + jnp.dot(p_n.astype(BF16), vn, preferred_element_type=F32)) / l
            mine = row_seq == bb
            o_h = jnp.where(mine, o, o_h)
            l_h = jnp.where(mine, jnp.broadcast_to(m + jnp.log(l), (rows8, HEAD_DIM)), l_h)
        o_ref[:, cs] = o_h
        l_ref[:, cs] = l_h


def _attn_sample(a_mat, cache, layer, o_prev, l_prev, g, seq, dec_seq):
    window, dil = ATT_GROUPS[g]
    depth, B, wb = cache.shape[0], cache.shape[1], cache.shape[2]
    assert wb == window and B % 2 == 0 and dec_seq == 4 and seq % (2 * dec_seq) == 0
    rows8 = 2 * dec_seq
    R = min(wb, BAND)
    stride = wb // R
    nsub = min(stride, dec_seq)
    cv = cache.reshape(depth, B, R, stride * 2 * HEADS, HEAD_DIM)
    base = seq // rows8
    c0 = 3 * g

    def new(c):
        return pl.BlockSpec((rows8, ATT_WIDTH), lambda i: (base + i, c0 + c))

    out_spec = pl.BlockSpec((rows8, ATT_WIDTH), lambda i: (base + i, 0))
    any_spec = pl.BlockSpec(memory_space=pl.ANY)
    return pl.pallas_call(
        functools.partial(_attn_sample_kernel, slopes=_alibi_slopes()[g], window=window, dil=dil,
                          dec_seq=dec_seq, nsub=nsub),
        grid=(B // 2,),
        in_specs=[new(0), new(1), new(2),
                  pl.BlockSpec((None, 2, R, nsub * 2 * HEADS, HEAD_DIM), lambda i: (layer, i, 0, 0, 0)),
                  any_spec, any_spec],
        out_specs=[out_spec, out_spec],
        out_shape=[jax.ShapeDtypeStruct(o_prev.shape, F32), jax.ShapeDtypeStruct(l_prev.shape, F32)],
        input_output_aliases={4: 0, 5: 1},
        compiler_params=_cparams(1),
        name=f"attn_sample_g{g}",
    )(a_mat, a_mat, a_mat, cv, o_prev, l_prev)


def _log_decay(ga, wa, ba):
    z = jnp.dot(ga, wa, preferred_element_type=F32, precision=lax.Precision.HIGHEST) + ba
    return (jnp.minimum(z, 0.0) - jnp.log(1.0 + jnp.exp(-jnp.abs(z)))) * (1.0 / GLA_TAU)


def _gla_prompt_kernel(q_ref, k_ref, v_ref, ga_ref, wa_ref, ba_ref, o_ref, sfin_ref, st_ref, la_ref,
                      qe_ref, upd_ref, ebl_ref):
    c = pl.program_id(1)

    @pl.when(c < pl.num_programs(1) - 1)
    def _():
        _gla_prompt_chunk(q_ref, k_ref, v_ref, ga_ref, wa_ref, ba_ref, o_ref, sfin_ref, st_ref, la_ref,
                      qe_ref, upd_ref, ebl_ref)

    @pl.when(c == pl.num_programs(1) - 1)
    def _():
        o_ref[...] = jnp.zeros_like(o_ref)


def _gla_prompt_chunk(q_ref, k_ref, v_ref, ga_ref, wa_ref, ba_ref, o_ref, sfin_ref, st_ref, la_ref,
                      qe_ref, upd_ref, ebl_ref):
    c = pl.program_id(1)
    C = GLA_CHUNK
    per_run = C // RUNS
    nsb = C // GLA_SUB
    sub_w = GLA_SUB // RUNS

    @pl.when(c == 0)
    def _():
        st_ref[...] = jnp.zeros_like(st_ref)

    la_ref[...] = _log_decay(ga_ref[...], wa_ref[...], ba_ref[...])

    r_i = lax.broadcasted_iota(jnp.int32, (C, C), 0)
    c_i = lax.broadcasted_iota(jnp.int32, (C, C), 1)
    tok_r = RUNS * (r_i % per_run) + r_i // per_run
    tok_c = RUNS * (c_i % per_run) + c_i // per_run
    tri = (tok_c <= tok_r).astype(F32)
    diag = (tok_r // GLA_SUB == tok_c // GLA_SUB) & (tok_c <= tok_r)
    rowblk = (lax.broadcasted_iota(jnp.int32, (C, GLA_HK), 0) % per_run) // sub_w
    last_run = (RUNS - 1) * per_run

    def gather(ref, j):
        return jnp.concatenate(
            [ref[r * RUN_LEN + j * per_run: r * RUN_LEN + (j + 1) * per_run, :] for r in range(RUNS)], axis=0)

    def scatter(ref, j, val, add):
        for r in range(RUNS):
            rows = slice(r * RUN_LEN + j * per_run, r * RUN_LEN + (j + 1) * per_run)
            piece = val[r * per_run:(r + 1) * per_run, :]
            ref[rows, :] = ref[rows, :] + piece if add else piece

    def local(js):
        def ref_rows(b, i):
            end_row = last_run + (i + 1) * sub_w - 1
            return b[end_row - sub_w:end_row - sub_w + 1, :], b[end_row:end_row + 1, :]

        b = [jnp.dot(tri, gather(la_ref, j), preferred_element_type=F32, precision=lax.Precision.HIGHEST) for j in js]
        q = [gather(q_ref, j) * (GLA_HK ** -0.5) for j in js]
        k = [gather(k_ref, j) for j in js]
        v = [gather(v_ref, j).astype(BF16) for j in js]
        qd, kd, ke, bstart = [], [], [], []
        for t, j in enumerate(js):
            qe_ref[j] = (q[t] * jnp.exp(b[t])).astype(BF16)
            bs = jnp.zeros_like(b[t])
            be = jnp.zeros_like(b[t])
            for i in range(nsb):
                prev_end, own_end = ref_rows(b[t], i)
                be = jnp.where(rowblk == i, own_end, be)
                if i > 0:
                    bs = jnp.where(rowblk == i, prev_end, bs)
            bstart.append(bs)
            qd.append(q[t] * jnp.exp(b[t] - bs))
            kd.append((k[t] * jnp.exp(bs - b[t])).astype(BF16))
            ke.append(k[t] * jnp.exp(be - b[t]))
        att = [jnp.where(diag, lax.dot_general(qd[t].astype(BF16), kd[t], _NT, preferred_element_type=F32), 0.0)
               for t in range(len(js))]
        for i in range(nsb - 1):
            qj, kej = [], []
            for t in range(len(js)):
                own_end = ref_rows(b[t], i)[1]
                dj = jnp.where(rowblk > i, jnp.exp(jnp.minimum(bstart[t] - own_end, 0.0)), 0.0)
                qj.append((qd[t] * dj).astype(BF16))
                kej.append(jnp.where(rowblk == i, ke[t], 0.0).astype(BF16))
            att = [att[t] + lax.dot_general(qj[t], kej[t], _NT, preferred_element_type=F32) for t in range(len(js))]
        intra = [jnp.dot(att[t].astype(BF16), v[t], preferred_element_type=F32) for t in range(len(js))]
        kdec = [(k[t] * jnp.exp(b[t][C - 1:C, :] - b[t])).astype(BF16) for t in range(len(js))]
        upd = [lax.dot_general(v[t], kdec[t], _TN, preferred_element_type=F32) for t in range(len(js))]
        for t, j in enumerate(js):
            scatter(o_ref, j, intra[t], add=False)
            upd_ref[j] = upd[t]
            ebl_ref[j] = jnp.broadcast_to(jnp.exp(b[t][C - 1:C, :]), (SUBLANES, GLA_HK))

    def recur(j):
        st = st_ref[...]
        scatter(o_ref, j, lax.dot_general(qe_ref[j], st.astype(BF16), _NT, preferred_element_type=F32), add=True)
        st_ref[...] = st * ebl_ref[j][0:1, :] + upd_ref[j]

    n_steps = CHUNK // C
    for j0 in range(0, n_steps, GLA_GROUP):
        local(list(range(j0, j0 + GLA_GROUP)))
    for j in range(n_steps):
        recur(j)

    @pl.when(c == pl.num_programs(1) - 2)
    def _():
        sfin_ref[...] = st_ref[...].T


def _gla_prompt(b1, b2, wa_pad, ba, seq):
    T = b1.shape[0]
    assert seq % CHUNK == 0
    nq = GLA_DK // GLA_HK
    og, sfin = pl.pallas_call(
        _gla_prompt_kernel,
        grid=(GLA_HEADS, seq // CHUNK + 1),
        in_specs=[pl.BlockSpec((CHUNK, GLA_HK), lambda h, n: (n, h)),
                  pl.BlockSpec((CHUNK, GLA_HK), lambda h, n: (n, nq + h)),
                  pl.BlockSpec((CHUNK, GLA_HV), lambda h, n: (n, h)),
                  pl.BlockSpec((CHUNK, LANE), lambda h, n: (n, 2 * nq)),
                  pl.BlockSpec((LANE, GLA_HK), lambda h, n: (0, h)),
                  pl.BlockSpec((1, GLA_HK), lambda h, n: (0, h))],
        out_specs=[pl.BlockSpec((CHUNK, GLA_HV), lambda h, n: (n, h)),
                   pl.BlockSpec((None, GLA_HK, GLA_HV), lambda h, n: (h, 0, 0))],
        out_shape=[jax.ShapeDtypeStruct((T, GLA_DV), F32),
                   jax.ShapeDtypeStruct((GLA_HEADS, GLA_HK, GLA_HV), F32)],
        scratch_shapes=[pltpu.VMEM((GLA_HV, GLA_HK), F32), pltpu.VMEM((CHUNK, GLA_HK), F32),
                        pltpu.VMEM((CHUNK // GLA_CHUNK, GLA_CHUNK, GLA_HK), BF16),
                        pltpu.VMEM((CHUNK // GLA_CHUNK, GLA_HV, GLA_HK), F32),
                        pltpu.VMEM((CHUNK // GLA_CHUNK, SUBLANES, GLA_HK), F32)],
        compiler_params=_cparams(2),
        name="gla_prompt",
    )(b2, b2, b1, b2, wa_pad, ba)
    return og, sfin


def _gla_sample_kernel(q_ref, k_ref, v_ref, ga_ref, wa_ref, ba_ref, s0_ref, og_in, o_ref, snew_ref, *, dec_seq):
    del og_in
    rows8 = 2 * dec_seq
    r_i = lax.broadcasted_iota(jnp.int32, (rows8, rows8), 0)
    c_i = lax.broadcasted_iota(jnp.int32, (rows8, rows8), 1)
    same = (r_i // dec_seq == c_i // dec_seq) & (c_i <= r_i)
    tri = same.astype(F32)
    la = _log_decay(ga_ref[...], wa_ref[...], ba_ref[...])
    b_all = jnp.dot(tri, la, preferred_element_type=F32, precision=lax.Precision.HIGHEST)
    row_seq_k = lax.broadcasted_iota(jnp.int32, (rows8, GLA_HK), 0) // dec_seq
    row_seq_v = lax.broadcasted_iota(jnp.int32, (rows8, GLA_HV), 0) // dec_seq
    for h in range(GLA_HEADS):
        ks = slice(h * GLA_HK, (h + 1) * GLA_HK)
        vs = slice(h * GLA_HV, (h + 1) * GLA_HV)
        b = b_all[:, ks]
        q = q_ref[:, ks] * (GLA_HK ** -0.5)
        k = k_ref[:, ks]
        v = v_ref[:, vs].astype(BF16)
        qd = (q * jnp.exp(b)).astype(BF16)
        kd = (k * jnp.exp(-b)).astype(BF16)
        att = jnp.where(same, lax.dot_general(qd, kd, _NT, preferred_element_type=F32), 0.0)
        o = jnp.dot(att.astype(BF16), v, preferred_element_type=F32)
        for bb in range(2):
            st = s0_ref[bb, h].T
            inter = lax.dot_general(qd, st.astype(BF16), _NT, preferred_element_type=F32)
            o = o + jnp.where(row_seq_v == bb, inter, 0.0)
            bl = b[(bb + 1) * dec_seq - 1:(bb + 1) * dec_seq, :]
            kdec = jnp.where(row_seq_k == bb, k * jnp.exp(bl - b), 0.0).astype(BF16)
            upd = lax.dot_general(v, kdec, _TN, preferred_element_type=F32)
            snew_ref[bb, h] = (st * jnp.exp(bl) + upd).T
        o_ref[:, vs] = o


def _gla_sample(b1, b2, wa_pad, ba, state, layer, og_prev, seq, dec_seq):
    B = state.shape[1]
    rows8 = 2 * dec_seq
    base = seq // rows8
    nq = GLA_DK // GLA_HK
    return pl.pallas_call(
        functools.partial(_gla_sample_kernel, dec_seq=dec_seq),
        grid=(B // 2,),
        in_specs=[pl.BlockSpec((rows8, GLA_DK), lambda i: (base + i, 0)),
                  pl.BlockSpec((rows8, GLA_DK), lambda i: (base + i, 1)),
                  pl.BlockSpec((rows8, GLA_DV), lambda i: (base + i, 0)),
                  pl.BlockSpec((rows8, LANE), lambda i: (base + i, 2 * nq)),
                  pl.BlockSpec((LANE, GLA_DK), lambda i: (0, 0)),
                  pl.BlockSpec((1, GLA_DK), lambda i: (0, 0)),
                  pl.BlockSpec((None, 2, GLA_HEADS, GLA_HK, GLA_HV), lambda i: (layer, i, 0, 0, 0)),
                  pl.BlockSpec(memory_space=pl.ANY)],
        out_specs=[pl.BlockSpec((rows8, GLA_DV), lambda i: (base + i, 0)),
                   pl.BlockSpec((2, GLA_HEADS, GLA_HK, GLA_HV), lambda i: (i, 0, 0, 0))],
        out_shape=[jax.ShapeDtypeStruct(og_prev.shape, F32), jax.ShapeDtypeStruct(state.shape[1:], F32)],
        input_output_aliases={7: 0},
        compiler_params=_cparams(1),
        name="gla_sample",
    )(b2, b2, b1, b2, wa_pad, ba, state, og_prev)


def _post_kernel(o1, o2, o3, l1, l2, l3, og_ref, gr_ref, ga_ref, gb_ref, x_ref,
                 wpa_ref, wpb_ref, wout_ref, gn_ref, lg_ref, lb_ref, xo_ref, xb_ref, *, alpha):
    m = jnp.maximum(jnp.maximum(l1[...], l2[...]), l3[...])
    e1 = jnp.exp(l1[...] - m)
    e2 = jnp.exp(l2[...] - m)
    e3 = jnp.exp(l3[...] - m)
    att = (e1 * o1[...] + e2 * o2[...] + e3 * o3[...]) / (e1 + e2 + e3)
    ya = jnp.dot(att.astype(BF16), wpa_ref[...], preferred_element_type=F32)
    parts = []
    for h in range(GLA_HEADS):
        vs = slice(h * GLA_HV, (h + 1) * GLA_HV)
        og = og_ref[:, vs]
        ms = jnp.mean(og * og, axis=-1, keepdims=True)
        parts.append((og * lax.rsqrt(ms + RMS_EPS) * gn_ref[:, vs] * _silu(gr_ref[:, vs])).astype(BF16))
    y = jnp.concatenate(parts, axis=-1)
    yb = jnp.dot(y, wpb_ref[...], preferred_element_type=F32)
    merged = _sigmoid(ga_ref[...]) * ya + _sigmoid(gb_ref[...]) * yb
    mix = jnp.dot(merged.astype(BF16), wout_ref[...], preferred_element_type=F32)
    xn = _layer_norm_rows(alpha * x_ref[...] + mix, lg_ref[...], lb_ref[...])
    xo_ref[...] = xn
    xb_ref[...] = xn.astype(BF16)


def _post(os_, ls_, og, b1, x, wpa, wpb, wout, gn, lg, lb, alpha, tm):
    T, D = x.shape
    row = lambda w: pl.BlockSpec((tm, w), lambda i: (i, 0))
    col = lambda c: pl.BlockSpec((tm, D), lambda i: (i, c))
    full = lambda a: pl.BlockSpec(a.shape, lambda i: (0,) * a.ndim)
    return pl.pallas_call(
        functools.partial(_post_kernel, alpha=alpha),
        grid=(pl.cdiv(T, tm),),
        in_specs=[row(ATT_WIDTH)] * 6 + [row(GLA_DV), col(1), col(2), col(3), row(D),
                                          full(wpa), full(wpb), full(wout), full(gn), full(lg), full(lb)],
        out_specs=[row(D), row(D)],
        out_shape=[jax.ShapeDtypeStruct((T, D), F32), jax.ShapeDtypeStruct((T, D), BF16)],
        compiler_params=_cparams(1),
        name="post_mixer",
    )(*os_, *ls_, og, b1, b1, b1, x, wpa, wpb, wout, gn, lg, lb)


def _ffn_kernel(xb_ref, x_ref, wg_ref, wu_ref, wd_ref, lg_ref, lb_ref, xo_ref, xbo_ref, acc_ref, *, alpha):
    j = pl.program_id(1)
    xb = xb_ref[...]
    g = jnp.dot(xb, wg_ref[...], preferred_element_type=F32)
    u = jnp.dot(xb, wu_ref[...], preferred_element_type=F32)
    part = jnp.dot((_silu(g) * u).astype(BF16), wd_ref[...], preferred_element_type=F32)

    @pl.when(j == 0)
    def _():
        acc_ref[...] = part

    @pl.when(j > 0)
    def _():
        acc_ref[...] += part

    @pl.when(j == pl.num_programs(1) - 1)
    def _():
        xn = _layer_norm_rows(alpha * x_ref[...] + acc_ref[...], lg_ref[...], lb_ref[...])
        xo_ref[...] = xn
        xbo_ref[...] = xn.astype(BF16)


def _ffn(xb, x, wgu, wd, lg, lb, alpha, tm, tc):
    T, D = x.shape
    dff = wd.shape[0]
    assert dff % tc == 0
    nj = dff // tc
    full = lambda a: pl.BlockSpec(a.shape, lambda i, j: (0,) * a.ndim)
    return pl.pallas_call(
        functools.partial(_ffn_kernel, alpha=alpha),
        grid=(pl.cdiv(T, tm), nj),
        in_specs=[pl.BlockSpec((tm, D), lambda i, j: (i, 0)),
                  pl.BlockSpec((tm, D), lambda i, j: (i, 0)),
                  pl.BlockSpec((D, tc), lambda i, j: (0, j)),
                  pl.BlockSpec((D, tc), lambda i, j: (0, nj + j)),
                  pl.BlockSpec((tc, D), lambda i, j: (j, 0)),
                  full(lg), full(lb)],
        out_specs=[pl.BlockSpec((tm, D), lambda i, j: (i, 0))] * 2,
        out_shape=[jax.ShapeDtypeStruct((T, D), F32), jax.ShapeDtypeStruct((T, D), BF16)],
        scratch_shapes=[pltpu.VMEM((tm, D), F32)],
        compiler_params=_cparams(2),
        name="ffn_dense",
    )(xb, x, wgu, wgu, wd, lg, lb)


MOE_TOKENS = 512
MOE_PIECE = 16
MOE_ROWS = -(-(2 * MOE_TOKENS + N_EXPERTS * (MOE_PIECE - 1)) // MOE_PIECE) * MOE_PIECE
MOE_BLOCK = 1024
MOE_HALF = 512
MOE_SLOTS = MOE_BLOCK // MOE_PIECE


def _moe_route_kernel(x_ref, xb_ref, wrt_ref, xs_ref, meta_ref, cnt_ref, *, n_tokens):
    i = pl.program_id(0)
    ts = x_ref.shape[0]
    row_ok = (lax.broadcasted_iota(jnp.int32, (ts, 1), 0) + i * ts) < n_tokens
    x = jnp.where(row_ok, x_ref[...], 0.0)
    xb = jnp.where(row_ok, xb_ref[...], jnp.zeros((), BF16))
    lt = lax.dot_general(wrt_ref[...], x, _NT, preferred_element_type=F32, precision=lax.Precision.HIGHEST)
    sub = lax.broadcasted_iota(jnp.int32, (N_EXPERTS, ts), 0).astype(F32)
    tok_ok = (lax.broadcasted_iota(jnp.int32, (1, ts), 1) + i * ts) < n_tokens
    m1 = jnp.max(lt, axis=0, keepdims=True)
    i1 = jnp.min(jnp.where(lt == m1, sub, float(N_EXPERTS)), axis=0, keepdims=True)
    lt2 = jnp.where(sub == i1, NEG, lt)
    m2 = jnp.max(lt2, axis=0, keepdims=True)
    i2 = jnp.min(jnp.where(lt2 == m2, sub, float(N_EXPERTS)), axis=0, keepdims=True)
    e2 = jnp.exp(m2 - m1)
    g1 = 1.0 / (1.0 + e2)
    g2 = e2 / (1.0 + e2)
    oh1 = (sub == i1) & tok_ok
    oh2 = (sub == i2) & tok_ok
    assign = jnp.where(oh1 | oh2, 1.0, 0.0)
    before = (lax.broadcasted_iota(jnp.int32, (ts, ts), 0) < lax.broadcasted_iota(jnp.int32, (ts, ts), 1))
    rank = jnp.dot(assign.astype(BF16), jnp.where(before, 1.0, 0.0).astype(BF16), preferred_element_type=F32)
    cnt = jnp.sum(assign, axis=1, keepdims=True)
    padded = jnp.floor((cnt + (MOE_PIECE - 1)) * (1.0 / MOE_PIECE)) * MOE_PIECE
    lower = (lax.broadcasted_iota(jnp.int32, (N_EXPERTS, N_EXPERTS), 1)
             < lax.broadcasted_iota(jnp.int32, (N_EXPERTS, N_EXPERTS), 0)).astype(F32)
    seg_off = jnp.dot(lower, jnp.broadcast_to(padded, (N_EXPERTS, ts)), preferred_element_type=F32,
                      precision=lax.Precision.HIGHEST)
    dest = seg_off + rank
    d1 = jnp.where(tok_ok, jnp.sum(jnp.where(oh1, dest, 0.0), axis=0, keepdims=True), -1.0)
    d2 = jnp.where(tok_ok, jnp.sum(jnp.where(oh2, dest, 0.0), axis=0, keepdims=True), -1.0)
    p_i = lax.broadcasted_iota(jnp.int32, (MOE_ROWS, ts), 0).astype(F32)
    perm = jnp.where((p_i == d1) | (p_i == d2), 1.0, 0.0).astype(BF16)
    xs_ref[...] = jnp.dot(perm, xb, preferred_element_type=F32).astype(BF16)
    sub8 = lax.broadcasted_iota(jnp.int32, (SUBLANES, ts), 0)
    meta_ref[...] = jnp.where(sub8 == 0, d1, jnp.where(sub8 == 1, d2, jnp.where(sub8 == 2, g1,
                              jnp.where(sub8 == 3, g2, 0.0))))
    cnt_ref[...] = jnp.broadcast_to(cnt, (N_EXPERTS, LANE))


def _moe_route(x, xb, wr_t):
    T, D = x.shape
    nt = pl.cdiv(T, MOE_TOKENS)
    return pl.pallas_call(
        functools.partial(_moe_route_kernel, n_tokens=T),
        grid=(nt,),
        in_specs=[pl.BlockSpec((MOE_TOKENS, D), lambda i: (i, 0)),
                  pl.BlockSpec((MOE_TOKENS, D), lambda i: (i, 0)),
                  pl.BlockSpec(wr_t.shape, lambda i: (0, 0))],
        out_specs=[pl.BlockSpec((MOE_ROWS, D), lambda i: (i, 0)),
                   pl.BlockSpec((None, SUBLANES, MOE_TOKENS), lambda i: (i, 0, 0)),
                   pl.BlockSpec((None, N_EXPERTS, LANE), lambda i: (i, 0, 0))],
        out_shape=[jax.ShapeDtypeStruct((nt * MOE_ROWS, D), BF16),
                   jax.ShapeDtypeStruct((nt, SUBLANES, MOE_TOKENS), F32),
                   jax.ShapeDtypeStruct((nt, N_EXPERTS, LANE), F32)],
        compiler_params=_cparams(1),
        name="moe_route",
    )(x, xb, wr_t)


def _moe_plan(cnt, n_blocks):
    nt = cnt.shape[0]
    i32 = jnp.int32
    padded = (cnt + MOE_PIECE - 1) // MOE_PIECE * MOE_PIECE
    seg_row = jnp.cumsum(padded, axis=1) - padded + (jnp.arange(nt, dtype=i32) * MOE_ROWS)[:, None]
    n_seg = (padded // MOE_PIECE).T.reshape(-1)
    seg_row = seg_row.T.reshape(-1)
    seg_end = jnp.cumsum(n_seg)
    seg_start = seg_end - n_seg
    per_e = n_seg.reshape(N_EXPERTS, nt).sum(axis=1)
    e_end = jnp.cumsum(per_e)
    e_start = e_end - per_e
    blocks_e = (per_e + MOE_SLOTS - 1) // MOE_SLOTS
    blk_end = jnp.cumsum(blocks_e)
    blk_start = blk_end - blocks_e
    w = jnp.arange(n_blocks, dtype=i32)
    ew = jnp.minimum(jnp.sum((w[:, None] >= blk_end[None, :]).astype(i32), axis=1), N_EXPERTS - 1)
    is_e = ew[:, None] == jnp.arange(N_EXPERTS, dtype=i32)[None, :]
    pick = lambda v: jnp.sum(jnp.where(is_e, v[None, :], 0), axis=1)
    first = pick(e_start) + (w - pick(blk_start)) * MOE_SLOTS
    n_w = jnp.where(w < blk_end[-1], jnp.clip(pick(e_end) - first, 0, MOE_SLOTS), 0).astype(i32)
    slot = jnp.arange(MOE_SLOTS, dtype=i32)
    p = first[:, None] + slot[None, :]
    in_seg = (p[:, :, None] >= seg_start[None, None, :]) & (p[:, :, None] < seg_end[None, None, :])
    rows = jnp.sum(jnp.where(in_seg, (seg_row - MOE_PIECE * seg_start)[None, None, :], 0), axis=2) + MOE_PIECE * p
    rows = jnp.where(slot[None, :] < n_w[:, None], rows, 0).astype(i32)
    return ew.astype(i32), n_w, rows.reshape(-1)


def _moe_expert_kernel(ew_ref, n_ref, rows_ref, xs_hbm, wg_ref, wu_ref, wd_ref, ys_in, ys_hbm,
                       lhs_ref, acc_ref, out_ref, sem_in, sem_out):
    del ew_ref, ys_in
    w = pl.program_id(0)
    j = pl.program_id(1)
    n_w = pl.num_programs(0)
    last_j = pl.num_programs(1) - 1
    n = n_ref[w]
    slot = w % 2
    halves = [(s, slice(s * MOE_HALF, (s + 1) * MOE_HALF)) for s in range(MOE_BLOCK // MOE_HALF)]

    def piece_in(blk, q, sl):
        src = pl.multiple_of(rows_ref[blk * MOE_SLOTS + q], MOE_PIECE)
        dst = pl.multiple_of(q * MOE_PIECE, MOE_PIECE)
        return pltpu.make_async_copy(xs_hbm.at[pl.ds(src, MOE_PIECE), :],
                                     lhs_ref.at[sl, pl.ds(dst, MOE_PIECE), :], sem_in.at[sl])

    def piece_out(blk, q):
        dst = pl.multiple_of(rows_ref[blk * MOE_SLOTS + q], MOE_PIECE)
        src = pl.multiple_of(q * MOE_PIECE, MOE_PIECE)
        return pltpu.make_async_copy(out_ref.at[pl.ds(src, MOE_PIECE), :], ys_hbm.at[pl.ds(dst, MOE_PIECE), :], sem_out)

    def for_pieces(blk, fn):
        lax.fori_loop(0, n_ref[blk], lambda q, c: (fn(q), c)[1], 0)

    @pl.when(j == 0)
    def _():
        @pl.when(w == 0)
        def _():
            lhs_ref[...] = jnp.zeros_like(lhs_ref)
            for_pieces(0, lambda q: piece_in(0, q, 0).start())

        for_pieces(w, lambda q: piece_in(w, q, slot).wait())

        @pl.when(w + 1 < n_w)
        def _():
            for_pieces(w + 1, lambda q: piece_in(w + 1, q, 1 - slot).start())

    for s, rows in halves:
        @pl.when(n > s * (MOE_HALF // MOE_PIECE))
        def _(rows=rows):
            xb = lhs_ref[slot, rows, :]
            g = jnp.dot(xb, wg_ref[...], preferred_element_type=F32)
            u = jnp.dot(xb, wu_ref[...], preferred_element_type=F32)
            part = jnp.dot((_silu(g) * u).astype(BF16), wd_ref[...], preferred_element_type=F32)

            @pl.when(j == 0)
            def _():
                acc_ref[rows, :] = part

            @pl.when(j > 0)
            def _():
                acc_ref[rows, :] += part

    @pl.when(j == last_j)
    def _():
        @pl.when(w > 0)
        def _():
            for_pieces(w - 1, lambda q: piece_out(w - 1, q).wait())

        for s, rows in halves:
            @pl.when(n > s * (MOE_HALF // MOE_PIECE))
            def _(rows=rows):
                out_ref[rows, :] = acc_ref[rows, :].astype(BF16)

        for_pieces(w, lambda q: piece_out(w, q).start())

        @pl.when(w == n_w - 1)
        def _():
            for_pieces(w, lambda q: piece_out(w, q).wait())


def _moe_experts(xs, plan, wgu, wd, tc):
    ew, n_w, rows = plan
    n_blocks = ew.shape[0]
    D = xs.shape[1]
    dff = wd.shape[1]
    assert dff % tc == 0
    nj = dff // tc
    any_spec = pl.BlockSpec(memory_space=pl.ANY)
    return pl.pallas_call(
        _moe_expert_kernel,
        grid_spec=pltpu.PrefetchScalarGridSpec(
            num_scalar_prefetch=3,
            grid=(n_blocks, nj),
            in_specs=[any_spec,
                      pl.BlockSpec((None, D, tc), lambda w, j, ew, n, r: (ew[w], 0, j)),
                      pl.BlockSpec((None, D, tc), lambda w, j, ew, n, r: (ew[w], 0, nj + j)),
                      pl.BlockSpec((None, tc, D), lambda w, j, ew, n, r: (ew[w], j, 0)),
                      any_spec],
            out_specs=any_spec,
            scratch_shapes=[pltpu.VMEM((2, MOE_BLOCK, D), BF16), pltpu.VMEM((MOE_BLOCK, D), F32),
                            pltpu.VMEM((MOE_BLOCK, D), BF16),
                            pltpu.SemaphoreType.DMA((2,)), pltpu.SemaphoreType.DMA(())]),
        out_shape=jax.ShapeDtypeStruct(xs.shape, BF16),
        input_output_aliases={7: 0},
        compiler_params=_cparams(2),
        name="moe_experts",
    )(ew, n_w, rows, xs, wgu, wgu, wd, jnp.zeros(xs.shape, BF16))


def _moe_combine_kernel(ys_ref, meta_ref, x_ref, lg_ref, lb_ref, xo_ref, xbo_ref, *, alpha):
    ts = x_ref.shape[0]
    meta = meta_ref[...]
    d1, d2, g1, g2 = meta[0:1, :], meta[1:2, :], meta[2:3, :], meta[3:4, :]
    p_i = lax.broadcasted_iota(jnp.int32, (MOE_ROWS, ts), 0).astype(F32)
    back = (jnp.where(p_i == d1, g1, 0.0) + jnp.where(p_i == d2, g2, 0.0)).astype(BF16)
    f = lax.dot_general(back, ys_ref[...], _TN, preferred_element_type=F32)
    xn = _layer_norm_rows(alpha * x_ref[...] + f, lg_ref[...], lb_ref[...])
    xo_ref[...] = xn
    xbo_ref[...] = xn.astype(BF16)


def _moe_combine(ys, meta, x, lg, lb, alpha):
    T, D = x.shape
    nt = meta.shape[0]
    full = lambda a: pl.BlockSpec(a.shape, lambda i: (0,) * a.ndim)
    return pl.pallas_call(
        functools.partial(_moe_combine_kernel, alpha=alpha),
        grid=(nt,),
        in_specs=[pl.BlockSpec((MOE_ROWS, D), lambda i: (i, 0)),
                  pl.BlockSpec((None, SUBLANES, MOE_TOKENS), lambda i: (i, 0, 0)),
                  pl.BlockSpec((MOE_TOKENS, D), lambda i: (i, 0)),
                  full(lg), full(lb)],
        out_specs=[pl.BlockSpec((MOE_TOKENS, D), lambda i: (i, 0))] * 2,
        out_shape=[jax.ShapeDtypeStruct((T, D), F32), jax.ShapeDtypeStruct((T, D), BF16)],
        compiler_params=_cparams(1),
        name="moe_combine",
    )(ys, meta, x, lg, lb)


def _moe(x, xb, w_router, wgu, wd, lg, lb, alpha):
    T = x.shape[0]
    nt = pl.cdiv(T, MOE_TOKENS)
    xs, meta, cnt = _moe_route(x, xb, w_router.T)
    n_pieces = (2 * T + nt * N_EXPERTS * (MOE_PIECE - 1)) // MOE_PIECE + 1
    n_blocks = pl.cdiv(n_pieces, MOE_SLOTS) + N_EXPERTS
    plan = _moe_plan(cnt[:, :, 0].astype(jnp.int32), n_blocks)
    ys = _moe_experts(xs, plan, wgu, wd, 896)
    return _moe_combine(ys, meta, x, lg, lb, alpha)


def _split_w_in(w):
    a_end = N_GROUPS * 3 * ATT_WIDTH
    gq0, gk0, gv0 = a_end, a_end + GLA_DK, a_end + 2 * GLA_DK
    gr0 = gv0 + GLA_DV
    ga0 = gr0 + GLA_DV
    gta0 = ga0 + GLA_RANK
    wa = w[:, :a_end].astype(BF16)
    wb1 = jnp.concatenate([w[:, gv0:gr0], w[:, gr0:ga0], w[:, gta0:]], axis=1).astype(BF16)
    ga_pad = jnp.pad(w[:, ga0:gta0], ((0, 0), (0, LANE - GLA_RANK)))
    wb2 = jnp.concatenate([w[:, gq0:gk0], w[:, gk0:gv0], ga_pad], axis=1).astype(BF16)
    return wa, wb1, wb2


def _new_kv_rows(a_mat, g, seq, db, dec_seq):
    keep = min(ATT_GROUPS[g][0], seq)
    assert keep <= CHUNK and keep % RUNS == 0
    kv_cols = slice((3 * g + 1) * ATT_WIDTH, (3 * g + 3) * ATT_WIDTH)
    last = a_mat[seq - CHUNK:seq, kv_cols].reshape(RUNS, RUN_LEN, 2, HEADS, HEAD_DIM)
    kv_p = last[:, RUN_LEN - keep // RUNS:].transpose(1, 0, 2, 3, 4).reshape(1, keep, 2, HEADS, HEAD_DIM)
    kv_s = a_mat[seq:, kv_cols].reshape(db, dec_seq, 2, HEADS, HEAD_DIM)
    return kv_p, kv_s


def kernel(x_prompt, x_sample, cache_kv_w128, cache_kv_w512, cache_kv_w2048, state_gla,
           w_in, w_alpha2, b_alpha, gla_norm_g, w_pa, w_pb, w_out,
           ln_mix_g, ln_mix_b, ln_ffn_g, ln_ffn_b,
           w_ffn_gu, w_ffn_down, w_router, w_exp_gu, w_exp_down):
    depth = w_in.shape[0]
    bp, seq, D = x_prompt.shape
    db, dec_seq, _ = x_sample.shape
    assert bp == 1
    ns = db * dec_seq
    alpha = (2 * depth) ** 0.25
    caches = (cache_kv_w128, cache_kv_w512, cache_kv_w2048)

    x = jnp.concatenate([_interleave_rows(x_prompt.reshape(seq, D)), x_sample.reshape(ns, D)], axis=0)
    xb = x.astype(BF16)
    kv_p = [[] for _ in range(N_GROUPS)]
    kv_s = [[] for _ in range(N_GROUPS)]
    st_p, st_s = [], []
    row2 = lambda v: v.reshape(1, -1)

    for l in range(depth):
        wa, wb1, wb2 = _split_w_in(w_in[l])
        a_mat = _mm(xb, wa, 1024, 1152)
        b1 = _mm(xb, wb1, 1024, 1024)
        b2 = _mm(xb, wb2, 1024, wb2.shape[1])

        os_, ls_ = [], []
        for g in range(N_GROUPS):
            o, lse = _attn_prompt(a_mat, g, seq)
            o, lse = _attn_sample(a_mat, caches[g], l, o, lse, g, seq, dec_seq)
            os_.append(o)
            ls_.append(lse)
            p_rows, s_rows = _new_kv_rows(a_mat, g, seq, db, dec_seq)
            kv_p[g].append(p_rows)
            kv_s[g].append(s_rows)

        wa_pad = jnp.pad(w_alpha2[l], ((0, LANE - GLA_RANK), (0, 0)))
        og, s_p = _gla_prompt(b1, b2, wa_pad, row2(b_alpha[l]), seq)
        og, s_s = _gla_sample(b1, b2, wa_pad, row2(b_alpha[l]), state_gla, l, og, seq, dec_seq)
        st_p.append(s_p[None])
        st_s.append(s_s)

        x, xb = _post(os_, ls_, og, b1, x, w_pa[l].astype(BF16), w_pb[l].astype(BF16), w_out[l].astype(BF16),
                      row2(gla_norm_g[l]), row2(ln_mix_g[l]), row2(ln_mix_b[l]), alpha, 256)

        if l % 2 == 0:
            x, xb = _ffn(xb, x, w_ffn_gu[l // 2].astype(BF16), w_ffn_down[l // 2].astype(BF16),
                         row2(ln_ffn_g[l]), row2(ln_ffn_b[l]), alpha, 512, 1408)
        else:
            x, xb = _moe(x, xb, w_router[l // 2], w_exp_gu[l // 2].astype(BF16), w_exp_down[l // 2].astype(BF16),
                         row2(ln_ffn_g[l]), row2(ln_ffn_b[l]), alpha)

    return (_deinterleave_rows(x[:seq]).reshape(1, seq, D), x[seq:].reshape(db, dec_seq, D),
            jnp.stack(kv_p[0], 0), jnp.stack(kv_s[0], 0),
            jnp.stack(kv_p[1], 0), jnp.stack(kv_s[1], 0),
            jnp.stack(kv_p[2], 0), jnp.stack(kv_s[2], 0),
            jnp.stack(st_p, 0), jnp.stack(st_s, 0))
```

```python
import functools

import jax
import jax.numpy as jnp
from jax import lax
from jax.experimental import pallas as pl
from jax.experimental.pallas import tpu as pltpu

F32 = jnp.float32
BF16 = jnp.bfloat16

ATT_GROUPS = ((128, 1), (512, 4), (2048, 16))
N_GROUPS = 3
HEADS = 4
HEAD_DIM = 128
ATT_WIDTH = HEADS * HEAD_DIM
BAND = 128
ATT_GROUP = 4
GLA_HEADS = 4
GLA_HK = 128
GLA_HV = 256
GLA_DK = GLA_HEADS * GLA_HK
GLA_DV = GLA_HEADS * GLA_HV
GLA_RANK = 16
GLA_TAU = 16.0
GLA_CHUNK = 128
GLA_SUB = 32
GLA_GROUP = 8
N_EXPERTS = 8
LN_EPS = 1e-5
RMS_EPS = 1e-6
NEG = -1e30
LANE = 128
SUBLANES = 8
RUNS = 16
RUN_LEN = 128
CHUNK = RUNS * RUN_LEN
VMEM_LIMIT = 56 * 1024 * 1024

PROJ_ROWS = 1024
PROJ_COLS_ATT = 1536
GLA_PANEL = 2304
POST_ROWS = 256
FFN_ROWS = 512
FFN_CHUNK = 2816
MOE_CHUNK = 1792

_NT = (((1,), (1,)), ((), ()))
_TN = (((0,), (0,)), ((), ()))


def _alibi_slopes():
    n = N_GROUPS * HEADS
    return [[2.0 ** (-8.0 * (g * HEADS + h + 1) / n) for h in range(HEADS)] for g in range(N_GROUPS)]


def _cparams(n_axes):
    return pltpu.CompilerParams(dimension_semantics=("arbitrary",) * n_axes, vmem_limit_bytes=VMEM_LIMIT)


def _layer_norm_rows(r, g, b):
    mu = jnp.mean(r, axis=-1, keepdims=True)
    c = r - mu
    var = jnp.mean(c * c, axis=-1, keepdims=True)
    return c * lax.rsqrt(var + LN_EPS) * g + b


def _sigmoid(x):
    return 1.0 / (1.0 + jnp.exp(-x))


def _silu(x):
    return x * _sigmoid(x)


def _interleave_rows(x):
    s, d = x.shape
    return x.reshape(s // CHUNK, RUN_LEN, RUNS, d).transpose(0, 2, 1, 3).reshape(s, d)


def _deinterleave_rows(x):
    s, d = x.shape
    return x.reshape(s // CHUNK, RUNS, RUN_LEN, d).transpose(0, 2, 1, 3).reshape(s, d)


def _mm_kernel(x_ref, w_ref, o_ref):
    o_ref[...] = jnp.dot(x_ref[...], w_ref[...], preferred_element_type=F32)


def _mm(x, w, tm, tn):
    M, K = x.shape
    N = w.shape[1]
    assert N % tn == 0
    return pl.pallas_call(
        _mm_kernel,
        grid=(N // tn, pl.cdiv(M, tm)),
        in_specs=[pl.BlockSpec((tm, K), lambda j, i: (i, 0)),
                  pl.BlockSpec((K, tn), lambda j, i: (0, j))],
        out_specs=pl.BlockSpec((tm, tn), lambda j, i: (i, j)),
        out_shape=jax.ShapeDtypeStruct((M, N), F32),
        compiler_params=_cparams(2),
        name="in_proj",
    )(x, w)


def _attn_prompt_kernel(slope_ref, q_ref, k_ref, v_ref, o_ref, l_ref, kp_ref, vp_ref, *, dil):
    c = pl.program_id(1)

    @pl.when(c < pl.num_programs(1) - 1)
    def _():
        _attn_prompt_chunk(slope_ref, q_ref, k_ref, v_ref, o_ref, l_ref, kp_ref, vp_ref, dil=dil)

    @pl.when(c == pl.num_programs(1) - 1)
    def _():
        o_ref[...] = jnp.zeros_like(o_ref)
        l_ref[...] = jnp.zeros_like(l_ref)


def _attn_prompt_chunk(slope_ref, q_ref, k_ref, v_ref, o_ref, l_ref, kp_ref, vp_ref, *, dil):
    h = pl.program_id(0)
    c = pl.program_id(1)
    G = RUNS // dil
    wq = BAND // G
    scale = HEAD_DIM ** -0.5

    @pl.when(c == 0)
    def _():
        kp_ref[...] = jnp.zeros_like(kp_ref)
        vp_ref[...] = jnp.zeros_like(vp_ref)

    qi = lax.broadcasted_iota(jnp.int32, (BAND, 2 * BAND), 0)
    ci = lax.broadcasted_iota(jnp.int32, (BAND, 2 * BAND), 1)
    e = ci % (2 * wq)
    rel = (G * (qi % wq) + qi // wq) - (G * (e - wq) + ci // (2 * wq))
    valid = (rel >= 0) & (rel <= BAND)
    bias = rel.astype(F32) * (-slope_ref[h] * dil)
    bias_in = jnp.where(valid, bias, NEG)
    first_e = jnp.where(c > 0, 0, wq)
    bias_first = jnp.where(valid & (e >= first_e), bias, NEG)

    def keys(cur_ref, prev_ref, runs, m):
        if m == 0:
            parts = []
            for rho in runs:
                parts += [prev_ref[(rho + 1) * RUN_LEN - wq:(rho + 1) * RUN_LEN, :],
                          cur_ref[rho * RUN_LEN: rho * RUN_LEN + wq, :]]
        else:
            parts = [cur_ref[rho * RUN_LEN + wq * (m - 1): rho * RUN_LEN + wq * (m + 1), :] for rho in runs]
        return jnp.concatenate(parts, axis=0).astype(BF16)

    blocks = [([r + dil * v for v in range(G)], m) for r in range(dil) for m in range(G)]
    for g0 in range(0, len(blocks), ATT_GROUP):
        grp = blocks[g0:g0 + ATT_GROUP]
        q = [jnp.concatenate([q_ref[rho * RUN_LEN + wq * m: rho * RUN_LEN + wq * (m + 1), :] for rho in runs],
                             axis=0).astype(BF16) for runs, m in grp]
        s = [lax.dot_general(q[t], keys(k_ref, kp_ref, runs, m), _NT, preferred_element_type=F32) * scale
             + (bias_first if m == 0 else bias_in) for t, (runs, m) in enumerate(grp)]
        mx = [jnp.max(x, axis=-1, keepdims=True) for x in s]
        p = [jnp.exp(s[t] - mx[t]) for t in range(len(grp))]
        l = [jnp.sum(x, axis=-1, keepdims=True) for x in p]
        o = [jnp.dot(p[t].astype(BF16), keys(v_ref, vp_ref, runs, m), preferred_element_type=F32) / l[t]
             for t, (runs, m) in enumerate(grp)]
        for t, (runs, m) in enumerate(grp):
            lse = jnp.broadcast_to(mx[t] + jnp.log(l[t]), (BAND, HEAD_DIM))
            for idx, rho in enumerate(runs):
                rows = slice(rho * RUN_LEN + wq * m, rho * RUN_LEN + wq * (m + 1))
                o_ref[rows, :] = o[t][idx * wq:(idx + 1) * wq, :]
                l_ref[rows, :] = lse[idx * wq:(idx + 1) * wq, :]

    kp_ref[...] = k_ref[...]
    vp_ref[...] = v_ref[...]


def _attn_prompt(a_mat, g, seq):
    T = a_mat.shape[0]
    _, dil = ATT_GROUPS[g]
    assert seq % CHUNK == 0 and RUNS % dil == 0
    slopes = jnp.asarray(_alibi_slopes()[g], F32)

    def col(c):
        return pl.BlockSpec((CHUNK, HEAD_DIM), lambda h, n: (n, (3 * g + c) * HEADS + h))

    out_spec = pl.BlockSpec((CHUNK, HEAD_DIM), lambda h, n: (n, h))
    out_sds = jax.ShapeDtypeStruct((T, ATT_WIDTH), F32)
    return pl.pallas_call(
        functools.partial(_attn_prompt_kernel, dil=dil),
        grid=(HEADS, seq // CHUNK + 1),
        in_specs=[pl.BlockSpec(memory_space=pltpu.SMEM), col(0), col(1), col(2)],
        out_specs=[out_spec, out_spec],
        out_shape=[out_sds, out_sds],
        scratch_shapes=[pltpu.VMEM((CHUNK, HEAD_DIM), F32), pltpu.VMEM((CHUNK, HEAD_DIM), F32)],
        compiler_params=_cparams(2),
        name=f"attn_prompt_g{g}",
    )(slopes, a_mat, a_mat, a_mat)


def _attn_sample_kernel(q_ref, k_ref, v_ref, c_ref, o_in, l_in, o_ref, l_ref,
                        *, slopes, window, dil, dec_seq, nsub):
    del o_in, l_in
    R = c_ref.shape[1]
    wb = window
    stride = wb // R
    rows8 = 2 * dec_seq
    scale = HEAD_DIM ** -0.5
    ncache = R * nsub

    ri = lax.broadcasted_iota(jnp.int32, (rows8, ncache), 0)
    ji = lax.broadcasted_iota(jnp.int32, (rows8, ncache), 1)
    t_row = ri % dec_seq
    c_true = (ji % R) * stride + ji // R
    dist_c = wb + t_row - c_true
    ok_c = ((dist_c & (dil - 1)) == 0) & (dist_c <= window)
    bias_c_base = jnp.where(ok_c, 0.0, NEG)
    dist_c_f = dist_c.astype(F32)

    rn = lax.broadcasted_iota(jnp.int32, (rows8, rows8), 0)
    jn = lax.broadcasted_iota(jnp.int32, (rows8, rows8), 1)
    dist_n = rn % dec_seq - jn % dec_seq
    ok_n = (rn // dec_seq == jn // dec_seq) & (dist_n >= 0) & ((dist_n & (dil - 1)) == 0)
    bias_n_base = jnp.where(ok_n, 0.0, NEG)
    dist_n_f = dist_n.astype(F32)
    row_seq = lax.broadcasted_iota(jnp.int32, (rows8, HEAD_DIM), 0) // dec_seq

    for h in range(HEADS):
        cs = slice(h * HEAD_DIM, (h + 1) * HEAD_DIM)
        q = q_ref[:, cs].astype(BF16)
        kn = k_ref[:, cs].astype(BF16)
        vn = v_ref[:, cs].astype(BF16)
        s_n = lax.dot_general(q, kn, _NT, preferred_element_type=F32) * scale - dist_n_f * slopes[h] + bias_n_base
        o_h = jnp.zeros((rows8, HEAD_DIM), F32)
        l_h = jnp.zeros((rows8, HEAD_DIM), F32)
        for bb in range(2):
            kc = jnp.concatenate([c_ref[bb, :, u * 2 * HEADS + h, :] for u in range(nsub)], axis=0).astype(BF16)
            vc = jnp.concatenate([c_ref[bb, :, u * 2 * HEADS + HEADS + h, :] for u in range(nsub)],
                                 axis=0).astype(BF16)
            s_c = lax.dot_general(q, kc, _NT, preferred_element_type=F32) * scale - dist_c_f * slopes[h] + bias_c_base
            m = jnp.maximum(jnp.max(s_c, axis=-1, keepdims=True), jnp.max(s_n, axis=-1, keepdims=True))
            p_c = jnp.exp(s_c - m)
            p_n = jnp.exp(s_n - m)
            l = jnp.sum(p_c, axis=-1, keepdims=True) + jnp.sum(p_n, axis=-1, keepdims=True)
            o = (jnp.dot(p_c.astype(BF16), vc, preferred_element_type=F32)
                 + jnp.dot(p_n.astype(BF16), vn, preferred_element_type=F32)) / l
            mine = row_seq == bb
            o_h = jnp.where(mine, o, o_h)
            l_h = jnp.where(mine, jnp.broadcast_to(m + jnp.log(l), (rows8, HEAD_DIM)), l_h)
        o_ref[:, cs] = o_h
        l_ref[:, cs] = l_h


def _attn_sample(a_mat, cache, layer, o_prev, l_prev, g, seq, dec_seq):
    window, dil = ATT_GROUPS[g]
    depth, B, wb = cache.shape[0], cache.shape[1], cache.shape[2]
    assert wb == window and B % 2 == 0 and dec_seq == 4 and seq % (2 * dec_seq) == 0
    rows8 = 2 * dec_seq
    R = min(wb, BAND)
    stride = wb // R
    nsub = min(stride, dec_seq)
    cv = cache.reshape(depth, B, R, stride * 2 * HEADS, HEAD_DIM)
    base = seq // rows8
    c0 = 3 * g

    def new(c):
        return pl.BlockSpec((rows8, ATT_WIDTH), lambda i: (base + i, c0 + c))

    out_spec = pl.BlockSpec((rows8, ATT_WIDTH), lambda i: (base + i, 0))
    any_spec = pl.BlockSpec(memory_space=pl.ANY)
    return pl.pallas_call(
        functools.partial(_attn_sample_kernel, slopes=_alibi_slopes()[g], window=window, dil=dil,
                          dec_seq=dec_seq, nsub=nsub),
        grid=(B // 2,),
        in_specs=[new(0), new(1), new(2),
                  pl.BlockSpec((None, 2, R, nsub * 2 * HEADS, HEAD_DIM), lambda i: (layer, i, 0, 0, 0)),
                  any_spec, any_spec],
        out_specs=[out_spec, out_spec],
        out_shape=[jax.ShapeDtypeStruct(o_prev.shape, F32), jax.ShapeDtypeStruct(l_prev.shape, F32)],
        input_output_aliases={4: 0, 5: 1},
        compiler_params=_cparams(1),
        name=f"attn_sample_g{g}",
    )(a_mat, a_mat, a_mat, cv, o_prev, l_prev)


def _log_decay(ga, wa, ba):
    z = jnp.dot(ga, wa, preferred_element_type=F32, precision=lax.Precision.HIGHEST) + ba
    return (jnp.minimum(z, 0.0) - jnp.log(1.0 + jnp.exp(-jnp.abs(z)))) * (1.0 / GLA_TAU)


def _gla_prompt_kernel(q_ref, k_ref, v_ref, ga_ref, wa_ref, ba_ref, o_ref, sfin_ref, st_ref, la_ref,
                      qe_ref, upd_ref, ebl_ref):
    c = pl.program_id(1)

    @pl.when(c < pl.num_programs(1) - 1)
    def _():
        _gla_prompt_chunk(q_ref, k_ref, v_ref, ga_ref, wa_ref, ba_ref, o_ref, sfin_ref, st_ref, la_ref,
                      qe_ref, upd_ref, ebl_ref)

    @pl.when(c == pl.num_programs(1) - 1)
    def _():
        o_ref[...] = jnp.zeros_like(o_ref)


def _gla_prompt_chunk(q_ref, k_ref, v_ref, ga_ref, wa_ref, ba_ref, o_ref, sfin_ref, st_ref, la_ref,
                      qe_ref, upd_ref, ebl_ref):
    c = pl.program_id(1)
    C = GLA_CHUNK
    per_run = C // RUNS
    nsb = C // GLA_SUB
    sub_w = GLA_SUB // RUNS

    @pl.when(c == 0)
    def _():
        st_ref[...] = jnp.zeros_like(st_ref)

    la_ref[...] = _log_decay(ga_ref[...], wa_ref[...], ba_ref[...])

    r_i = lax.broadcasted_iota(jnp.int32, (C, C), 0)
    c_i = lax.broadcasted_iota(jnp.int32, (C, C), 1)
    tok_r = RUNS * (r_i % per_run) + r_i // per_run
    tok_c = RUNS * (c_i % per_run) + c_i // per_run
    tri = (tok_c <= tok_r).astype(F32)
    diag = (tok_r // GLA_SUB == tok_c // GLA_SUB) & (tok_c <= tok_r)
    rowblk = (lax.broadcasted_iota(jnp.int32, (C, GLA_HK), 0) % per_run) // sub_w
    last_run = (RUNS - 1) * per_run

    def gather(ref, j):
        return jnp.concatenate(
            [ref[r * RUN_LEN + j * per_run: r * RUN_LEN + (j + 1) * per_run, :] for r in range(RUNS)], axis=0)

    def scatter(ref, j, val, add):
        for r in range(RUNS):
            rows = slice(r * RUN_LEN + j * per_run, r * RUN_LEN + (j + 1) * per_run)
            piece = val[r * per_run:(r + 1) * per_run, :]
            ref[rows, :] = ref[rows, :] + piece if add else piece

    def local(js):
        def ref_rows(b, i):
            end_row = last_run + (i + 1) * sub_w - 1
            return b[end_row - sub_w:end_row - sub_w + 1, :], b[end_row:end_row + 1, :]

        b = [jnp.dot(tri, gather(la_ref, j), preferred_element_type=F32, precision=lax.Precision.HIGHEST) for j in js]
        q = [gather(q_ref, j) * (GLA_HK ** -0.5) for j in js]
        k = [gather(k_ref, j) for j in js]
        v = [gather(v_ref, j).astype(BF16) for j in js]
        qd, kd, ke, bstart = [], [], [], []
        for t, j in enumerate(js):
            qe_ref[j] = (q[t] * jnp.exp(b[t])).astype(BF16)
            bs = jnp.zeros_like(b[t])
            be = jnp.zeros_like(b[t])
            for i in range(nsb):
                prev_end, own_end = ref_rows(b[t], i)
                be = jnp.where(rowblk == i, own_end, be)
                if i > 0:
                    bs = jnp.where(rowblk == i, prev_end, bs)
            bstart.append(bs)
            qd.append(q[t] * jnp.exp(b[t] - bs))
            kd.append((k[t] * jnp.exp(bs - b[t])).astype(BF16))
            ke.append(k[t] * jnp.exp(be - b[t]))
        att = [jnp.where(diag, lax.dot_general(qd[t].astype(BF16), kd[t], _NT, preferred_element_type=F32), 0.0)
               for t in range(len(js))]
        for i in range(nsb - 1):
            qj, kej = [], []
            for t in range(len(js)):
                own_end = ref_rows(b[t], i)[1]
                dj = jnp.where(rowblk > i, jnp.exp(jnp.minimum(bstart[t] - own_end, 0.0)), 0.0)
                qj.append((qd[t] * dj).astype(BF16))
                kej.append(jnp.where(rowblk == i, ke[t], 0.0).astype(BF16))
            att = [att[t] + lax.dot_general(qj[t], kej[t], _NT, preferred_element_type=F32) for t in range(len(js))]
        intra = [jnp.dot(att[t].astype(BF16), v[t], preferred_element_type=F32) for t in range(len(js))]
        kdec = [(k[t] * jnp.exp(b[t][C - 1:C, :] - b[t])).astype(BF16) for t in range(len(js))]
        upd = [lax.dot_general(v[t], kdec[t], _TN, preferred_element_type=F32) for t in range(len(js))]
        for t, j in enumerate(js):
            scatter(o_ref, j, intra[t], add=False)
            upd_ref[j] = upd[t]
            ebl_ref[j] = jnp.broadcast_to(jnp.exp(b[t][C - 1:C, :]), (SUBLANES, GLA_HK))

    def recur(j):
        st = st_ref[...]
        scatter(o_ref, j, lax.dot_general(qe_ref[j], st.astype(BF16), _NT, preferred_element_type=F32), add=True)
        st_ref[...] = st * ebl_ref[j][0:1, :] + upd_ref[j]

    n_steps = CHUNK // C
    for j0 in range(0, n_steps, GLA_GROUP):
        local(list(range(j0, j0 + GLA_GROUP)))
    for j in range(n_steps):
        recur(j)

    @pl.when(c == pl.num_programs(1) - 2)
    def _():
        sfin_ref[...] = st_ref[...].T


def _gla_prompt(b, wa_pad, ba, seq):
    T = b.shape[0]
    assert seq % CHUNK == 0
    q0 = GLA_DV // GLA_HK
    k0 = q0 + GLA_DK // GLA_HK
    a0 = k0 + GLA_DK // GLA_HK
    og, sfin = pl.pallas_call(
        _gla_prompt_kernel,
        grid=(GLA_HEADS, seq // CHUNK + 1),
        in_specs=[pl.BlockSpec((CHUNK, GLA_HK), lambda h, n: (n, q0 + h)),
                  pl.BlockSpec((CHUNK, GLA_HK), lambda h, n: (n, k0 + h)),
                  pl.BlockSpec((CHUNK, GLA_HV), lambda h, n: (n, h)),
                  pl.BlockSpec((CHUNK, LANE), lambda h, n: (n, a0)),
                  pl.BlockSpec((LANE, GLA_HK), lambda h, n: (0, h)),
                  pl.BlockSpec((1, GLA_HK), lambda h, n: (0, h))],
        out_specs=[pl.BlockSpec((CHUNK, GLA_HV), lambda h, n: (n, h)),
                   pl.BlockSpec((None, GLA_HK, GLA_HV), lambda h, n: (h, 0, 0))],
        out_shape=[jax.ShapeDtypeStruct((T, GLA_DV), F32),
                   jax.ShapeDtypeStruct((GLA_HEADS, GLA_HK, GLA_HV), F32)],
        scratch_shapes=[pltpu.VMEM((GLA_HV, GLA_HK), F32), pltpu.VMEM((CHUNK, GLA_HK), F32),
                        pltpu.VMEM((CHUNK // GLA_CHUNK, GLA_CHUNK, GLA_HK), BF16),
                        pltpu.VMEM((CHUNK // GLA_CHUNK, GLA_HV, GLA_HK), F32),
                        pltpu.VMEM((CHUNK // GLA_CHUNK, SUBLANES, GLA_HK), F32)],
        compiler_params=_cparams(2),
        name="gla_prompt",
    )(b, b, b, b, wa_pad, ba)
    return og, sfin


def _gla_sample_kernel(q_ref, k_ref, v_ref, ga_ref, wa_ref, ba_ref, s0_ref, og_in, o_ref, snew_ref, *, dec_seq):
    del og_in
    rows8 = 2 * dec_seq
    r_i = lax.broadcasted_iota(jnp.int32, (rows8, rows8), 0)
    c_i = lax.broadcasted_iota(jnp.int32, (rows8, rows8), 1)
    same = (r_i // dec_seq == c_i // dec_seq) & (c_i <= r_i)
    tri = same.astype(F32)
    la = _log_decay(ga_ref[...], wa_ref[...], ba_ref[...])
    b_all = jnp.dot(tri, la, preferred_element_type=F32, precision=lax.Precision.HIGHEST)
    row_seq_k = lax.broadcasted_iota(jnp.int32, (rows8, GLA_HK), 0) // dec_seq
    row_seq_v = lax.broadcasted_iota(jnp.int32, (rows8, GLA_HV), 0) // dec_seq
    for h in range(GLA_HEADS):
        ks = slice(h * GLA_HK, (h + 1) * GLA_HK)
        vs = slice(h * GLA_HV, (h + 1) * GLA_HV)
        b = b_all[:, ks]
        q = q_ref[:, ks] * (GLA_HK ** -0.5)
        k = k_ref[:, ks]
        v = v_ref[:, vs].astype(BF16)
        qd = (q * jnp.exp(b)).astype(BF16)
        kd = (k * jnp.exp(-b)).astype(BF16)
        att = jnp.where(same, lax.dot_general(qd, kd, _NT, preferred_element_type=F32), 0.0)
        o = jnp.dot(att.astype(BF16), v, preferred_element_type=F32)
        for bb in range(2):
            st = s0_ref[bb, h].T
            inter = lax.dot_general(qd, st.astype(BF16), _NT, preferred_element_type=F32)
            o = o + jnp.where(row_seq_v == bb, inter, 0.0)
            bl = b[(bb + 1) * dec_seq - 1:(bb + 1) * dec_seq, :]
            kdec = jnp.where(row_seq_k == bb, k * jnp.exp(bl - b), 0.0).astype(BF16)
            upd = lax.dot_general(v, kdec, _TN, preferred_element_type=F32)
            snew_ref[bb, h] = (st * jnp.exp(bl) + upd).T
        o_ref[:, vs] = o


def _gla_sample(b, wa_pad, ba, state, layer, og_prev, seq, dec_seq):
    B = state.shape[1]
    rows8 = 2 * dec_seq
    base = seq // rows8
    q0 = GLA_DV // GLA_DK
    a0 = (GLA_DV + 2 * GLA_DK) // LANE
    return pl.pallas_call(
        functools.partial(_gla_sample_kernel, dec_seq=dec_seq),
        grid=(B // 2,),
        in_specs=[pl.BlockSpec((rows8, GLA_DK), lambda i: (base + i, q0)),
                  pl.BlockSpec((rows8, GLA_DK), lambda i: (base + i, q0 + 1)),
                  pl.BlockSpec((rows8, GLA_DV), lambda i: (base + i, 0)),
                  pl.BlockSpec((rows8, LANE), lambda i: (base + i, a0)),
                  pl.BlockSpec((LANE, GLA_DK), lambda i: (0, 0)),
                  pl.BlockSpec((1, GLA_DK), lambda i: (0, 0)),
                  pl.BlockSpec((None, 2, GLA_HEADS, GLA_HK, GLA_HV), lambda i: (layer, i, 0, 0, 0)),
                  pl.BlockSpec(memory_space=pl.ANY)],
        out_specs=[pl.BlockSpec((rows8, GLA_DV), lambda i: (base + i, 0)),
                   pl.BlockSpec((2, GLA_HEADS, GLA_HK, GLA_HV), lambda i: (i, 0, 0, 0))],
        out_shape=[jax.ShapeDtypeStruct(og_prev.shape, F32), jax.ShapeDtypeStruct(state.shape[1:], F32)],
        input_output_aliases={7: 0},
        compiler_params=_cparams(1),
        name="gla_sample",
    )(b, b, b, b, wa_pad, ba, state, og_prev)


def _post_kernel(o1, o2, o3, l1, l2, l3, og_ref, xin_ref, x_ref,
                 wgate_ref, wpa_ref, wpb_ref, wout_ref, gn_ref, lg_ref, lb_ref, xo_ref, xb_ref, *, alpha):
    gates = jnp.dot(xin_ref[...], wgate_ref[...], preferred_element_type=F32)
    D = x_ref.shape[1]
    gr, gate_a, gate_b = gates[:, :GLA_DV], gates[:, GLA_DV:GLA_DV + D], gates[:, GLA_DV + D:]
    m = jnp.maximum(jnp.maximum(l1[...], l2[...]), l3[...])
    e1 = jnp.exp(l1[...] - m)
    e2 = jnp.exp(l2[...] - m)
    e3 = jnp.exp(l3[...] - m)
    att = (e1 * o1[...] + e2 * o2[...] + e3 * o3[...]) / (e1 + e2 + e3)
    ya = jnp.dot(att.astype(BF16), wpa_ref[...], preferred_element_type=F32)
    parts = []
    for h in range(GLA_HEADS):
        vs = slice(h * GLA_HV, (h + 1) * GLA_HV)
        og = og_ref[:, vs]
        ms = jnp.mean(og * og, axis=-1, keepdims=True)
        parts.append((og * lax.rsqrt(ms + RMS_EPS) * gn_ref[:, vs] * _silu(gr[:, vs])).astype(BF16))
    y = jnp.concatenate(parts, axis=-1)
    yb = jnp.dot(y, wpb_ref[...], preferred_element_type=F32)
    merged = _sigmoid(gate_a) * ya + _sigmoid(gate_b) * yb
    mix = jnp.dot(merged.astype(BF16), wout_ref[...], preferred_element_type=F32)
    xn = _layer_norm_rows(alpha * x_ref[...] + mix, lg_ref[...], lb_ref[...])
    xo_ref[...] = xn
    xb_ref[...] = xn.astype(BF16)


def _post(os_, ls_, og, xb, x, wgate, wpa, wpb, wout, gn, lg, lb, alpha, tm):
    T, D = x.shape
    row = lambda w: pl.BlockSpec((tm, w), lambda i: (i, 0))
    full = lambda a: pl.BlockSpec(a.shape, lambda i: (0,) * a.ndim)
    resident = lambda a: pl.BlockSpec(a.shape, lambda i: (0,) * a.ndim, pipeline_mode=pl.Buffered(1))
    return pl.pallas_call(
        functools.partial(_post_kernel, alpha=alpha),
        grid=(pl.cdiv(T, tm),),
        in_specs=[row(ATT_WIDTH)] * 6 + [row(GLA_DV), row(D), row(D),
                                          resident(wgate), resident(wpa), resident(wpb), resident(wout),
                                          full(gn), full(lg), full(lb)],
        out_specs=[row(D), row(D)],
        out_shape=[jax.ShapeDtypeStruct((T, D), F32), jax.ShapeDtypeStruct((T, D), BF16)],
        compiler_params=_cparams(1),
        name="post_mixer",
    )(*os_, *ls_, og, xb, x, wgate, wpa, wpb, wout, gn, lg, lb)


def _ffn_kernel(xb_ref, x_ref, wg_ref, wu_ref, wd_ref, lg_ref, lb_ref, xo_ref, xbo_ref, acc_ref, *, alpha):
    j = pl.program_id(1)
    xb = xb_ref[...]
    g = jnp.dot(xb, wg_ref[...], preferred_element_type=F32)
    u = jnp.dot(xb, wu_ref[...], preferred_element_type=F32)
    part = jnp.dot((_silu(g) * u).astype(BF16), wd_ref[...], preferred_element_type=F32)

    @pl.when(j == 0)
    def _():
        acc_ref[...] = part

    @pl.when(j > 0)
    def _():
        acc_ref[...] += part

    @pl.when(j == pl.num_programs(1) - 1)
    def _():
        xn = _layer_norm_rows(alpha * x_ref[...] + acc_ref[...], lg_ref[...], lb_ref[...])
        xo_ref[...] = xn
        xbo_ref[...] = xn.astype(BF16)


def _ffn(xb, x, wgu, wd, lg, lb, alpha, tm, tc):
    T, D = x.shape
    dff = wd.shape[0]
    assert dff % tc == 0
    nj = dff // tc
    full = lambda a: pl.BlockSpec(a.shape, lambda i, j: (0,) * a.ndim)
    wmode = dict(pipeline_mode=pl.Buffered(1)) if nj == 1 else {}
    return pl.pallas_call(
        functools.partial(_ffn_kernel, alpha=alpha),
        grid=(pl.cdiv(T, tm), nj),
        in_specs=[pl.BlockSpec((tm, D), lambda i, j: (i, 0)),
                  pl.BlockSpec((tm, D), lambda i, j: (i, 0)),
                  pl.BlockSpec((D, tc), lambda i, j: (0, j), **wmode),
                  pl.BlockSpec((D, tc), lambda i, j: (0, nj + j), **wmode),
                  pl.BlockSpec((tc, D), lambda i, j: (j, 0), **wmode),
                  full(lg), full(lb)],
        out_specs=[pl.BlockSpec((tm, D), lambda i, j: (i, 0))] * 2,
        out_shape=[jax.ShapeDtypeStruct((T, D), F32), jax.ShapeDtypeStruct((T, D), BF16)],
        scratch_shapes=[pltpu.VMEM((tm, D), F32)],
        compiler_params=_cparams(2),
        name="ffn_dense",
    )(xb, x, wgu, wgu, wd, lg, lb)


MOE_TOKENS = 512
MOE_PIECE = 16
MOE_ROWS = -(-(2 * MOE_TOKENS + N_EXPERTS * (MOE_PIECE - 1)) // MOE_PIECE) * MOE_PIECE
MOE_BLOCK = 1024
MOE_HALF = 512
MOE_SLOTS = MOE_BLOCK // MOE_PIECE


def _moe_route_kernel(x_ref, xb_ref, wrt_ref, xs_ref, meta_ref, cnt_ref, *, n_tokens):
    i = pl.program_id(0)
    ts = x_ref.shape[0]
    row_ok = (lax.broadcasted_iota(jnp.int32, (ts, 1), 0) + i * ts) < n_tokens
    x = jnp.where(row_ok, x_ref[...], 0.0)
    xb = jnp.where(row_ok, xb_ref[...], jnp.zeros((), BF16))
    lt = lax.dot_general(wrt_ref[...], x, _NT, preferred_element_type=F32, precision=lax.Precision.HIGHEST)
    sub = lax.broadcasted_iota(jnp.int32, (N_EXPERTS, ts), 0).astype(F32)
    tok_ok = (lax.broadcasted_iota(jnp.int32, (1, ts), 1) + i * ts) < n_tokens
    m1 = jnp.max(lt, axis=0, keepdims=True)
    i1 = jnp.min(jnp.where(lt == m1, sub, float(N_EXPERTS)), axis=0, keepdims=True)
    lt2 = jnp.where(sub == i1, NEG, lt)
    m2 = jnp.max(lt2, axis=0, keepdims=True)
    i2 = jnp.min(jnp.where(lt2 == m2, sub, float(N_EXPERTS)), axis=0, keepdims=True)
    e2 = jnp.exp(m2 - m1)
    g1 = 1.0 / (1.0 + e2)
    g2 = e2 / (1.0 + e2)
    oh1 = (sub == i1) & tok_ok
    oh2 = (sub == i2) & tok_ok
    assign = jnp.where(oh1 | oh2, 1.0, 0.0)
    before = (lax.broadcasted_iota(jnp.int32, (ts, ts), 0) < lax.broadcasted_iota(jnp.int32, (ts, ts), 1))
    rank = jnp.dot(assign.astype(BF16), jnp.where(before, 1.0, 0.0).astype(BF16), preferred_element_type=F32)
    cnt = jnp.sum(assign, axis=1, keepdims=True)
    padded = jnp.floor((cnt + (MOE_PIECE - 1)) * (1.0 / MOE_PIECE)) * MOE_PIECE
    lower = (lax.broadcasted_iota(jnp.int32, (N_EXPERTS, N_EXPERTS), 1)
             < lax.broadcasted_iota(jnp.int32, (N_EXPERTS, N_EXPERTS), 0)).astype(F32)
    seg_off = jnp.dot(lower, jnp.broadcast_to(padded, (N_EXPERTS, ts)), preferred_element_type=F32,
                      precision=lax.Precision.HIGHEST)
    dest = seg_off + rank
    d1 = jnp.where(tok_ok, jnp.sum(jnp.where(oh1, dest, 0.0), axis=0, keepdims=True), -1.0)
    d2 = jnp.where(tok_ok, jnp.sum(jnp.where(oh2, dest, 0.0), axis=0, keepdims=True), -1.0)
    p_i = lax.broadcasted_iota(jnp.int32, (MOE_ROWS, ts), 0).astype(F32)
    perm = jnp.where((p_i == d1) | (p_i == d2), 1.0, 0.0).astype(BF16)
    xs_ref[...] = jnp.dot(perm, xb, preferred_element_type=F32).astype(BF16)
    sub8 = lax.broadcasted_iota(jnp.int32, (SUBLANES, ts), 0)
    meta_ref[...] = jnp.where(sub8 == 0, d1, jnp.where(sub8 == 1, d2, jnp.where(sub8 == 2, g1,
                              jnp.where(sub8 == 3, g2, 0.0))))
    cnt_ref[...] = jnp.broadcast_to(cnt, (N_EXPERTS, LANE))


def _moe_route(x, xb, wr_t):
    T, D = x.shape
    nt = pl.cdiv(T, MOE_TOKENS)
    return pl.pallas_call(
        functools.partial(_moe_route_kernel, n_tokens=T),
        grid=(nt,),
        in_specs=[pl.BlockSpec((MOE_TOKENS, D), lambda i: (i, 0)),
                  pl.BlockSpec((MOE_TOKENS, D), lambda i: (i, 0)),
                  pl.BlockSpec(wr_t.shape, lambda i: (0, 0))],
        out_specs=[pl.BlockSpec((MOE_ROWS, D), lambda i: (i, 0)),
                   pl.BlockSpec((None, SUBLANES, MOE_TOKENS), lambda i: (i, 0, 0)),
                   pl.BlockSpec((None, N_EXPERTS, LANE), lambda i: (i, 0, 0))],
        out_shape=[jax.ShapeDtypeStruct((nt * MOE_ROWS, D), BF16),
                   jax.ShapeDtypeStruct((nt, SUBLANES, MOE_TOKENS), F32),
                   jax.ShapeDtypeStruct((nt, N_EXPERTS, LANE), F32)],
        compiler_params=_cparams(1),
        name="moe_route",
    )(x, xb, wr_t)


def _moe_plan(cnt, n_blocks):
    nt = cnt.shape[0]
    i32 = jnp.int32
    padded = (cnt + MOE_PIECE - 1) // MOE_PIECE * MOE_PIECE
    seg_row = jnp.cumsum(padded, axis=1) - padded + (jnp.arange(nt, dtype=i32) * MOE_ROWS)[:, None]
    n_seg = (padded // MOE_PIECE).T.reshape(-1)
    seg_row = seg_row.T.reshape(-1)
    seg_end = jnp.cumsum(n_seg)
    seg_start = seg_end - n_seg
    per_e = n_seg.reshape(N_EXPERTS, nt).sum(axis=1)
    e_end = jnp.cumsum(per_e)
    e_start = e_end - per_e
    blocks_e = (per_e + MOE_SLOTS - 1) // MOE_SLOTS
    blk_end = jnp.cumsum(blocks_e)
    blk_start = blk_end - blocks_e
    w = jnp.arange(n_blocks, dtype=i32)
    ew = jnp.minimum(jnp.sum((w[:, None] >= blk_end[None, :]).astype(i32), axis=1), N_EXPERTS - 1)
    is_e = ew[:, None] == jnp.arange(N_EXPERTS, dtype=i32)[None, :]
    pick = lambda v: jnp.sum(jnp.where(is_e, v[None, :], 0), axis=1)
    first = pick(e_start) + (w - pick(blk_start)) * MOE_SLOTS
    n_w = jnp.where(w < blk_end[-1], jnp.clip(pick(e_end) - first, 0, MOE_SLOTS), 0).astype(i32)
    slot = jnp.arange(MOE_SLOTS, dtype=i32)
    p = first[:, None] + slot[None, :]
    in_seg = (p[:, :, None] >= seg_start[None, None, :]) & (p[:, :, None] < seg_end[None, None, :])
    rows = jnp.sum(jnp.where(in_seg, (seg_row - MOE_PIECE * seg_start)[None, None, :], 0), axis=2) + MOE_PIECE * p
    rows = jnp.where(slot[None, :] < n_w[:, None], rows, 0).astype(i32)
    return ew.astype(i32), n_w, rows.reshape(-1)


def _moe_expert_kernel(ew_ref, n_ref, rows_ref, xs_hbm, wg_ref, wu_ref, wd_ref, ys_in, ys_hbm,
                       lhs_ref, acc_ref, out_ref, sem_in, sem_out):
    del ew_ref, ys_in
    w = pl.program_id(0)
    j = pl.program_id(1)
    n_w = pl.num_programs(0)
    last_j = pl.num_programs(1) - 1
    n = n_ref[w]
    slot = w % 2
    halves = [(s, slice(s * MOE_HALF, (s + 1) * MOE_HALF)) for s in range(MOE_BLOCK // MOE_HALF)]

    def piece_in(blk, q, sl):
        src = pl.multiple_of(rows_ref[blk * MOE_SLOTS + q], MOE_PIECE)
        dst = pl.multiple_of(q * MOE_PIECE, MOE_PIECE)
        return pltpu.make_async_copy(xs_hbm.at[pl.ds(src, MOE_PIECE), :],
                                     lhs_ref.at[sl, pl.ds(dst, MOE_PIECE), :], sem_in.at[sl])

    def piece_out(blk, q):
        dst = pl.multiple_of(rows_ref[blk * MOE_SLOTS + q], MOE_PIECE)
        src = pl.multiple_of(q * MOE_PIECE, MOE_PIECE)
        return pltpu.make_async_copy(out_ref.at[pl.ds(src, MOE_PIECE), :], ys_hbm.at[pl.ds(dst, MOE_PIECE), :], sem_out)

    def for_pieces(blk, fn):
        lax.fori_loop(0, n_ref[blk], lambda q, c: (fn(q), c)[1], 0)

    @pl.when(j == 0)
    def _():
        @pl.when(w == 0)
        def _():
            lhs_ref[...] = jnp.zeros_like(lhs_ref)
            for_pieces(0, lambda q: piece_in(0, q, 0).start())

        for_pieces(w, lambda q: piece_in(w, q, slot).wait())

        @pl.when(w + 1 < n_w)
        def _():
            for_pieces(w + 1, lambda q: piece_in(w + 1, q, 1 - slot).start())

    for s, rows in halves:
        @pl.when(n > s * (MOE_HALF // MOE_PIECE))
        def _(rows=rows):
            xb = lhs_ref[slot, rows, :]
            g = jnp.dot(xb, wg_ref[...], preferred_element_type=F32)
            u = jnp.dot(xb, wu_ref[...], preferred_element_type=F32)
            part = jnp.dot((_silu(g) * u).astype(BF16), wd_ref[...], preferred_element_type=F32)

            @pl.when(j == 0)
            def _():
                acc_ref[rows, :] = part

            @pl.when(j > 0)
            def _():
                acc_ref[rows, :] += part

    @pl.when(j == last_j)
    def _():
        @pl.when(w > 0)
        def _():
            for_pieces(w - 1, lambda q: piece_out(w - 1, q).wait())

        for s, rows in halves:
            @pl.when(n > s * (MOE_HALF // MOE_PIECE))
            def _(rows=rows):
                out_ref[rows, :] = acc_ref[rows, :].astype(BF16)

        for_pieces(w, lambda q: piece_out(w, q).start())

        @pl.when(w == n_w - 1)
        def _():
            for_pieces(w, lambda q: piece_out(w, q).wait())


def _moe_experts(xs, plan, wgu, wd, tc):
    ew, n_w, rows = plan
    n_blocks = ew.shape[0]
    D = xs.shape[1]
    dff = wd.shape[1]
    assert dff % tc == 0
    nj = dff // tc
    any_spec = pl.BlockSpec(memory_space=pl.ANY)
    return pl.pallas_call(
        _moe_expert_kernel,
        grid_spec=pltpu.PrefetchScalarGridSpec(
            num_scalar_prefetch=3,
            grid=(n_blocks, nj),
            in_specs=[any_spec,
                      pl.BlockSpec((None, D, tc), lambda w, j, ew, n, r: (ew[w], 0, j)),
                      pl.BlockSpec((None, D, tc), lambda w, j, ew, n, r: (ew[w], 0, nj + j)),
                      pl.BlockSpec((None, tc, D), lambda w, j, ew, n, r: (ew[w], j, 0)),
                      any_spec],
            out_specs=any_spec,
            scratch_shapes=[pltpu.VMEM((2, MOE_BLOCK, D), BF16), pltpu.VMEM((MOE_BLOCK, D), F32),
                            pltpu.VMEM((MOE_BLOCK, D), BF16),
                            pltpu.SemaphoreType.DMA((2,)), pltpu.SemaphoreType.DMA(())]),
        out_shape=jax.ShapeDtypeStruct(xs.shape, BF16),
        input_output_aliases={7: 0},
        compiler_params=_cparams(2),
        name="moe_experts",
    )(ew, n_w, rows, xs, wgu, wgu, wd, jnp.zeros(xs.shape, BF16))


def _moe_combine_kernel(ys_ref, meta_ref, x_ref, lg_ref, lb_ref, xo_ref, xbo_ref, *, alpha):
    ts = x_ref.shape[0]
    meta = meta_ref[...]
    d1, d2, g1, g2 = meta[0:1, :], meta[1:2, :], meta[2:3, :], meta[3:4, :]
    p_i = lax.broadcasted_iota(jnp.int32, (MOE_ROWS, ts), 0).astype(F32)
    back = (jnp.where(p_i == d1, g1, 0.0) + jnp.where(p_i == d2, g2, 0.0)).astype(BF16)
    f = lax.dot_general(back, ys_ref[...], _TN, preferred_element_type=F32)
    xn = _layer_norm_rows(alpha * x_ref[...] + f, lg_ref[...], lb_ref[...])
    xo_ref[...] = xn
    xbo_ref[...] = xn.astype(BF16)


def _moe_combine(ys, meta, x, lg, lb, alpha):
    T, D = x.shape
    nt = meta.shape[0]
    full = lambda a: pl.BlockSpec(a.shape, lambda i: (0,) * a.ndim)
    return pl.pallas_call(
        functools.partial(_moe_combine_kernel, alpha=alpha),
        grid=(nt,),
        in_specs=[pl.BlockSpec((MOE_ROWS, D), lambda i: (i, 0)),
                  pl.BlockSpec((None, SUBLANES, MOE_TOKENS), lambda i: (i, 0, 0)),
                  pl.BlockSpec((MOE_TOKENS, D), lambda i: (i, 0)),
                  full(lg), full(lb)],
        out_specs=[pl.BlockSpec((MOE_TOKENS, D), lambda i: (i, 0))] * 2,
        out_shape=[jax.ShapeDtypeStruct((T, D), F32), jax.ShapeDtypeStruct((T, D), BF16)],
        compiler_params=_cparams(1),
        name="moe_combine",
    )(ys, meta, x, lg, lb)


def _moe(x, xb, w_router, wgu, wd, lg, lb, alpha):
    T = x.shape[0]
    nt = pl.cdiv(T, MOE_TOKENS)
    xs, meta, cnt = _moe_route(x, xb, w_router.T)
    n_pieces = (2 * T + nt * N_EXPERTS * (MOE_PIECE - 1)) // MOE_PIECE + 1
    n_blocks = pl.cdiv(n_pieces, MOE_SLOTS) + N_EXPERTS
    plan = _moe_plan(cnt[:, :, 0].astype(jnp.int32), n_blocks)
    ys = _moe_experts(xs, plan, wgu, wd, MOE_CHUNK)
    return _moe_combine(ys, meta, x, lg, lb, alpha)


def _split_w_in(w):
    a_end = N_GROUPS * 3 * ATT_WIDTH
    gq0, gk0, gv0 = a_end, a_end + GLA_DK, a_end + 2 * GLA_DK
    gr0 = gv0 + GLA_DV
    ga0 = gr0 + GLA_DV
    gta0 = ga0 + GLA_RANK
    wa = w[:, :a_end].astype(BF16)
    used = GLA_DV + 2 * GLA_DK + GLA_RANK
    ga_pad = jnp.pad(w[:, ga0:gta0], ((0, 0), (0, GLA_PANEL - used)))
    wb = jnp.concatenate([w[:, gv0:gr0], w[:, gq0:gk0], w[:, gk0:gv0], ga_pad], axis=1).astype(BF16)
    wg = jnp.concatenate([w[:, gr0:ga0], w[:, gta0:]], axis=1).astype(BF16)
    return wa, wb, wg


def _new_kv_rows(a_mat, g, seq, db, dec_seq):
    keep = min(ATT_GROUPS[g][0], seq)
    assert keep <= CHUNK and keep % RUNS == 0
    kv_cols = slice((3 * g + 1) * ATT_WIDTH, (3 * g + 3) * ATT_WIDTH)
    last = a_mat[seq - CHUNK:seq, kv_cols].reshape(RUNS, RUN_LEN, 2, HEADS, HEAD_DIM)
    kv_p = last[:, RUN_LEN - keep // RUNS:].transpose(1, 0, 2, 3, 4).reshape(1, keep, 2, HEADS, HEAD_DIM)
    kv_s = a_mat[seq:, kv_cols].reshape(db, dec_seq, 2, HEADS, HEAD_DIM)
    return kv_p, kv_s


def kernel(x_prompt, x_sample, cache_kv_w128, cache_kv_w512, cache_kv_w2048, state_gla,
           w_in, w_alpha2, b_alpha, gla_norm_g, w_pa, w_pb, w_out,
           ln_mix_g, ln_mix_b, ln_ffn_g, ln_ffn_b,
           w_ffn_gu, w_ffn_down, w_router, w_exp_gu, w_exp_down):
    depth = w_in.shape[0]
    bp, seq, D = x_prompt.shape
    db, dec_seq, _ = x_sample.shape
    assert bp == 1
    ns = db * dec_seq
    alpha = (2 * depth) ** 0.25
    caches = (cache_kv_w128, cache_kv_w512, cache_kv_w2048)

    x = jnp.concatenate([_interleave_rows(x_prompt.reshape(seq, D)), x_sample.reshape(ns, D)], axis=0)
    xb = x.astype(BF16)
    kv_p = [[] for _ in range(N_GROUPS)]
    kv_s = [[] for _ in range(N_GROUPS)]
    st_p, st_s = [], []
    row2 = lambda v: v.reshape(1, -1)

    for l in range(depth):
        wa, wb, wg = _split_w_in(w_in[l])
        a_mat = _mm(xb, wa, PROJ_ROWS, PROJ_COLS_ATT)
        b_mat = _mm(xb, wb, PROJ_ROWS, GLA_PANEL)

        os_, ls_ = [], []
        for g in range(N_GROUPS):
            o, lse = _attn_prompt(a_mat, g, seq)
            o, lse = _attn_sample(a_mat, caches[g], l, o, lse, g, seq, dec_seq)
            os_.append(o)
            ls_.append(lse)
            p_rows, s_rows = _new_kv_rows(a_mat, g, seq, db, dec_seq)
            kv_p[g].append(p_rows)
            kv_s[g].append(s_rows)

        wa_pad = jnp.pad(w_alpha2[l], ((0, LANE - GLA_RANK), (0, 0)))
        og, s_p = _gla_prompt(b_mat, wa_pad, row2(b_alpha[l]), seq)
        og, s_s = _gla_sample(b_mat, wa_pad, row2(b_alpha[l]), state_gla, l, og, seq, dec_seq)
        st_p.append(s_p[None])
        st_s.append(s_s)

        x, xb = _post(os_, ls_, og, xb, x, wg, w_pa[l].astype(BF16), w_pb[l].astype(BF16), w_out[l].astype(BF16),
                      row2(gla_norm_g[l]), row2(ln_mix_g[l]), row2(ln_mix_b[l]), alpha, POST_ROWS)

        if l % 2 == 0:
            x, xb = _ffn(xb, x, w_ffn_gu[l // 2].astype(BF16), w_ffn_down[l // 2].astype(BF16),
                         row2(ln_ffn_g[l]), row2(ln_ffn_b[l]), alpha, FFN_ROWS, FFN_CHUNK)
        else:
            x, xb = _moe(x, xb, w_router[l // 2], w_exp_gu[l // 2].astype(BF16), w_exp_down[l // 2].astype(BF16),
                         row2(ln_ffn_g[l]), row2(ln_ffn_b[l]), alpha)

    return (_deinterleave_rows(x[:seq]).reshape(1, seq, D), x[seq:].reshape(db, dec_seq, D),
            jnp.stack(kv_p[0], 0), jnp.stack(kv_s[0], 0),
            jnp.stack(kv_p[1], 0), jnp.stack(kv_s[1], 0),
            jnp.stack(kv_p[2], 0), jnp.stack(kv_s[2], 0),
            jnp.stack(st_p, 0), jnp.stack(st_s, 0))
```

```python
import functools

import jax
import jax.numpy as jnp
from jax import lax
from jax.experimental import pallas as pl
from jax.experimental.pallas import tpu as pltpu

F32 = jnp.float32
BF16 = jnp.bfloat16

ATT_GROUPS = ((128, 1), (512, 4), (2048, 16))
N_GROUPS = 3
HEADS = 4
HEAD_DIM = 128
ATT_WIDTH = HEADS * HEAD_DIM
BAND = 128
ATT_GROUP = 8
GLA_HEADS = 4
GLA_HK = 128
GLA_HV = 256
GLA_DK = GLA_HEADS * GLA_HK
GLA_DV = GLA_HEADS * GLA_HV
GLA_RANK = 16
GLA_TAU = 16.0
GLA_CHUNK = 128
GLA_SUB = 32
GLA_GROUP = 8
N_EXPERTS = 8
LN_EPS = 1e-5
RMS_EPS = 1e-6
NEG = -1e30
LOG2E = 1.4426950408889634
LN2 = 0.6931471805599453
LANE = 128
SUBLANES = 8
RUNS = 16
RUN_LEN = 128
CHUNK = RUNS * RUN_LEN
VMEM_LIMIT = 56 * 1024 * 1024

PROJ_ROWS = 1024
PROJ_COLS_ATT = 1536
GLA_PANEL = 2304
POST_ROWS = 256
FFN_ROWS = 512
FFN_CHUNK = 2816
MOE_CHUNK = 1792
SAMPLE_COLS = 1024
SAMPLE_FFN_CHUNK = 256

_NT = (((1,), (1,)), ((), ()))
_TN = (((0,), (0,)), ((), ()))


def _alibi_slopes():
    n = N_GROUPS * HEADS
    return [[2.0 ** (-8.0 * (g * HEADS + h + 1) / n) for h in range(HEADS)] for g in range(N_GROUPS)]


def _cparams(n_axes):
    return pltpu.CompilerParams(dimension_semantics=("arbitrary",) * n_axes, vmem_limit_bytes=VMEM_LIMIT)


def _layer_norm_rows(r, g, b):
    mu = jnp.mean(r, axis=-1, keepdims=True)
    c = r - mu
    var = jnp.mean(c * c, axis=-1, keepdims=True)
    return c * lax.rsqrt(var + LN_EPS) * g + b


def _sigmoid(x):
    return 1.0 / (1.0 + jnp.exp(-x))


def _silu(x):
    return x * _sigmoid(x)


def _operand(x, precise):
    return x.astype(F32 if precise else BF16)


def _dot(a, b, precise, dims=None):
    def mm(u, v):
        if dims is None:
            return jnp.dot(u, v, preferred_element_type=F32)
        return lax.dot_general(u, v, dims, preferred_element_type=F32)

    if not precise:
        return mm(a.astype(BF16), b.astype(BF16))
    a, b = a.astype(F32), b.astype(F32)
    a_hi, b_hi = a.astype(BF16), b.astype(BF16)
    a_lo = (a - a_hi.astype(F32)).astype(BF16)
    b_lo = (b - b_hi.astype(F32)).astype(BF16)
    return mm(a_hi, b_hi) + (mm(a_hi, b_lo) + mm(a_lo, b_hi))


def _interleave_rows(x):
    s, d = x.shape
    return x.reshape(s // CHUNK, RUN_LEN, RUNS, d).transpose(0, 2, 1, 3).reshape(s, d)


def _deinterleave_rows(x):
    s, d = x.shape
    return x.reshape(s // CHUNK, RUNS, RUN_LEN, d).transpose(0, 2, 1, 3).reshape(s, d)


def _mm_kernel(x_ref, w_ref, o_ref, *, precise):
    o_ref[...] = _dot(x_ref[...], w_ref[...], precise)


def _mm(x, w, tm, tn, precise=False, layer=None):
    M, K = x.shape
    N = w.shape[-1]
    if layer is None:
        w_spec = pl.BlockSpec((K, tn), lambda j, i: (0, j))
    else:
        w_spec = pl.BlockSpec((None, K, tn), lambda j, i: (layer, 0, j))
    return pl.pallas_call(
        functools.partial(_mm_kernel, precise=precise),
        grid=(pl.cdiv(N, tn), pl.cdiv(M, tm)),
        in_specs=[pl.BlockSpec((tm, K), lambda j, i: (i, 0)), w_spec],
        out_specs=pl.BlockSpec((tm, tn), lambda j, i: (i, j)),
        out_shape=jax.ShapeDtypeStruct((M, N), F32),
        compiler_params=_cparams(2),
        name="in_proj_sample" if precise else "in_proj",
    )(x, w)


def _attn_prompt_kernel(slope_ref, q_ref, k_ref, v_ref, o_ref, l_ref, kp_ref, vp_ref, *, dil):
    h = pl.program_id(0)
    c = pl.program_id(1)
    G = RUNS // dil
    wq = BAND // G
    scale = HEAD_DIM ** -0.5

    @pl.when(c == 0)
    def _():
        kp_ref[...] = jnp.zeros_like(kp_ref)
        vp_ref[...] = jnp.zeros_like(vp_ref)

    qi = lax.broadcasted_iota(jnp.int32, (BAND, 2 * BAND), 0)
    ci = lax.broadcasted_iota(jnp.int32, (BAND, 2 * BAND), 1)
    e = ci % (2 * wq)
    rel = (G * (qi % wq) + qi // wq) - (G * (e - wq) + ci // (2 * wq))
    valid = (rel >= 0) & (rel <= BAND)
    bias = rel.astype(F32) * (-slope_ref[h] * (dil * LOG2E))
    bias_in = jnp.where(valid, bias, NEG)
    first_e = jnp.where(c > 0, 0, wq)
    bias_first = jnp.where(valid & (e >= first_e), bias, NEG)

    def keys(cur_ref, prev_ref, runs, m):
        if m == 0:
            parts = []
            for rho in runs:
                parts += [prev_ref[(rho + 1) * RUN_LEN - wq:(rho + 1) * RUN_LEN, :],
                          cur_ref[rho * RUN_LEN: rho * RUN_LEN + wq, :]]
        else:
            parts = [cur_ref[rho * RUN_LEN + wq * (m - 1): rho * RUN_LEN + wq * (m + 1), :] for rho in runs]
        return jnp.concatenate(parts, axis=0).astype(BF16)

    blocks = [([r + dil * v for v in range(G)], m) for r in range(dil) for m in range(G)]
    for g0 in range(0, len(blocks), ATT_GROUP):
        grp = blocks[g0:g0 + ATT_GROUP]
        q = [(jnp.concatenate([q_ref[rho * RUN_LEN + wq * m: rho * RUN_LEN + wq * (m + 1), :] for rho in runs],
                              axis=0) * (scale * LOG2E)).astype(BF16) for runs, m in grp]
        s = [lax.dot_general(q[t], keys(k_ref, kp_ref, runs, m), _NT, preferred_element_type=F32)
             + (bias_first if m == 0 else bias_in) for t, (runs, m) in enumerate(grp)]
        mx = [jnp.max(x, axis=-1, keepdims=True) for x in s]
        p = [jnp.exp2(s[t] - mx[t]) for t in range(len(grp))]
        l = [jnp.sum(x, axis=-1, keepdims=True) for x in p]
        o = [jnp.dot(p[t].astype(BF16), keys(v_ref, vp_ref, runs, m), preferred_element_type=F32) / l[t]
             for t, (runs, m) in enumerate(grp)]
        for t, (runs, m) in enumerate(grp):
            lse = jnp.broadcast_to(mx[t] * LN2 + jnp.log(l[t]), (BAND, HEAD_DIM))
            for idx, rho in enumerate(runs):
                rows = slice(rho * RUN_LEN + wq * m, rho * RUN_LEN + wq * (m + 1))
                o_ref[rows, :] = o[t][idx * wq:(idx + 1) * wq, :]
                l_ref[rows, :] = lse[idx * wq:(idx + 1) * wq, :]

    kp_ref[...] = k_ref[...]
    vp_ref[...] = v_ref[...]


def _attn_prompt(a_mat, g):
    T = seq = a_mat.shape[0]
    _, dil = ATT_GROUPS[g]
    assert seq % CHUNK == 0 and RUNS % dil == 0
    slopes = jnp.asarray(_alibi_slopes()[g], F32)

    def col(c):
        return pl.BlockSpec((CHUNK, HEAD_DIM), lambda h, n: (n, (3 * g + c) * HEADS + h))

    out_spec = pl.BlockSpec((CHUNK, HEAD_DIM), lambda h, n: (n, h))
    out_sds = jax.ShapeDtypeStruct((T, ATT_WIDTH), F32)
    return pl.pallas_call(
        functools.partial(_attn_prompt_kernel, dil=dil),
        grid=(HEADS, seq // CHUNK),
        in_specs=[pl.BlockSpec(memory_space=pltpu.SMEM), col(0), col(1), col(2)],
        out_specs=[out_spec, out_spec],
        out_shape=[out_sds, out_sds],
        scratch_shapes=[pltpu.VMEM((CHUNK, HEAD_DIM), F32), pltpu.VMEM((CHUNK, HEAD_DIM), F32)],
        compiler_params=_cparams(2),
        name=f"attn_prompt_g{g}",
    )(slopes, a_mat, a_mat, a_mat)


def _attn_sample_kernel(q_ref, k_ref, v_ref, c_ref, o_ref, l_ref, *, slopes, window, dil, dec_seq, nsub, precise):
    R = c_ref.shape[1]
    wb = window
    stride = wb // R
    rows8 = 2 * dec_seq
    scale = HEAD_DIM ** -0.5
    ncache = R * nsub

    ri = lax.broadcasted_iota(jnp.int32, (rows8, ncache), 0)
    ji = lax.broadcasted_iota(jnp.int32, (rows8, ncache), 1)
    t_row = ri % dec_seq
    c_true = (ji % R) * stride + ji // R
    dist_c = wb + t_row - c_true
    ok_c = ((dist_c & (dil - 1)) == 0) & (dist_c <= window)
    bias_c_base = jnp.where(ok_c, 0.0, NEG)
    dist_c_f = dist_c.astype(F32)

    rn = lax.broadcasted_iota(jnp.int32, (rows8, rows8), 0)
    jn = lax.broadcasted_iota(jnp.int32, (rows8, rows8), 1)
    dist_n = rn % dec_seq - jn % dec_seq
    ok_n = (rn // dec_seq == jn // dec_seq) & (dist_n >= 0) & ((dist_n & (dil - 1)) == 0)
    bias_n_base = jnp.where(ok_n, 0.0, NEG)
    dist_n_f = dist_n.astype(F32)
    row_seq = lax.broadcasted_iota(jnp.int32, (rows8, HEAD_DIM), 0) // dec_seq

    for h in range(HEADS):
        cs = slice(h * HEAD_DIM, (h + 1) * HEAD_DIM)
        q = q_ref[:, cs]
        kn = k_ref[:, cs]
        vn = v_ref[:, cs]
        s_n = _dot(q, kn, precise, _NT) * scale - dist_n_f * slopes[h] + bias_n_base
        o_h = jnp.zeros((rows8, HEAD_DIM), F32)
        l_h = jnp.zeros((rows8, HEAD_DIM), F32)
        for bb in range(2):
            kc = jnp.concatenate([c_ref[bb, :, u * 2 * HEADS + h, :] for u in range(nsub)], axis=0)
            vc = jnp.concatenate([c_ref[bb, :, u * 2 * HEADS + HEADS + h, :] for u in range(nsub)], axis=0)
            s_c = _dot(q, kc, precise, _NT) * scale - dist_c_f * slopes[h] + bias_c_base
            m = jnp.maximum(jnp.max(s_c, axis=-1, keepdims=True), jnp.max(s_n, axis=-1, keepdims=True))
            p_c = jnp.exp(s_c - m)
            p_n = jnp.exp(s_n - m)
            l = jnp.sum(p_c, axis=-1, keepdims=True) + jnp.sum(p_n, axis=-1, keepdims=True)
            o = (_dot(p_c, vc, precise) + _dot(p_n, vn, precise)) / l
            mine = row_seq == bb
            o_h = jnp.where(mine, o, o_h)
            l_h = jnp.where(mine, jnp.broadcast_to(m + jnp.log(l), (rows8, HEAD_DIM)), l_h)
        o_ref[:, cs] = o_h
        l_ref[:, cs] = l_h


def _attn_sample(h_s, cache, layer, g, dec_seq, precise):
    window, dil = ATT_GROUPS[g]
    depth, B, wb = cache.shape[0], cache.shape[1], cache.shape[2]
    assert wb == window and B % 2 == 0 and dec_seq == 4
    rows8 = 2 * dec_seq
    R = min(wb, BAND)
    stride = wb // R
    nsub = min(stride, dec_seq)
    cv = cache.reshape(depth, B, R, stride * 2 * HEADS, HEAD_DIM)
    c0 = 3 * g

    def new(c):
        return pl.BlockSpec((rows8, ATT_WIDTH), lambda i: (i, c0 + c))

    out_spec = pl.BlockSpec((rows8, ATT_WIDTH), lambda i: (i, 0))
    out_sds = jax.ShapeDtypeStruct((B * dec_seq, ATT_WIDTH), F32)
    return pl.pallas_call(
        functools.partial(_attn_sample_kernel, slopes=_alibi_slopes()[g], window=window, dil=dil,
                          dec_seq=dec_seq, nsub=nsub, precise=precise),
        grid=(B // 2,),
        in_specs=[new(0), new(1), new(2),
                  pl.BlockSpec((None, 2, R, nsub * 2 * HEADS, HEAD_DIM), lambda i: (layer, i, 0, 0, 0))],
        out_specs=[out_spec, out_spec],
        out_shape=[out_sds, out_sds],
        compiler_params=_cparams(1),
        name=f"attn_sample_g{g}",
    )(h_s, h_s, h_s, cv)


def _log_decay(ga, wa, ba):
    z = jnp.dot(ga, wa, preferred_element_type=F32, precision=lax.Precision.HIGHEST) + ba
    return (jnp.minimum(z, 0.0) - jnp.log(1.0 + jnp.exp(-jnp.abs(z)))) * (1.0 / GLA_TAU)


def _gla_prompt_kernel(q_ref, k_ref, v_ref, ga_ref, wa_ref, ba_ref, o_ref, sfin_ref, st_ref, la_ref,
                      qe_ref, upd_ref, ebl_ref):
    c = pl.program_id(1)
    C = GLA_CHUNK
    per_run = C // RUNS
    nsb = C // GLA_SUB
    sub_w = GLA_SUB // RUNS

    @pl.when(c == 0)
    def _():
        st_ref[...] = jnp.zeros_like(st_ref)

    la_ref[...] = _log_decay(ga_ref[...], wa_ref[...], ba_ref[...])

    r_i = lax.broadcasted_iota(jnp.int32, (C, C), 0)
    c_i = lax.broadcasted_iota(jnp.int32, (C, C), 1)
    tok_r = RUNS * (r_i % per_run) + r_i // per_run
    tok_c = RUNS * (c_i % per_run) + c_i // per_run
    tri = (tok_c <= tok_r).astype(F32)
    diag = (tok_r // GLA_SUB == tok_c // GLA_SUB) & (tok_c <= tok_r)
    rowblk = (lax.broadcasted_iota(jnp.int32, (C, GLA_HK), 0) % per_run) // sub_w
    last_run = (RUNS - 1) * per_run

    def gather(ref, j):
        return jnp.concatenate(
            [ref[r * RUN_LEN + j * per_run: r * RUN_LEN + (j + 1) * per_run, :] for r in range(RUNS)], axis=0)

    def scatter(ref, j, val, add):
        for r in range(RUNS):
            rows = slice(r * RUN_LEN + j * per_run, r * RUN_LEN + (j + 1) * per_run)
            piece = val[r * per_run:(r + 1) * per_run, :]
            ref[rows, :] = ref[rows, :] + piece if add else piece

    def local(js):
        def ref_rows(b, i):
            end_row = last_run + (i + 1) * sub_w - 1
            return b[end_row - sub_w:end_row - sub_w + 1, :], b[end_row:end_row + 1, :]

        b = [jnp.dot(tri, gather(la_ref, j), preferred_element_type=F32, precision=lax.Precision.HIGHEST) for j in js]
        q = [gather(q_ref, j) * (GLA_HK ** -0.5) for j in js]
        k = [gather(k_ref, j) for j in js]
        v = [gather(v_ref, j).astype(BF16) for j in js]
        qd, kd, ke, bstart = [], [], [], []
        for t, j in enumerate(js):
            qe_ref[j] = (q[t] * jnp.exp(b[t])).astype(BF16)
            bs = jnp.zeros_like(b[t])
            be = jnp.zeros_like(b[t])
            for i in range(nsb):
                prev_end, own_end = ref_rows(b[t], i)
                be = jnp.where(rowblk == i, own_end, be)
                if i > 0:
                    bs = jnp.where(rowblk == i, prev_end, bs)
            bstart.append(bs)
            qd.append(q[t] * jnp.exp(b[t] - bs))
            kd.append((k[t] * jnp.exp(bs - b[t])).astype(BF16))
            ke.append(k[t] * jnp.exp(be - b[t]))
        att = [jnp.where(diag, lax.dot_general(qd[t].astype(BF16), kd[t], _NT, preferred_element_type=F32), 0.0)
               for t in range(len(js))]
        for i in range(nsb - 1):
            qj, kej = [], []
            for t in range(len(js)):
                own_end = ref_rows(b[t], i)[1]
                dj = jnp.where(rowblk > i, jnp.exp(jnp.minimum(bstart[t] - own_end, 0.0)), 0.0)
                qj.append((qd[t] * dj).astype(BF16))
                kej.append(jnp.where(rowblk == i, ke[t], 0.0).astype(BF16))
            att = [att[t] + lax.dot_general(qj[t], kej[t], _NT, preferred_element_type=F32) for t in range(len(js))]
        intra = [jnp.dot(att[t].astype(BF16), v[t], preferred_element_type=F32) for t in range(len(js))]
        kdec = [(k[t] * jnp.exp(b[t][C - 1:C, :] - b[t])).astype(BF16) for t in range(len(js))]
        upd = [lax.dot_general(v[t], kdec[t], _TN, preferred_element_type=F32) for t in range(len(js))]
        for t, j in enumerate(js):
            scatter(o_ref, j, intra[t], add=False)
            upd_ref[j] = upd[t]
            ebl_ref[j] = jnp.broadcast_to(jnp.exp(b[t][C - 1:C, :]), (SUBLANES, GLA_HK))

    def recur(j):
        st = st_ref[...]
        scatter(o_ref, j, lax.dot_general(qe_ref[j], st.astype(BF16), _NT, preferred_element_type=F32), add=True)
        st_ref[...] = st * ebl_ref[j][0:1, :] + upd_ref[j]

    n_steps = CHUNK // C
    for j0 in range(0, n_steps, GLA_GROUP):
        local(list(range(j0, j0 + GLA_GROUP)))
    for j in range(n_steps):
        recur(j)

    @pl.when(c == pl.num_programs(1) - 1)
    def _():
        sfin_ref[...] = st_ref[...].T


def _gla_prompt(b, wa_pad, ba):
    T = seq = b.shape[0]
    assert seq % CHUNK == 0
    q0 = GLA_DV // GLA_HK
    k0 = q0 + GLA_DK // GLA_HK
    a0 = k0 + GLA_DK // GLA_HK
    og, sfin = pl.pallas_call(
        _gla_prompt_kernel,
        grid=(GLA_HEADS, seq // CHUNK),
        in_specs=[pl.BlockSpec((CHUNK, GLA_HK), lambda h, n: (n, q0 + h)),
                  pl.BlockSpec((CHUNK, GLA_HK), lambda h, n: (n, k0 + h)),
                  pl.BlockSpec((CHUNK, GLA_HV), lambda h, n: (n, h)),
                  pl.BlockSpec((CHUNK, LANE), lambda h, n: (n, a0)),
                  pl.BlockSpec((LANE, GLA_HK), lambda h, n: (0, h)),
                  pl.BlockSpec((1, GLA_HK), lambda h, n: (0, h))],
        out_specs=[pl.BlockSpec((CHUNK, GLA_HV), lambda h, n: (n, h)),
                   pl.BlockSpec((None, GLA_HK, GLA_HV), lambda h, n: (h, 0, 0))],
        out_shape=[jax.ShapeDtypeStruct((T, GLA_DV), F32),
                   jax.ShapeDtypeStruct((GLA_HEADS, GLA_HK, GLA_HV), F32)],
        scratch_shapes=[pltpu.VMEM((GLA_HV, GLA_HK), F32), pltpu.VMEM((CHUNK, GLA_HK), F32),
                        pltpu.VMEM((CHUNK // GLA_CHUNK, GLA_CHUNK, GLA_HK), BF16),
                        pltpu.VMEM((CHUNK // GLA_CHUNK, GLA_HV, GLA_HK), F32),
                        pltpu.VMEM((CHUNK // GLA_CHUNK, SUBLANES, GLA_HK), F32)],
        compiler_params=_cparams(2),
        name="gla_prompt",
    )(b, b, b, b, wa_pad, ba)
    return og, sfin


def _gla_sample_kernel(q_ref, k_ref, vlo_ref, vhi_ref, ga_ref, wa_ref, ba_ref, s0_ref, o_ref, snew_ref,
                       *, dec_seq, precise):
    rows8 = 2 * dec_seq
    r_i = lax.broadcasted_iota(jnp.int32, (rows8, rows8), 0)
    c_i = lax.broadcasted_iota(jnp.int32, (rows8, rows8), 1)
    same = (r_i // dec_seq == c_i // dec_seq) & (c_i <= r_i)
    tri = same.astype(F32)
    la = _log_decay(ga_ref[...], wa_ref[...], ba_ref[...])
    b_all = jnp.dot(tri, la, preferred_element_type=F32, precision=lax.Precision.HIGHEST)
    row_seq_k = lax.broadcasted_iota(jnp.int32, (rows8, GLA_HK), 0) // dec_seq
    row_seq_v = lax.broadcasted_iota(jnp.int32, (rows8, GLA_HV), 0) // dec_seq
    for h in range(GLA_HEADS):
        ks = slice(h * GLA_HK, (h + 1) * GLA_HK)
        vs = slice(h * GLA_HV, (h + 1) * GLA_HV)
        b = b_all[:, ks]
        q = q_ref[:, ks] * (GLA_HK ** -0.5)
        k = k_ref[:, ks]
        half = GLA_HEADS // 2
        v_src = vlo_ref if h < half else vhi_ref
        v = v_src[:, (h % half) * GLA_HV:(h % half + 1) * GLA_HV]
        qd = q * jnp.exp(b)
        kd = k * jnp.exp(-b)
        att = jnp.where(same, _dot(qd, kd, precise, _NT), 0.0)
        o = _dot(att, v, precise)
        for bb in range(2):
            st = s0_ref[bb, h].T
            o = o + jnp.where(row_seq_v == bb, _dot(qd, st, precise, _NT), 0.0)
            bl = b[(bb + 1) * dec_seq - 1:(bb + 1) * dec_seq, :]
            kdec = jnp.where(row_seq_k == bb, k * jnp.exp(bl - b), 0.0)
            snew_ref[bb, h] = (st * jnp.exp(bl) + _dot(v, kdec, precise, _TN)).T
        o_ref[:, vs] = o


def _gla_sample(h_s, wa_pad, ba, state, layer, dec_seq, precise):
    B = state.shape[1]
    rows8 = 2 * dec_seq
    q0 = N_GROUPS * 3 * ATT_WIDTH // GLA_DK
    a0 = (N_GROUPS * 3 * ATT_WIDTH + 2 * GLA_DK + 2 * GLA_DV) // LANE
    blk = lambda c: pl.BlockSpec((rows8, GLA_DK), lambda i: (i, c))
    return pl.pallas_call(
        functools.partial(_gla_sample_kernel, dec_seq=dec_seq, precise=precise),
        grid=(B // 2,),
        in_specs=[blk(q0), blk(q0 + 1), blk(q0 + 2), blk(q0 + 3),
                  pl.BlockSpec((rows8, LANE), lambda i: (i, a0)),
                  pl.BlockSpec((LANE, GLA_DK), lambda i: (0, 0)),
                  pl.BlockSpec((1, GLA_DK), lambda i: (0, 0)),
                  pl.BlockSpec((None, 2, GLA_HEADS, GLA_HK, GLA_HV), lambda i: (layer, i, 0, 0, 0))],
        out_specs=[pl.BlockSpec((rows8, GLA_DV), lambda i: (i, 0)),
                   pl.BlockSpec((2, GLA_HEADS, GLA_HK, GLA_HV), lambda i: (i, 0, 0, 0))],
        out_shape=[jax.ShapeDtypeStruct((B * dec_seq, GLA_DV), F32), jax.ShapeDtypeStruct(state.shape[1:], F32)],
        compiler_params=_cparams(1),
        name="gla_sample",
    )(h_s, h_s, h_s, h_s, h_s, wa_pad, ba, state)


def _post_kernel(o1, o2, o3, l1, l2, l3, og_ref, xin_ref, x_ref,
                 wgate_ref, wpa_ref, wpb_ref, wout_ref, gn_ref, lg_ref, lb_ref, xo_ref, xb_ref, *, alpha, precise):
    gates = _dot(xin_ref[...], wgate_ref[...], precise)
    D = x_ref.shape[1]
    gr, gate_a, gate_b = gates[:, :GLA_DV], gates[:, GLA_DV:GLA_DV + D], gates[:, GLA_DV + D:]
    m = jnp.maximum(jnp.maximum(l1[...], l2[...]), l3[...])
    e1 = jnp.exp(l1[...] - m)
    e2 = jnp.exp(l2[...] - m)
    e3 = jnp.exp(l3[...] - m)
    att = (e1 * o1[...] + e2 * o2[...] + e3 * o3[...]) / (e1 + e2 + e3)
    ya = _dot(att, wpa_ref[...], precise)
    parts = []
    for h in range(GLA_HEADS):
        vs = slice(h * GLA_HV, (h + 1) * GLA_HV)
        og = og_ref[:, vs]
        ms = jnp.mean(og * og, axis=-1, keepdims=True)
        parts.append(_operand(og * lax.rsqrt(ms + RMS_EPS) * gn_ref[:, vs] * _silu(gr[:, vs]), precise))
    y = jnp.concatenate(parts, axis=-1)
    yb = _dot(y, wpb_ref[...], precise)
    merged = _sigmoid(gate_a) * ya + _sigmoid(gate_b) * yb
    mix = _dot(merged, wout_ref[...], precise)
    xn = _layer_norm_rows(alpha * x_ref[...] + mix, lg_ref[...], lb_ref[...])
    xo_ref[...] = xn
    xb_ref[...] = xn.astype(BF16)


def _post(os_, ls_, og, xin, x, wgate, wpa, wpb, wout, gn, lg, lb, alpha, tm, precise=False):
    T, D = x.shape
    row = lambda w: pl.BlockSpec((tm, w), lambda i: (i, 0))
    full = lambda a: pl.BlockSpec(a.shape, lambda i: (0,) * a.ndim)
    resident = lambda a: pl.BlockSpec(a.shape, lambda i: (0,) * a.ndim, pipeline_mode=pl.Buffered(1))
    return pl.pallas_call(
        functools.partial(_post_kernel, alpha=alpha, precise=precise),
        grid=(pl.cdiv(T, tm),),
        in_specs=[row(ATT_WIDTH)] * 6 + [row(GLA_DV), row(D), row(D),
                                          resident(wgate), resident(wpa), resident(wpb), resident(wout),
                                          full(gn), full(lg), full(lb)],
        out_specs=[row(D), row(D)],
        out_shape=[jax.ShapeDtypeStruct((T, D), F32), jax.ShapeDtypeStruct((T, D), BF16)],
        compiler_params=_cparams(1),
        name="post_mixer_sample" if precise else "post_mixer",
    )(*os_, *ls_, og, xin, x, wgate, wpa, wpb, wout, gn, lg, lb)


def _ffn_kernel(xb_ref, x_ref, wg_ref, wu_ref, wd_ref, lg_ref, lb_ref, xo_ref, xbo_ref, acc_ref, *, alpha, precise):
    j = pl.program_id(1)
    xb = xb_ref[...]
    g = _dot(xb, wg_ref[...], precise)
    u = _dot(xb, wu_ref[...], precise)
    part = _dot(_silu(g) * u, wd_ref[...], precise)

    @pl.when(j == 0)
    def _():
        acc_ref[...] = part

    @pl.when(j > 0)
    def _():
        acc_ref[...] += part

    @pl.when(j == pl.num_programs(1) - 1)
    def _():
        xn = _layer_norm_rows(alpha * x_ref[...] + acc_ref[...], lg_ref[...], lb_ref[...])
        xo_ref[...] = xn
        xbo_ref[...] = xn.astype(BF16)


def _ffn(xb, x, wgu, wd, lg, lb, alpha, tm, tc, precise=False):
    T, D = x.shape
    dff = wd.shape[0]
    assert dff % tc == 0
    nj = dff // tc
    full = lambda a: pl.BlockSpec(a.shape, lambda i, j: (0,) * a.ndim)
    wmode = dict(pipeline_mode=pl.Buffered(1)) if nj == 1 else {}
    return pl.pallas_call(
        functools.partial(_ffn_kernel, alpha=alpha, precise=precise),
        grid=(pl.cdiv(T, tm), nj),
        in_specs=[pl.BlockSpec((tm, D), lambda i, j: (i, 0)),
                  pl.BlockSpec((tm, D), lambda i, j: (i, 0)),
                  pl.BlockSpec((D, tc), lambda i, j: (0, j), **wmode),
                  pl.BlockSpec((D, tc), lambda i, j: (0, nj + j), **wmode),
                  pl.BlockSpec((tc, D), lambda i, j: (j, 0), **wmode),
                  full(lg), full(lb)],
        out_specs=[pl.BlockSpec((tm, D), lambda i, j: (i, 0))] * 2,
        out_shape=[jax.ShapeDtypeStruct((T, D), F32), jax.ShapeDtypeStruct((T, D), BF16)],
        scratch_shapes=[pltpu.VMEM((tm, D), F32)],
        compiler_params=_cparams(2),
        name="ffn_dense_sample" if precise else "ffn_dense",
    )(xb, x, wgu, wgu, wd, lg, lb)


MOE_TOKENS = 512
MOE_PIECE = 16
MOE_ROWS = -(-(2 * MOE_TOKENS + N_EXPERTS * (MOE_PIECE - 1)) // MOE_PIECE) * MOE_PIECE
MOE_BLOCK = 1024
MOE_HALF = 512
MOE_SLOTS = MOE_BLOCK // MOE_PIECE


def _moe_route_kernel(x_ref, xb_ref, wrt_ref, xs_ref, meta_ref, cnt_ref, *, n_tokens):
    i = pl.program_id(0)
    ts = x_ref.shape[0]
    row_ok = (lax.broadcasted_iota(jnp.int32, (ts, 1), 0) + i * ts) < n_tokens
    x = jnp.where(row_ok, x_ref[...], 0.0)
    xb = jnp.where(row_ok, xb_ref[...], jnp.zeros((), BF16))
    lt = lax.dot_general(wrt_ref[...], x, _NT, preferred_element_type=F32, precision=lax.Precision.HIGHEST)
    sub = lax.broadcasted_iota(jnp.int32, (N_EXPERTS, ts), 0).astype(F32)
    tok_ok = (lax.broadcasted_iota(jnp.int32, (1, ts), 1) + i * ts) < n_tokens
    m1 = jnp.max(lt, axis=0, keepdims=True)
    i1 = jnp.min(jnp.where(lt == m1, sub, float(N_EXPERTS)), axis=0, keepdims=True)
    lt2 = jnp.where(sub == i1, NEG, lt)
    m2 = jnp.max(lt2, axis=0, keepdims=True)
    i2 = jnp.min(jnp.where(lt2 == m2, sub, float(N_EXPERTS)), axis=0, keepdims=True)
    e2 = jnp.exp(m2 - m1)
    g1 = 1.0 / (1.0 + e2)
    g2 = e2 / (1.0 + e2)
    oh1 = (sub == i1) & tok_ok
    oh2 = (sub == i2) & tok_ok
    assign = jnp.where(oh1 | oh2, 1.0, 0.0)
    before = (lax.broadcasted_iota(jnp.int32, (ts, ts), 0) < lax.broadcasted_iota(jnp.int32, (ts, ts), 1))
    rank = jnp.dot(assign.astype(BF16), jnp.where(before, 1.0, 0.0).astype(BF16), preferred_element_type=F32)
    cnt = jnp.sum(assign, axis=1, keepdims=True)
    padded = jnp.floor((cnt + (MOE_PIECE - 1)) * (1.0 / MOE_PIECE)) * MOE_PIECE
    lower = (lax.broadcasted_iota(jnp.int32, (N_EXPERTS, N_EXPERTS), 1)
             < lax.broadcasted_iota(jnp.int32, (N_EXPERTS, N_EXPERTS), 0)).astype(F32)
    seg_off = jnp.dot(lower, jnp.broadcast_to(padded, (N_EXPERTS, ts)), preferred_element_type=F32,
                      precision=lax.Precision.HIGHEST)
    dest = seg_off + rank
    d1 = jnp.where(tok_ok, jnp.sum(jnp.where(oh1, dest, 0.0), axis=0, keepdims=True), -1.0)
    d2 = jnp.where(tok_ok, jnp.sum(jnp.where(oh2, dest, 0.0), axis=0, keepdims=True), -1.0)
    p_i = lax.broadcasted_iota(jnp.int32, (MOE_ROWS, ts), 0).astype(F32)
    perm = jnp.where((p_i == d1) | (p_i == d2), 1.0, 0.0).astype(BF16)
    xs_ref[...] = jnp.dot(perm, xb, preferred_element_type=F32).astype(BF16)
    sub8 = lax.broadcasted_iota(jnp.int32, (SUBLANES, ts), 0)
    meta_ref[...] = jnp.where(sub8 == 0, d1, jnp.where(sub8 == 1, d2, jnp.where(sub8 == 2, g1,
                              jnp.where(sub8 == 3, g2, 0.0))))
    cnt_ref[...] = jnp.broadcast_to(cnt, (N_EXPERTS, LANE))


def _moe_route(x, xb, wr_t):
    T, D = x.shape
    nt = pl.cdiv(T, MOE_TOKENS)
    return pl.pallas_call(
        functools.partial(_moe_route_kernel, n_tokens=T),
        grid=(nt,),
        in_specs=[pl.BlockSpec((MOE_TOKENS, D), lambda i: (i, 0)),
                  pl.BlockSpec((MOE_TOKENS, D), lambda i: (i, 0)),
                  pl.BlockSpec(wr_t.shape, lambda i: (0, 0))],
        out_specs=[pl.BlockSpec((MOE_ROWS, D), lambda i: (i, 0)),
                   pl.BlockSpec((None, SUBLANES, MOE_TOKENS), lambda i: (i, 0, 0)),
                   pl.BlockSpec((None, N_EXPERTS, LANE), lambda i: (i, 0, 0))],
        out_shape=[jax.ShapeDtypeStruct((nt * MOE_ROWS, D), BF16),
                   jax.ShapeDtypeStruct((nt, SUBLANES, MOE_TOKENS), F32),
                   jax.ShapeDtypeStruct((nt, N_EXPERTS, LANE), F32)],
        compiler_params=_cparams(1),
        name="moe_route",
    )(x, xb, wr_t)


def _moe_plan(cnt, n_blocks):
    nt = cnt.shape[0]
    i32 = jnp.int32
    padded = (cnt + MOE_PIECE - 1) // MOE_PIECE * MOE_PIECE
    seg_row = jnp.cumsum(padded, axis=1) - padded + (jnp.arange(nt, dtype=i32) * MOE_ROWS)[:, None]
    n_seg = (padded // MOE_PIECE).T.reshape(-1)
    seg_row = seg_row.T.reshape(-1)
    seg_end = jnp.cumsum(n_seg)
    seg_start = seg_end - n_seg
    per_e = n_seg.reshape(N_EXPERTS, nt).sum(axis=1)
    e_end = jnp.cumsum(per_e)
    e_start = e_end - per_e
    blocks_e = (per_e + MOE_SLOTS - 1) // MOE_SLOTS
    blk_end = jnp.cumsum(blocks_e)
    blk_start = blk_end - blocks_e
    w = jnp.arange(n_blocks, dtype=i32)
    ew = jnp.minimum(jnp.sum((w[:, None] >= blk_end[None, :]).astype(i32), axis=1), N_EXPERTS - 1)
    is_e = ew[:, None] == jnp.arange(N_EXPERTS, dtype=i32)[None, :]
    pick = lambda v: jnp.sum(jnp.where(is_e, v[None, :], 0), axis=1)
    first = pick(e_start) + (w - pick(blk_start)) * MOE_SLOTS
    n_w = jnp.where(w < blk_end[-1], jnp.clip(pick(e_end) - first, 0, MOE_SLOTS), 0).astype(i32)
    slot = jnp.arange(MOE_SLOTS, dtype=i32)
    p = first[:, None] + slot[None, :]
    in_seg = (p[:, :, None] >= seg_start[None, None, :]) & (p[:, :, None] < seg_end[None, None, :])
    rows = jnp.sum(jnp.where(in_seg, (seg_row - MOE_PIECE * seg_start)[None, None, :], 0), axis=2) + MOE_PIECE * p
    rows = jnp.where(slot[None, :] < n_w[:, None], rows, 0).astype(i32)
    return ew.astype(i32), n_w, rows.reshape(-1)


def _moe_expert_kernel(ew_ref, n_ref, rows_ref, xs_hbm, wg_ref, wu_ref, wd_ref, ys_in, ys_hbm,
                       lhs_ref, acc_ref, out_ref, sem_in, sem_out):
    del ew_ref, ys_in
    w = pl.program_id(0)
    j = pl.program_id(1)
    n_w = pl.num_programs(0)
    last_j = pl.num_programs(1) - 1
    n = n_ref[w]
    slot = w % 2
    halves = [(s, slice(s * MOE_HALF, (s + 1) * MOE_HALF)) for s in range(MOE_BLOCK // MOE_HALF)]

    def piece_in(blk, q, sl):
        src = pl.multiple_of(rows_ref[blk * MOE_SLOTS + q], MOE_PIECE)
        dst = pl.multiple_of(q * MOE_PIECE, MOE_PIECE)
        return pltpu.make_async_copy(xs_hbm.at[pl.ds(src, MOE_PIECE), :],
                                     lhs_ref.at[sl, pl.ds(dst, MOE_PIECE), :], sem_in.at[sl])

    def piece_out(blk, q):
        dst = pl.multiple_of(rows_ref[blk * MOE_SLOTS + q], MOE_PIECE)
        src = pl.multiple_of(q * MOE_PIECE, MOE_PIECE)
        return pltpu.make_async_copy(out_ref.at[pl.ds(src, MOE_PIECE), :], ys_hbm.at[pl.ds(dst, MOE_PIECE), :], sem_out)

    def for_pieces(blk, fn):
        lax.fori_loop(0, n_ref[blk], lambda q, c: (fn(q), c)[1], 0)

    @pl.when(j == 0)
    def _():
        @pl.when(w == 0)
        def _():
            lhs_ref[...] = jnp.zeros_like(lhs_ref)
            for_pieces(0, lambda q: piece_in(0, q, 0).start())

        for_pieces(w, lambda q: piece_in(w, q, slot).wait())

        @pl.when(w + 1 < n_w)
        def _():
            for_pieces(w + 1, lambda q: piece_in(w + 1, q, 1 - slot).start())

    for s, rows in halves:
        @pl.when(n > s * (MOE_HALF // MOE_PIECE))
        def _(rows=rows):
            xb = lhs_ref[slot, rows, :]
            g = jnp.dot(xb, wg_ref[...], preferred_element_type=F32)
            u = jnp.dot(xb, wu_ref[...], preferred_element_type=F32)
            part = jnp.dot((_silu(g) * u).astype(BF16), wd_ref[...], preferred_element_type=F32)

            @pl.when(j == 0)
            def _():
                acc_ref[rows, :] = part

            @pl.when(j > 0)
            def _():
                acc_ref[rows, :] += part

    @pl.when(j == last_j)
    def _():
        @pl.when(w > 0)
        def _():
            for_pieces(w - 1, lambda q: piece_out(w - 1, q).wait())

        for s, rows in halves:
            @pl.when(n > s * (MOE_HALF // MOE_PIECE))
            def _(rows=rows):
                out_ref[rows, :] = acc_ref[rows, :].astype(BF16)

        for_pieces(w, lambda q: piece_out(w, q).start())

        @pl.when(w == n_w - 1)
        def _():
            for_pieces(w, lambda q: piece_out(w, q).wait())


def _moe_experts(xs, plan, wgu, wd, tc):
    ew, n_w, rows = plan
    n_blocks = ew.shape[0]
    D = xs.shape[1]
    dff = wd.shape[1]
    assert dff % tc == 0
    nj = dff // tc
    any_spec = pl.BlockSpec(memory_space=pl.ANY)
    return pl.pallas_call(
        _moe_expert_kernel,
        grid_spec=pltpu.PrefetchScalarGridSpec(
            num_scalar_prefetch=3,
            grid=(n_blocks, nj),
            in_specs=[any_spec,
                      pl.BlockSpec((None, D, tc), lambda w, j, ew, n, r: (ew[w], 0, j)),
                      pl.BlockSpec((None, D, tc), lambda w, j, ew, n, r: (ew[w], 0, nj + j)),
                      pl.BlockSpec((None, tc, D), lambda w, j, ew, n, r: (ew[w], j, 0)),
                      any_spec],
            out_specs=any_spec,
            scratch_shapes=[pltpu.VMEM((2, MOE_BLOCK, D), BF16), pltpu.VMEM((MOE_BLOCK, D), F32),
                            pltpu.VMEM((MOE_BLOCK, D), BF16),
                            pltpu.SemaphoreType.DMA((2,)), pltpu.SemaphoreType.DMA(())]),
        out_shape=jax.ShapeDtypeStruct(xs.shape, BF16),
        input_output_aliases={7: 0},
        compiler_params=_cparams(2),
        name="moe_experts",
    )(ew, n_w, rows, xs, wgu, wgu, wd, jnp.zeros(xs.shape, BF16))


def _moe_combine_kernel(ys_ref, meta_ref, x_ref, lg_ref, lb_ref, out_a, out_b, *, alpha, prompt_tiles):
    ts = x_ref.shape[0]
    meta = meta_ref[...]
    d1, d2, g1, g2 = meta[0:1, :], meta[1:2, :], meta[2:3, :], meta[3:4, :]
    p_i = lax.broadcasted_iota(jnp.int32, (MOE_ROWS, ts), 0).astype(F32)
    back = (jnp.where(p_i == d1, g1, 0.0) + jnp.where(p_i == d2, g2, 0.0)).astype(BF16)
    f = lax.dot_general(back, ys_ref[...], _TN, preferred_element_type=F32)
    xn = _layer_norm_rows(alpha * x_ref[...] + f, lg_ref[...], lb_ref[...])
    if prompt_tiles is None:
        out_a[...] = xn
        out_b[...] = xn.astype(BF16)
    else:
        i = pl.program_id(0)

        @pl.when(i < prompt_tiles)
        def _():
            out_a[...] = xn

        @pl.when(i == prompt_tiles)
        def _():
            out_b[...] = xn[:out_b.shape[0], :]


def _moe_combine(ys, meta, x, lg, lb, alpha, split_at):
    T, D = x.shape
    nt = meta.shape[0]
    full = lambda a: pl.BlockSpec(a.shape, lambda i: (0,) * a.ndim)
    if split_at is None:
        prompt_tiles = None
        out_specs = [pl.BlockSpec((MOE_TOKENS, D), lambda i: (i, 0))] * 2
        out_shape = [jax.ShapeDtypeStruct((T, D), F32), jax.ShapeDtypeStruct((T, D), BF16)]
    else:
        prompt_tiles, ns = split_at // MOE_TOKENS, T - split_at
        assert split_at % MOE_TOKENS == 0 and 0 < ns <= MOE_TOKENS and nt == prompt_tiles + 1
        out_specs = [pl.BlockSpec((MOE_TOKENS, D), lambda i: (jnp.minimum(i, prompt_tiles - 1), 0)),
                     pl.BlockSpec((ns, D), lambda i: (0, 0))]
        out_shape = [jax.ShapeDtypeStruct((split_at, D), F32), jax.ShapeDtypeStruct((ns, D), F32)]
    return pl.pallas_call(
        functools.partial(_moe_combine_kernel, alpha=alpha, prompt_tiles=prompt_tiles),
        grid=(nt,),
        in_specs=[pl.BlockSpec((MOE_ROWS, D), lambda i: (i, 0)),
                  pl.BlockSpec((None, SUBLANES, MOE_TOKENS), lambda i: (i, 0, 0)),
                  pl.BlockSpec((MOE_TOKENS, D), lambda i: (i, 0)),
                  full(lg), full(lb)],
        out_specs=out_specs,
        out_shape=out_shape,
        compiler_params=_cparams(1),
        name="moe_combine",
    )(ys, meta, x, lg, lb)


def _moe(x, xb, w_router, wgu, wd, lg, lb, alpha, split_at=None):
    T = x.shape[0]
    nt = pl.cdiv(T, MOE_TOKENS)
    xs, meta, cnt = _moe_route(x, xb, w_router.T)
    n_pieces = (2 * T + nt * N_EXPERTS * (MOE_PIECE - 1)) // MOE_PIECE + 1
    n_blocks = pl.cdiv(n_pieces, MOE_SLOTS) + N_EXPERTS
    plan = _moe_plan(cnt[:, :, 0].astype(jnp.int32), n_blocks)
    ys = _moe_experts(xs, plan, wgu, wd, MOE_CHUNK)
    return _moe_combine(ys, meta, x, lg, lb, alpha, split_at)


def _split_w_in(w):
    a_end = N_GROUPS * 3 * ATT_WIDTH
    gq0, gk0, gv0 = a_end, a_end + GLA_DK, a_end + 2 * GLA_DK
    gr0 = gv0 + GLA_DV
    ga0 = gr0 + GLA_DV
    gta0 = ga0 + GLA_RANK
    wa = w[:, :a_end].astype(BF16)
    used = GLA_DV + 2 * GLA_DK + GLA_RANK
    ga_pad = jnp.pad(w[:, ga0:gta0], ((0, 0), (0, GLA_PANEL - used)))
    wb = jnp.concatenate([w[:, gv0:gr0], w[:, gq0:gk0], w[:, gk0:gv0], ga_pad], axis=1).astype(BF16)
    wg = jnp.concatenate([w[:, gr0:ga0], w[:, gta0:]], axis=1).astype(BF16)
    return wa, wb, wg


def _kv_rows_kernel(kp_ref, vp_ref, ks_ref, vs_ref, op_ref, os_ref, *, keep):
    n = keep // RUNS
    ns = ks_ref.shape[0]
    for kv, (p_ref, s_ref) in enumerate(((kp_ref, ks_ref), (vp_ref, vs_ref))):
        for h in range(HEADS):
            cs = slice(h * HEAD_DIM, (h + 1) * HEAD_DIM)
            sub = kv * HEADS + h
            for r in range(RUNS):
                op_ref[pl.ds(2 * HEADS * r + sub, n, stride=2 * HEADS * RUNS), :] = \
                    p_ref[(r + 1) * RUN_LEN - n:(r + 1) * RUN_LEN, cs]
            os_ref[pl.ds(sub, ns, stride=2 * HEADS), :] = s_ref[:, cs]


def _new_kv_rows(a_mat, h_s, g, db, dec_seq):
    seq = a_mat.shape[0]
    keep = min(ATT_GROUPS[g][0], seq)
    ns = db * dec_seq
    assert keep <= CHUNK and keep % RUNS == 0 and seq % CHUNK == 0 and h_s.shape[0] == ns
    last = seq // CHUNK - 1
    spec = lambda rows, rb, c: pl.BlockSpec((rows, ATT_WIDTH), lambda i: (rb, 3 * g + c))
    kv_p, kv_s = pl.pallas_call(
        functools.partial(_kv_rows_kernel, keep=keep),
        grid=(1,),
        in_specs=[spec(CHUNK, last, 1), spec(CHUNK, last, 2), spec(ns, 0, 1), spec(ns, 0, 2)],
        out_specs=[pl.BlockSpec((keep * 2 * HEADS, HEAD_DIM), lambda i: (0, 0)),
                   pl.BlockSpec((ns * 2 * HEADS, HEAD_DIM), lambda i: (0, 0))],
        out_shape=[jax.ShapeDtypeStruct((keep * 2 * HEADS, HEAD_DIM), F32),
                   jax.ShapeDtypeStruct((ns * 2 * HEADS, HEAD_DIM), F32)],
        compiler_params=_cparams(1),
        name=f"kv_rows_g{g}",
    )(a_mat, a_mat, h_s, h_s)
    return kv_p.reshape(1, keep, 2, HEADS, HEAD_DIM), kv_s.reshape(db, dec_seq, 2, HEADS, HEAD_DIM)


def kernel(x_prompt, x_sample, cache_kv_w128, cache_kv_w512, cache_kv_w2048, state_gla,
           w_in, w_alpha2, b_alpha, gla_norm_g, w_pa, w_pb, w_out,
           ln_mix_g, ln_mix_b, ln_ffn_g, ln_ffn_b,
           w_ffn_gu, w_ffn_down, w_router, w_exp_gu, w_exp_down):
    depth = w_in.shape[0]
    bp, seq, D = x_prompt.shape
    db, dec_seq, _ = x_sample.shape
    assert bp == 1
    ns = db * dec_seq
    alpha = (2 * depth) ** 0.25
    caches = (cache_kv_w128, cache_kv_w512, cache_kv_w2048)

    x = _interleave_rows(x_prompt.reshape(seq, D))
    xb = x.astype(BF16)
    xs = x_sample.reshape(ns, D)
    kv_p = [[] for _ in range(N_GROUPS)]
    kv_s = [[] for _ in range(N_GROUPS)]
    st_p, st_s = [], []
    row2 = lambda v: v.reshape(1, -1)
    gate0 = N_GROUPS * 3 * ATT_WIDTH + 2 * GLA_DK + GLA_DV
    gate1 = gate0 + GLA_DV

    for l in range(depth):
        wa, wb, wg = _split_w_in(w_in[l])
        a_mat = _mm(xb, wa, PROJ_ROWS, PROJ_COLS_ATT)
        b_mat = _mm(xb, wb, PROJ_ROWS, GLA_PANEL)
        h_s = _mm(xs, w_in, ns, SAMPLE_COLS, precise=True, layer=l)

        os_, ls_, os_s, ls_s = [], [], [], []
        for g in range(N_GROUPS):
            o, lse = _attn_prompt(a_mat, g)
            os_.append(o)
            ls_.append(lse)
            o, lse = _attn_sample(h_s, caches[g], l, g, dec_seq, precise=True)
            os_s.append(o)
            ls_s.append(lse)
            p_rows, s_rows = _new_kv_rows(a_mat, h_s, g, db, dec_seq)
            kv_p[g].append(p_rows)
            kv_s[g].append(s_rows)

        wa_pad = jnp.pad(w_alpha2[l], ((0, LANE - GLA_RANK), (0, 0)))
        og, s_p = _gla_prompt(b_mat, wa_pad, row2(b_alpha[l]))
        og_s, s_s = _gla_sample(h_s, wa_pad, row2(b_alpha[l]), state_gla, l, dec_seq, precise=True)
        st_p.append(s_p[None])
        st_s.append(s_s)

        norms = (row2(gla_norm_g[l]), row2(ln_mix_g[l]), row2(ln_mix_b[l]))
        x, xb = _post(os_, ls_, og, xb, x, wg, w_pa[l].astype(BF16), w_pb[l].astype(BF16), w_out[l].astype(BF16),
                      *norms, alpha, POST_ROWS)
        wg_f32 = jnp.concatenate([w_in[l][:, gate0:gate1], w_in[l][:, gate1 + GLA_RANK:]], axis=1)
        xs, _ = _post(os_s, ls_s, og_s, xs, xs, wg_f32, w_pa[l], w_pb[l], w_out[l], *norms, alpha, ns, precise=True)

        ffn_norm = (row2(ln_ffn_g[l]), row2(ln_ffn_b[l]), alpha)
        if l % 2 == 0:
            x, xb = _ffn(xb, x, w_ffn_gu[l // 2].astype(BF16), w_ffn_down[l // 2].astype(BF16),
                         *ffn_norm, FFN_ROWS, FFN_CHUNK)
            xs, _ = _ffn(xs, xs, w_ffn_gu[l // 2], w_ffn_down[l // 2], *ffn_norm, ns, SAMPLE_FFN_CHUNK, precise=True)
        else:
            moe_w = (w_router[l // 2], w_exp_gu[l // 2].astype(BF16), w_exp_down[l // 2].astype(BF16), *ffn_norm)
            x_all = jnp.concatenate([x, xs], axis=0)
            xb_all = jnp.concatenate([xb, xs.astype(BF16)], axis=0)
            if l == depth - 1:
                y_prompt, y_sample = _moe(x_all, xb_all, *moe_w, split_at=seq)
            else:
                x_all, xb_all = _moe(x_all, xb_all, *moe_w)
                x, xb, xs = x_all[:seq], xb_all[:seq], x_all[seq:]
    if depth % 2 == 1:
        y_prompt, y_sample = x, xs

    return (_deinterleave_rows(y_prompt).reshape(1, seq, D), y_sample.reshape(db, dec_seq, D),
            jnp.stack(kv_p[0], 0), jnp.stack(kv_s[0], 0),
            jnp.stack(kv_p[1], 0), jnp.stack(kv_s[1], 0),
            jnp.stack(kv_p[2], 0), jnp.stack(kv_s[2], 0),
            jnp.stack(st_p, 0), jnp.stack(st_s, 0))
```

```python
import functools

import jax
import jax.numpy as jnp
from jax import lax
from jax.experimental import pallas as pl
from jax.experimental.pallas import tpu as pltpu

F32 = jnp.float32
BF16 = jnp.bfloat16

ATT_GROUPS = ((128, 1), (512, 4), (2048, 16))
N_GROUPS = 3
HEADS = 4
HEAD_DIM = 128
ATT_WIDTH = HEADS * HEAD_DIM
BAND = 128
ATT_GROUP = 8
GLA_HEADS = 4
GLA_HK = 128
GLA_HV = 256
GLA_DK = GLA_HEADS * GLA_HK
GLA_DV = GLA_HEADS * GLA_HV
GLA_RANK = 16
GLA_TAU = 16.0
GLA_CHUNK = 128
GLA_SUB = 32
GLA_GROUP = 8
N_EXPERTS = 8
LN_EPS = 1e-5
RMS_EPS = 1e-6
NEG = -1e30
LOG2E = 1.4426950408889634
LN2 = 0.6931471805599453
LANE = 128
SUBLANES = 8
RUNS = 16
RUN_LEN = 128
CHUNK = RUNS * RUN_LEN
VMEM_LIMIT = 56 * 1024 * 1024

PROJ_ROWS = 1024
PROJ_COLS_ATT = 1536
GLA_PANEL = 2304
POST_ROWS = 256
FFN_ROWS = 512
FFN_CHUNK = 2816
MOE_CHUNK = 1792
SAMPLE_FFN_CHUNK = 256

_NT = (((1,), (1,)), ((), ()))
_TN = (((0,), (0,)), ((), ()))


def _alibi_slopes():
    n = N_GROUPS * HEADS
    return [[2.0 ** (-8.0 * (g * HEADS + h + 1) / n) for h in range(HEADS)] for g in range(N_GROUPS)]


def _cparams(n_axes):
    return pltpu.CompilerParams(dimension_semantics=("arbitrary",) * n_axes, vmem_limit_bytes=VMEM_LIMIT)


def _layer_norm_rows(r, g, b):
    mu = jnp.mean(r, axis=-1, keepdims=True)
    c = r - mu
    var = jnp.mean(c * c, axis=-1, keepdims=True)
    return c * lax.rsqrt(var + LN_EPS) * g + b


def _sigmoid(x):
    return 1.0 / (1.0 + jnp.exp(-x))


def _silu(x):
    return x * _sigmoid(x)


def _operand(x, precise):
    return x.astype(F32 if precise else BF16)


def _dot(a, b, precise, dims=None):
    def mm(u, v):
        if dims is None:
            return jnp.dot(u, v, preferred_element_type=F32)
        return lax.dot_general(u, v, dims, preferred_element_type=F32)

    if not precise:
        return mm(a.astype(BF16), b.astype(BF16))
    a = a.astype(F32)
    a_hi = a.astype(BF16)
    a_lo = (a - a_hi.astype(F32)).astype(BF16)
    if isinstance(b, tuple):
        b_hi, b_lo = b
    else:
        b = b.astype(F32)
        b_hi = b.astype(BF16)
        b_lo = (b - b_hi.astype(F32)).astype(BF16)
    return mm(a_hi, b_hi) + (mm(a_hi, b_lo) + mm(a_lo, b_hi))


def _interleave_rows(x):
    s, d = x.shape
    return x.reshape(s // CHUNK, RUN_LEN, RUNS, d).transpose(0, 2, 1, 3).reshape(s, d)


def _deinterleave_rows(x):
    s, d = x.shape
    return x.reshape(s // CHUNK, RUNS, RUN_LEN, d).transpose(0, 2, 1, 3).reshape(s, d)


def _mm_kernel(x_ref, w_ref, o_ref):
    o_ref[...] = _dot(x_ref[...], w_ref[...], False)


def _mm_precise_kernel(x_ref, wh_ref, wl_ref, o_ref):
    o_ref[...] = _dot(x_ref[...], (wh_ref[...], wl_ref[...]), True)


def _mm(x, w, tm, tn):
    precise = isinstance(w, tuple)
    ws = w if precise else (w,)
    M, K = x.shape
    N = ws[0].shape[1]
    assert N % tn == 0
    w_spec = pl.BlockSpec((K, tn), lambda j, i: (0, j))
    return pl.pallas_call(
        _mm_precise_kernel if precise else _mm_kernel,
        grid=(N // tn, pl.cdiv(M, tm)),
        in_specs=[pl.BlockSpec((tm, K), lambda j, i: (i, 0))] + [w_spec] * len(ws),
        out_specs=pl.BlockSpec((tm, tn), lambda j, i: (i, j)),
        out_shape=jax.ShapeDtypeStruct((M, N), F32),
        compiler_params=_cparams(2),
        name="in_proj_sample" if precise else "in_proj",
    )(x, *ws)


def _attn_prompt_kernel(slope_ref, q_ref, k_ref, v_ref, o_ref, l_ref, kp_ref, vp_ref, *, dil):
    h = pl.program_id(0)
    c = pl.program_id(1)
    G = RUNS // dil
    wq = BAND // G
    scale = HEAD_DIM ** -0.5

    @pl.when(c == 0)
    def _():
        kp_ref[...] = jnp.zeros_like(kp_ref)
        vp_ref[...] = jnp.zeros_like(vp_ref)

    qi = lax.broadcasted_iota(jnp.int32, (BAND, 2 * BAND), 0)
    ci = lax.broadcasted_iota(jnp.int32, (BAND, 2 * BAND), 1)
    e = ci % (2 * wq)
    rel = (G * (qi % wq) + qi // wq) - (G * (e - wq) + ci // (2 * wq))
    valid = (rel >= 0) & (rel <= BAND)
    bias = rel.astype(F32) * (-slope_ref[h] * (dil * LOG2E))
    bias_in = jnp.where(valid, bias, NEG)
    first_e = jnp.where(c > 0, 0, wq)
    bias_first = jnp.where(valid & (e >= first_e), bias, NEG)

    def keys(cur_ref, prev_ref, runs, m):
        if m == 0:
            parts = []
            for rho in runs:
                parts += [prev_ref[(rho + 1) * RUN_LEN - wq:(rho + 1) * RUN_LEN, :],
                          cur_ref[rho * RUN_LEN: rho * RUN_LEN + wq, :]]
        else:
            parts = [cur_ref[rho * RUN_LEN + wq * (m - 1): rho * RUN_LEN + wq * (m + 1), :] for rho in runs]
        return jnp.concatenate(parts, axis=0).astype(BF16)

    blocks = [([r + dil * v for v in range(G)], m) for r in range(dil) for m in range(G)]
    for g0 in range(0, len(blocks), ATT_GROUP):
        grp = blocks[g0:g0 + ATT_GROUP]
        q = [(jnp.concatenate([q_ref[rho * RUN_LEN + wq * m: rho * RUN_LEN + wq * (m + 1), :] for rho in runs],
                              axis=0) * (scale * LOG2E)).astype(BF16) for runs, m in grp]
        s = [lax.dot_general(q[t], keys(k_ref, kp_ref, runs, m), _NT, preferred_element_type=F32)
             + (bias_first if m == 0 else bias_in) for t, (runs, m) in enumerate(grp)]
        mx = [jnp.max(x, axis=-1, keepdims=True) for x in s]
        p = [jnp.exp2(s[t] - mx[t]) for t in range(len(grp))]
        l = [jnp.sum(x, axis=-1, keepdims=True) for x in p]
        o = [jnp.dot(p[t].astype(BF16), keys(v_ref, vp_ref, runs, m), preferred_element_type=F32) / l[t]
             for t, (runs, m) in enumerate(grp)]
        for t, (runs, m) in enumerate(grp):
            lse = jnp.broadcast_to(mx[t] * LN2 + jnp.log(l[t]), (BAND, HEAD_DIM))
            for idx, rho in enumerate(runs):
                rows = slice(rho * RUN_LEN + wq * m, rho * RUN_LEN + wq * (m + 1))
                o_ref[rows, :] = o[t][idx * wq:(idx + 1) * wq, :]
                l_ref[rows, :] = lse[idx * wq:(idx + 1) * wq, :]

    kp_ref[...] = k_ref[...]
    vp_ref[...] = v_ref[...]


def _attn_prompt(a_mat, g):
    T = seq = a_mat.shape[0]
    _, dil = ATT_GROUPS[g]
    assert seq % CHUNK == 0 and RUNS % dil == 0
    slopes = jnp.asarray(_alibi_slopes()[g], F32)

    def col(c):
        return pl.BlockSpec((CHUNK, HEAD_DIM), lambda h, n: (n, (3 * g + c) * HEADS + h))

    out_spec = pl.BlockSpec((CHUNK, HEAD_DIM), lambda h, n: (n, h))
    out_sds = jax.ShapeDtypeStruct((T, ATT_WIDTH), F32)
    return pl.pallas_call(
        functools.partial(_attn_prompt_kernel, dil=dil),
        grid=(HEADS, seq // CHUNK),
        in_specs=[pl.BlockSpec(memory_space=pltpu.SMEM), col(0), col(1), col(2)],
        out_specs=[out_spec, out_spec],
        out_shape=[out_sds, out_sds],
        scratch_shapes=[pltpu.VMEM((CHUNK, HEAD_DIM), F32), pltpu.VMEM((CHUNK, HEAD_DIM), F32)],
        compiler_params=_cparams(2),
        name=f"attn_prompt_g{g}",
    )(slopes, a_mat, a_mat, a_mat)


def _attn_sample_kernel(q_ref, k_ref, v_ref, c_ref, o_ref, l_ref, *, slopes, window, dil, dec_seq, nsub, precise):
    R = c_ref.shape[1]
    wb = window
    stride = wb // R
    rows8 = 2 * dec_seq
    scale = HEAD_DIM ** -0.5
    ncache = R * nsub

    ri = lax.broadcasted_iota(jnp.int32, (rows8, ncache), 0)
    ji = lax.broadcasted_iota(jnp.int32, (rows8, ncache), 1)
    t_row = ri % dec_seq
    c_true = (ji % R) * stride + ji // R
    dist_c = wb + t_row - c_true
    ok_c = ((dist_c & (dil - 1)) == 0) & (dist_c <= window)
    bias_c_base = jnp.where(ok_c, 0.0, NEG)
    dist_c_f = dist_c.astype(F32)

    rn = lax.broadcasted_iota(jnp.int32, (rows8, rows8), 0)
    jn = lax.broadcasted_iota(jnp.int32, (rows8, rows8), 1)
    dist_n = rn % dec_seq - jn % dec_seq
    ok_n = (rn // dec_seq == jn // dec_seq) & (dist_n >= 0) & ((dist_n & (dil - 1)) == 0)
    bias_n_base = jnp.where(ok_n, 0.0, NEG)
    dist_n_f = dist_n.astype(F32)
    row_seq = lax.broadcasted_iota(jnp.int32, (rows8, HEAD_DIM), 0) // dec_seq

    for h in range(HEADS):
        cs = slice(h * HEAD_DIM, (h + 1) * HEAD_DIM)
        q = q_ref[:, cs]
        kn = k_ref[:, cs]
        vn = v_ref[:, cs]
        s_n = _dot(q, kn, precise, _NT) * scale - dist_n_f * slopes[h] + bias_n_base
        o_h = jnp.zeros((rows8, HEAD_DIM), F32)
        l_h = jnp.zeros((rows8, HEAD_DIM), F32)
        for bb in range(2):
            kc = jnp.concatenate([c_ref[bb, :, u * 2 * HEADS + h, :] for u in range(nsub)], axis=0)
            vc = jnp.concatenate([c_ref[bb, :, u * 2 * HEADS + HEADS + h, :] for u in range(nsub)], axis=0)
            s_c = _dot(q, kc, precise, _NT) * scale - dist_c_f * slopes[h] + bias_c_base
            m = jnp.maximum(jnp.max(s_c, axis=-1, keepdims=True), jnp.max(s_n, axis=-1, keepdims=True))
            p_c = jnp.exp(s_c - m)
            p_n = jnp.exp(s_n - m)
            l = jnp.sum(p_c, axis=-1, keepdims=True) + jnp.sum(p_n, axis=-1, keepdims=True)
            o = (_dot(p_c, vc, precise) + _dot(p_n, vn, precise)) / l
            mine = row_seq == bb
            o_h = jnp.where(mine, o, o_h)
            l_h = jnp.where(mine, jnp.broadcast_to(m + jnp.log(l), (rows8, HEAD_DIM)), l_h)
        o_ref[:, cs] = o_h
        l_ref[:, cs] = l_h


def _attn_sample(a_s, cache, layer, g, dec_seq, precise):
    window, dil = ATT_GROUPS[g]
    depth, B, wb = cache.shape[0], cache.shape[1], cache.shape[2]
    assert wb == window and B % 2 == 0 and dec_seq == 4
    rows8 = 2 * dec_seq
    R = min(wb, BAND)
    stride = wb // R
    nsub = min(stride, dec_seq)
    cv = cache.reshape(depth, B, R, stride * 2 * HEADS, HEAD_DIM)
    c0 = 3 * g

    def new(c):
        return pl.BlockSpec((rows8, ATT_WIDTH), lambda i: (i, c0 + c))

    out_spec = pl.BlockSpec((rows8, ATT_WIDTH), lambda i: (i, 0))
    out_sds = jax.ShapeDtypeStruct((B * dec_seq, ATT_WIDTH), F32)
    return pl.pallas_call(
        functools.partial(_attn_sample_kernel, slopes=_alibi_slopes()[g], window=window, dil=dil,
                          dec_seq=dec_seq, nsub=nsub, precise=precise),
        grid=(B // 2,),
        in_specs=[new(0), new(1), new(2),
                  pl.BlockSpec((None, 2, R, nsub * 2 * HEADS, HEAD_DIM), lambda i: (layer, i, 0, 0, 0))],
        out_specs=[out_spec, out_spec],
        out_shape=[out_sds, out_sds],
        compiler_params=_cparams(1),
        name=f"attn_sample_g{g}",
    )(a_s, a_s, a_s, cv)


def _log_decay(ga, wa, ba):
    z = jnp.dot(ga, wa, preferred_element_type=F32, precision=lax.Precision.HIGHEST) + ba
    return (jnp.minimum(z, 0.0) - jnp.log(1.0 + jnp.exp(-jnp.abs(z)))) * (1.0 / GLA_TAU)


def _gla_prompt_kernel(q_ref, k_ref, v_ref, ga_ref, wa_ref, ba_ref, o_ref, sfin_ref, st_ref, la_ref,
                      qe_ref, upd_ref, ebl_ref):
    c = pl.program_id(1)
    C = GLA_CHUNK
    per_run = C // RUNS
    nsb = C // GLA_SUB
    sub_w = GLA_SUB // RUNS

    @pl.when(c == 0)
    def _():
        st_ref[...] = jnp.zeros_like(st_ref)

    la_ref[...] = _log_decay(ga_ref[...], wa_ref[...], ba_ref[...])

    r_i = lax.broadcasted_iota(jnp.int32, (C, C), 0)
    c_i = lax.broadcasted_iota(jnp.int32, (C, C), 1)
    tok_r = RUNS * (r_i % per_run) + r_i // per_run
    tok_c = RUNS * (c_i % per_run) + c_i // per_run
    tri = (tok_c <= tok_r).astype(F32)
    diag = (tok_r // GLA_SUB == tok_c // GLA_SUB) & (tok_c <= tok_r)
    rowblk = (lax.broadcasted_iota(jnp.int32, (C, GLA_HK), 0) % per_run) // sub_w
    last_run = (RUNS - 1) * per_run

    def gather(ref, j):
        return jnp.concatenate(
            [ref[r * RUN_LEN + j * per_run: r * RUN_LEN + (j + 1) * per_run, :] for r in range(RUNS)], axis=0)

    def scatter(ref, j, val, add):
        for r in range(RUNS):
            rows = slice(r * RUN_LEN + j * per_run, r * RUN_LEN + (j + 1) * per_run)
            piece = val[r * per_run:(r + 1) * per_run, :]
            ref[rows, :] = ref[rows, :] + piece if add else piece

    def local(js):
        def ref_rows(b, i):
            end_row = last_run + (i + 1) * sub_w - 1
            return b[end_row - sub_w:end_row - sub_w + 1, :], b[end_row:end_row + 1, :]

        b = [jnp.dot(tri, gather(la_ref, j), preferred_element_type=F32, precision=lax.Precision.HIGHEST) for j in js]
        q = [gather(q_ref, j) * (GLA_HK ** -0.5) for j in js]
        k = [gather(k_ref, j) for j in js]
        v = [gather(v_ref, j).astype(BF16) for j in js]
        qd, kd, ke, bstart = [], [], [], []
        for t, j in enumerate(js):
            qe_ref[j] = (q[t] * jnp.exp(b[t])).astype(BF16)
            bs = jnp.zeros_like(b[t])
            be = jnp.zeros_like(b[t])
            for i in range(nsb):
                prev_end, own_end = ref_rows(b[t], i)
                be = jnp.where(rowblk == i, own_end, be)
                if i > 0:
                    bs = jnp.where(rowblk == i, prev_end, bs)
            bstart.append(bs)
            qd.append(q[t] * jnp.exp(b[t] - bs))
            kd.append((k[t] * jnp.exp(bs - b[t])).astype(BF16))
            ke.append(k[t] * jnp.exp(be - b[t]))
        att = [jnp.where(diag, lax.dot_general(qd[t].astype(BF16), kd[t], _NT, preferred_element_type=F32), 0.0)
               for t in range(len(js))]
        for i in range(nsb - 1):
            qj, kej = [], []
            for t in range(len(js)):
                own_end = ref_rows(b[t], i)[1]
                dj = jnp.where(rowblk > i, jnp.exp(jnp.minimum(bstart[t] - own_end, 0.0)), 0.0)
                qj.append((qd[t] * dj).astype(BF16))
                kej.append(jnp.where(rowblk == i, ke[t], 0.0).astype(BF16))
            att = [att[t] + lax.dot_general(qj[t], kej[t], _NT, preferred_element_type=F32) for t in range(len(js))]
        intra = [jnp.dot(att[t].astype(BF16), v[t], preferred_element_type=F32) for t in range(len(js))]
        kdec = [(k[t] * jnp.exp(b[t][C - 1:C, :] - b[t])).astype(BF16) for t in range(len(js))]
        upd = [lax.dot_general(v[t], kdec[t], _TN, preferred_element_type=F32) for t in range(len(js))]
        for t, j in enumerate(js):
            scatter(o_ref, j, intra[t], add=False)
            upd_ref[j] = upd[t]
            ebl_ref[j] = jnp.broadcast_to(jnp.exp(b[t][C - 1:C, :]), (SUBLANES, GLA_HK))

    def recur(j):
        st = st_ref[...]
        scatter(o_ref, j, lax.dot_general(qe_ref[j], st.astype(BF16), _NT, preferred_element_type=F32), add=True)
        st_ref[...] = st * ebl_ref[j][0:1, :] + upd_ref[j]

    n_steps = CHUNK // C
    for j0 in range(0, n_steps, GLA_GROUP):
        local(list(range(j0, j0 + GLA_GROUP)))
    for j in range(n_steps):
        recur(j)

    @pl.when(c == pl.num_programs(1) - 1)
    def _():
        sfin_ref[...] = st_ref[...].T


def _gla_prompt(b, wa_pad, ba):
    T = seq = b.shape[0]
    assert seq % CHUNK == 0
    q0 = GLA_DV // GLA_HK
    k0 = q0 + GLA_DK // GLA_HK
    a0 = k0 + GLA_DK // GLA_HK
    og, sfin = pl.pallas_call(
        _gla_prompt_kernel,
        grid=(GLA_HEADS, seq // CHUNK),
        in_specs=[pl.BlockSpec((CHUNK, GLA_HK), lambda h, n: (n, q0 + h)),
                  pl.BlockSpec((CHUNK, GLA_HK), lambda h, n: (n, k0 + h)),
                  pl.BlockSpec((CHUNK, GLA_HV), lambda h, n: (n, h)),
                  pl.BlockSpec((CHUNK, LANE), lambda h, n: (n, a0)),
                  pl.BlockSpec((LANE, GLA_HK), lambda h, n: (0, h)),
                  pl.BlockSpec((1, GLA_HK), lambda h, n: (0, h))],
        out_specs=[pl.BlockSpec((CHUNK, GLA_HV), lambda h, n: (n, h)),
                   pl.BlockSpec((None, GLA_HK, GLA_HV), lambda h, n: (h, 0, 0))],
        out_shape=[jax.ShapeDtypeStruct((T, GLA_DV), F32),
                   jax.ShapeDtypeStruct((GLA_HEADS, GLA_HK, GLA_HV), F32)],
        scratch_shapes=[pltpu.VMEM((GLA_HV, GLA_HK), F32), pltpu.VMEM((CHUNK, GLA_HK), F32),
                        pltpu.VMEM((CHUNK // GLA_CHUNK, GLA_CHUNK, GLA_HK), BF16),
                        pltpu.VMEM((CHUNK // GLA_CHUNK, GLA_HV, GLA_HK), F32),
                        pltpu.VMEM((CHUNK // GLA_CHUNK, SUBLANES, GLA_HK), F32)],
        compiler_params=_cparams(2),
        name="gla_prompt",
    )(b, b, b, b, wa_pad, ba)
    return og, sfin


def _gla_sample_kernel(q_ref, k_ref, vlo_ref, vhi_ref, ga_ref, wa_ref, ba_ref, s0_ref, o_ref, snew_ref,
                       *, dec_seq, precise):
    rows8 = 2 * dec_seq
    r_i = lax.broadcasted_iota(jnp.int32, (rows8, rows8), 0)
    c_i = lax.broadcasted_iota(jnp.int32, (rows8, rows8), 1)
    same = (r_i // dec_seq == c_i // dec_seq) & (c_i <= r_i)
    tri = same.astype(F32)
    la = _log_decay(ga_ref[...], wa_ref[...], ba_ref[...])
    b_all = jnp.dot(tri, la, preferred_element_type=F32, precision=lax.Precision.HIGHEST)
    row_seq_k = lax.broadcasted_iota(jnp.int32, (rows8, GLA_HK), 0) // dec_seq
    row_seq_v = lax.broadcasted_iota(jnp.int32, (rows8, GLA_HV), 0) // dec_seq
    for h in range(GLA_HEADS):
        ks = slice(h * GLA_HK, (h + 1) * GLA_HK)
        vs = slice(h * GLA_HV, (h + 1) * GLA_HV)
        b = b_all[:, ks]
        q = q_ref[:, ks] * (GLA_HK ** -0.5)
        k = k_ref[:, ks]
        half = GLA_HEADS // 2
        v_src = vlo_ref if h < half else vhi_ref
        v = v_src[:, (h % half) * GLA_HV:(h % half + 1) * GLA_HV]
        qd = q * jnp.exp(b)
        kd = k * jnp.exp(-b)
        att = jnp.where(same, _dot(qd, kd, precise, _NT), 0.0)
        o = _dot(att, v, precise)
        for bb in range(2):
            st = s0_ref[bb, h].T
            o = o + jnp.where(row_seq_v == bb, _dot(qd, st, precise, _NT), 0.0)
            bl = b[(bb + 1) * dec_seq - 1:(bb + 1) * dec_seq, :]
            kdec = jnp.where(row_seq_k == bb, k * jnp.exp(bl - b), 0.0)
            snew_ref[bb, h] = (st * jnp.exp(bl) + _dot(v, kdec, precise, _TN)).T
        o_ref[:, vs] = o


def _gla_sample(b_s, wa_pad, ba, state, layer, dec_seq, precise):
    B = state.shape[1]
    rows8 = 2 * dec_seq
    q0 = GLA_DV // GLA_DK
    a0 = (GLA_DV + 2 * GLA_DK) // LANE
    assert q0 == 2
    blk = lambda c: pl.BlockSpec((rows8, GLA_DK), lambda i: (i, c))
    return pl.pallas_call(
        functools.partial(_gla_sample_kernel, dec_seq=dec_seq, precise=precise),
        grid=(B // 2,),
        in_specs=[blk(q0), blk(q0 + 1), blk(0), blk(1),
                  pl.BlockSpec((rows8, LANE), lambda i: (i, a0)),
                  pl.BlockSpec((LANE, GLA_DK), lambda i: (0, 0)),
                  pl.BlockSpec((1, GLA_DK), lambda i: (0, 0)),
                  pl.BlockSpec((None, 2, GLA_HEADS, GLA_HK, GLA_HV), lambda i: (layer, i, 0, 0, 0))],
        out_specs=[pl.BlockSpec((rows8, GLA_DV), lambda i: (i, 0)),
                   pl.BlockSpec((2, GLA_HEADS, GLA_HK, GLA_HV), lambda i: (i, 0, 0, 0))],
        out_shape=[jax.ShapeDtypeStruct((B * dec_seq, GLA_DV), F32), jax.ShapeDtypeStruct(state.shape[1:], F32)],
        compiler_params=_cparams(1),
        name="gla_sample",
    )(b_s, b_s, b_s, b_s, b_s, wa_pad, ba, state)


def _post_kernel(o1, o2, o3, l1, l2, l3, og_ref, xin_ref, x_ref, *rest, alpha, precise):
    n_gate = 2 if precise else 1
    wgate_refs = rest[:n_gate]
    wpa_ref, wpb_ref, wout_ref, gn_ref, lg_ref, lb_ref, xo_ref, xb_ref = rest[n_gate:]
    wgate = tuple(r[...] for r in wgate_refs) if precise else wgate_refs[0][...]
    gates = _dot(xin_ref[...], wgate, precise)
    D = x_ref.shape[1]
    gr, gate_a, gate_b = gates[:, :GLA_DV], gates[:, GLA_DV:GLA_DV + D], gates[:, GLA_DV + D:]
    m = jnp.maximum(jnp.maximum(l1[...], l2[...]), l3[...])
    e1 = jnp.exp(l1[...] - m)
    e2 = jnp.exp(l2[...] - m)
    e3 = jnp.exp(l3[...] - m)
    att = (e1 * o1[...] + e2 * o2[...] + e3 * o3[...]) / (e1 + e2 + e3)
    ya = _dot(att, wpa_ref[...], precise)
    parts = []
    for h in range(GLA_HEADS):
        vs = slice(h * GLA_HV, (h + 1) * GLA_HV)
        og = og_ref[:, vs]
        ms = jnp.mean(og * og, axis=-1, keepdims=True)
        parts.append(_operand(og * lax.rsqrt(ms + RMS_EPS) * gn_ref[:, vs] * _silu(gr[:, vs]), precise))
    y = jnp.concatenate(parts, axis=-1)
    yb = _dot(y, wpb_ref[...], precise)
    merged = _sigmoid(gate_a) * ya + _sigmoid(gate_b) * yb
    mix = _dot(merged, wout_ref[...], precise)
    xn = _layer_norm_rows(alpha * x_ref[...] + mix, lg_ref[...], lb_ref[...])
    xo_ref[...] = xn
    xb_ref[...] = xn.astype(BF16)


def _post(os_, ls_, og, xin, x, wgate, wpa, wpb, wout, gn, lg, lb, alpha, tm):
    precise = isinstance(wgate, tuple)
    wgates = wgate if precise else (wgate,)
    T, D = x.shape
    row = lambda w: pl.BlockSpec((tm, w), lambda i: (i, 0))
    full = lambda a: pl.BlockSpec(a.shape, lambda i: (0,) * a.ndim)
    resident = lambda a: pl.BlockSpec(a.shape, lambda i: (0,) * a.ndim, pipeline_mode=pl.Buffered(1))
    return pl.pallas_call(
        functools.partial(_post_kernel, alpha=alpha, precise=precise),
        grid=(pl.cdiv(T, tm),),
        in_specs=[row(ATT_WIDTH)] * 6 + [row(GLA_DV), row(D), row(D)] + [resident(w) for w in wgates]
                 + [resident(wpa), resident(wpb), resident(wout), full(gn), full(lg), full(lb)],
        out_specs=[row(D), row(D)],
        out_shape=[jax.ShapeDtypeStruct((T, D), F32), jax.ShapeDtypeStruct((T, D), BF16)],
        compiler_params=_cparams(1),
        name="post_mixer_sample" if precise else "post_mixer",
    )(*os_, *ls_, og, xin, x, *wgates, wpa, wpb, wout, gn, lg, lb)


def _ffn_kernel(xb_ref, x_ref, wg_ref, wu_ref, wd_ref, lg_ref, lb_ref, xo_ref, xbo_ref, acc_ref, *, alpha, precise):
    j = pl.program_id(1)
    xb = xb_ref[...]
    g = _dot(xb, wg_ref[...], precise)
    u = _dot(xb, wu_ref[...], precise)
    part = _dot(_silu(g) * u, wd_ref[...], precise)

    @pl.when(j == 0)
    def _():
        acc_ref[...] = part

    @pl.when(j > 0)
    def _():
        acc_ref[...] += part

    @pl.when(j == pl.num_programs(1) - 1)
    def _():
        xn = _layer_norm_rows(alpha * x_ref[...] + acc_ref[...], lg_ref[...], lb_ref[...])
        xo_ref[...] = xn
        xbo_ref[...] = xn.astype(BF16)


def _ffn(xb, x, wgu, wd, lg, lb, alpha, tm, tc, precise=False):
    T, D = x.shape
    dff = wd.shape[0]
    assert dff % tc == 0
    nj = dff // tc
    full = lambda a: pl.BlockSpec(a.shape, lambda i, j: (0,) * a.ndim)
    wmode = dict(pipeline_mode=pl.Buffered(1)) if nj == 1 else {}
    return pl.pallas_call(
        functools.partial(_ffn_kernel, alpha=alpha, precise=precise),
        grid=(pl.cdiv(T, tm), nj),
        in_specs=[pl.BlockSpec((tm, D), lambda i, j: (i, 0)),
                  pl.BlockSpec((tm, D), lambda i, j: (i, 0)),
                  pl.BlockSpec((D, tc), lambda i, j: (0, j), **wmode),
                  pl.BlockSpec((D, tc), lambda i, j: (0, nj + j), **wmode),
                  pl.BlockSpec((tc, D), lambda i, j: (j, 0), **wmode),
                  full(lg), full(lb)],
        out_specs=[pl.BlockSpec((tm, D), lambda i, j: (i, 0))] * 2,
        out_shape=[jax.ShapeDtypeStruct((T, D), F32), jax.ShapeDtypeStruct((T, D), BF16)],
        scratch_shapes=[pltpu.VMEM((tm, D), F32)],
        compiler_params=_cparams(2),
        name="ffn_dense_sample" if precise else "ffn_dense",
    )(xb, x, wgu, wgu, wd, lg, lb)


MOE_TOKENS = 512
MOE_PIECE = 16
MOE_ROWS = -(-(2 * MOE_TOKENS + N_EXPERTS * (MOE_PIECE - 1)) // MOE_PIECE) * MOE_PIECE
MOE_BLOCK = 1024
MOE_HALF = 512
MOE_SLOTS = MOE_BLOCK // MOE_PIECE


def _moe_route_kernel(x_ref, xb_ref, wrt_ref, xs_ref, meta_ref, cnt_ref, *, n_tokens):
    i = pl.program_id(0)
    ts = x_ref.shape[0]
    row_ok = (lax.broadcasted_iota(jnp.int32, (ts, 1), 0) + i * ts) < n_tokens
    x = jnp.where(row_ok, x_ref[...], 0.0)
    xb = jnp.where(row_ok, xb_ref[...], jnp.zeros((), BF16))
    lt = lax.dot_general(wrt_ref[...], x, _NT, preferred_element_type=F32, precision=lax.Precision.HIGHEST)
    sub = lax.broadcasted_iota(jnp.int32, (N_EXPERTS, ts), 0).astype(F32)
    tok_ok = (lax.broadcasted_iota(jnp.int32, (1, ts), 1) + i * ts) < n_tokens
    m1 = jnp.max(lt, axis=0, keepdims=True)
    i1 = jnp.min(jnp.where(lt == m1, sub, float(N_EXPERTS)), axis=0, keepdims=True)
    lt2 = jnp.where(sub == i1, NEG, lt)
    m2 = jnp.max(lt2, axis=0, keepdims=True)
    i2 = jnp.min(jnp.where(lt2 == m2, sub, float(N_EXPERTS)), axis=0, keepdims=True)
    e2 = jnp.exp(m2 - m1)
    g1 = 1.0 / (1.0 + e2)
    g2 = e2 / (1.0 + e2)
    oh1 = (sub == i1) & tok_ok
    oh2 = (sub == i2) & tok_ok
    assign = jnp.where(oh1 | oh2, 1.0, 0.0)
    before = (lax.broadcasted_iota(jnp.int32, (ts, ts), 0) < lax.broadcasted_iota(jnp.int32, (ts, ts), 1))
    rank = jnp.dot(assign.astype(BF16), jnp.where(before, 1.0, 0.0).astype(BF16), preferred_element_type=F32)
    cnt = jnp.sum(assign, axis=1, keepdims=True)
    padded = jnp.floor((cnt + (MOE_PIECE - 1)) * (1.0 / MOE_PIECE)) * MOE_PIECE
    lower = (lax.broadcasted_iota(jnp.int32, (N_EXPERTS, N_EXPERTS), 1)
             < lax.broadcasted_iota(jnp.int32, (N_EXPERTS, N_EXPERTS), 0)).astype(F32)
    seg_off = jnp.dot(lower, jnp.broadcast_to(padded, (N_EXPERTS, ts)), preferred_element_type=F32,
                      precision=lax.Precision.HIGHEST)
    dest = seg_off + rank
    d1 = jnp.where(tok_ok, jnp.sum(jnp.where(oh1, dest, 0.0), axis=0, keepdims=True), -1.0)
    d2 = jnp.where(tok_ok, jnp.sum(jnp.where(oh2, dest, 0.0), axis=0, keepdims=True), -1.0)
    p_i = lax.broadcasted_iota(jnp.int32, (MOE_ROWS, ts), 0).astype(F32)
    perm = jnp.where((p_i == d1) | (p_i == d2), 1.0, 0.0).astype(BF16)
    xs_ref[...] = jnp.dot(perm, xb, preferred_element_type=F32).astype(BF16)
    sub8 = lax.broadcasted_iota(jnp.int32, (SUBLANES, ts), 0)
    meta_ref[...] = jnp.where(sub8 == 0, d1, jnp.where(sub8 == 1, d2, jnp.where(sub8 == 2, g1,
                              jnp.where(sub8 == 3, g2, 0.0))))
    cnt_ref[...] = jnp.broadcast_to(cnt, (N_EXPERTS, LANE))


def _moe_route(x, xb, wr_t):
    T, D = x.shape
    nt = pl.cdiv(T, MOE_TOKENS)
    return pl.pallas_call(
        functools.partial(_moe_route_kernel, n_tokens=T),
        grid=(nt,),
        in_specs=[pl.BlockSpec((MOE_TOKENS, D), lambda i: (i, 0)),
                  pl.BlockSpec((MOE_TOKENS, D), lambda i: (i, 0)),
                  pl.BlockSpec(wr_t.shape, lambda i: (0, 0))],
        out_specs=[pl.BlockSpec((MOE_ROWS, D), lambda i: (i, 0)),
                   pl.BlockSpec((None, SUBLANES, MOE_TOKENS), lambda i: (i, 0, 0)),
                   pl.BlockSpec((None, N_EXPERTS, LANE), lambda i: (i, 0, 0))],
        out_shape=[jax.ShapeDtypeStruct((nt * MOE_ROWS, D), BF16),
                   jax.ShapeDtypeStruct((nt, SUBLANES, MOE_TOKENS), F32),
                   jax.ShapeDtypeStruct((nt, N_EXPERTS, LANE), F32)],
        compiler_params=_cparams(1),
        name="moe_route",
    )(x, xb, wr_t)


def _moe_plan(cnt, n_blocks):
    nt = cnt.shape[0]
    i32 = jnp.int32
    padded = (cnt + MOE_PIECE - 1) // MOE_PIECE * MOE_PIECE
    seg_row = jnp.cumsum(padded, axis=1) - padded + (jnp.arange(nt, dtype=i32) * MOE_ROWS)[:, None]
    n_seg = (padded // MOE_PIECE).T.reshape(-1)
    seg_row = seg_row.T.reshape(-1)
    seg_end = jnp.cumsum(n_seg)
    seg_start = seg_end - n_seg
    per_e = n_seg.reshape(N_EXPERTS, nt).sum(axis=1)
    e_end = jnp.cumsum(per_e)
    e_start = e_end - per_e
    blocks_e = (per_e + MOE_SLOTS - 1) // MOE_SLOTS
    blk_end = jnp.cumsum(blocks_e)
    blk_start = blk_end - blocks_e
    w = jnp.arange(n_blocks, dtype=i32)
    ew = jnp.minimum(jnp.sum((w[:, None] >= blk_end[None, :]).astype(i32), axis=1), N_EXPERTS - 1)
    is_e = ew[:, None] == jnp.arange(N_EXPERTS, dtype=i32)[None, :]
    pick = lambda v: jnp.sum(jnp.where(is_e, v[None, :], 0), axis=1)
    first = pick(e_start) + (w - pick(blk_start)) * MOE_SLOTS
    n_w = jnp.where(w < blk_end[-1], jnp.clip(pick(e_end) - first, 0, MOE_SLOTS), 0).astype(i32)
    slot = jnp.arange(MOE_SLOTS, dtype=i32)
    p = first[:, None] + slot[None, :]
    in_seg = (p[:, :, None] >= seg_start[None, None, :]) & (p[:, :, None] < seg_end[None, None, :])
    rows = jnp.sum(jnp.where(in_seg, (seg_row - MOE_PIECE * seg_start)[None, None, :], 0), axis=2) + MOE_PIECE * p
    rows = jnp.where(slot[None, :] < n_w[:, None], rows, 0).astype(i32)
    return ew.astype(i32), n_w, rows.reshape(-1)


def _moe_expert_kernel(ew_ref, n_ref, rows_ref, xs_hbm, wg_ref, wu_ref, wd_ref, ys_in, ys_hbm,
                       lhs_ref, acc_ref, out_ref, sem_in, sem_out):
    del ew_ref, ys_in
    w = pl.program_id(0)
    j = pl.program_id(1)
    n_w = pl.num_programs(0)
    last_j = pl.num_programs(1) - 1
    n = n_ref[w]
    slot = w % 2
    halves = [(s, slice(s * MOE_HALF, (s + 1) * MOE_HALF)) for s in range(MOE_BLOCK // MOE_HALF)]

    def piece_in(blk, q, sl):
        src = pl.multiple_of(rows_ref[blk * MOE_SLOTS + q], MOE_PIECE)
        dst = pl.multiple_of(q * MOE_PIECE, MOE_PIECE)
        return pltpu.make_async_copy(xs_hbm.at[pl.ds(src, MOE_PIECE), :],
                                     lhs_ref.at[sl, pl.ds(dst, MOE_PIECE), :], sem_in.at[sl])

    def piece_out(blk, q):
        dst = pl.multiple_of(rows_ref[blk * MOE_SLOTS + q], MOE_PIECE)
        src = pl.multiple_of(q * MOE_PIECE, MOE_PIECE)
        return pltpu.make_async_copy(out_ref.at[pl.ds(src, MOE_PIECE), :], ys_hbm.at[pl.ds(dst, MOE_PIECE), :], sem_out)

    def for_pieces(blk, fn):
        lax.fori_loop(0, n_ref[blk], lambda q, c: (fn(q), c)[1], 0)

    @pl.when(j == 0)
    def _():
        @pl.when(w == 0)
        def _():
            lhs_ref[...] = jnp.zeros_like(lhs_ref)
            for_pieces(0, lambda q: piece_in(0, q, 0).start())

        for_pieces(w, lambda q: piece_in(w, q, slot).wait())

        @pl.when(w + 1 < n_w)
        def _():
            for_pieces(w + 1, lambda q: piece_in(w + 1, q, 1 - slot).start())

    for s, rows in halves:
        @pl.when(n > s * (MOE_HALF // MOE_PIECE))
        def _(rows=rows):
            xb = lhs_ref[slot, rows, :]
            g = jnp.dot(xb, wg_ref[...], preferred_element_type=F32)
            u = jnp.dot(xb, wu_ref[...], preferred_element_type=F32)
            part = jnp.dot((_silu(g) * u).astype(BF16), wd_ref[...], preferred_element_type=F32)

            @pl.when(j == 0)
            def _():
                acc_ref[rows, :] = part

            @pl.when(j > 0)
            def _():
                acc_ref[rows, :] += part

    @pl.when(j == last_j)
    def _():
        @pl.when(w > 0)
        def _():
            for_pieces(w - 1, lambda q: piece_out(w - 1, q).wait())

        for s, rows in halves:
            @pl.when(n > s * (MOE_HALF // MOE_PIECE))
            def _(rows=rows):
                out_ref[rows, :] = acc_ref[rows, :].astype(BF16)

        for_pieces(w, lambda q: piece_out(w, q).start())

        @pl.when(w == n_w - 1)
        def _():
            for_pieces(w, lambda q: piece_out(w, q).wait())


def _moe_experts(xs, plan, wgu, wd, tc):
    ew, n_w, rows = plan
    n_blocks = ew.shape[0]
    D = xs.shape[1]
    dff = wd.shape[1]
    assert dff % tc == 0
    nj = dff // tc
    any_spec = pl.BlockSpec(memory_space=pl.ANY)
    return pl.pallas_call(
        _moe_expert_kernel,
        grid_spec=pltpu.PrefetchScalarGridSpec(
            num_scalar_prefetch=3,
            grid=(n_blocks, nj),
            in_specs=[any_spec,
                      pl.BlockSpec((None, D, tc), lambda w, j, ew, n, r: (ew[w], 0, j)),
                      pl.BlockSpec((None, D, tc), lambda w, j, ew, n, r: (ew[w], 0, nj + j)),
                      pl.BlockSpec((None, tc, D), lambda w, j, ew, n, r: (ew[w], j, 0)),
                      any_spec],
            out_specs=any_spec,
            scratch_shapes=[pltpu.VMEM((2, MOE_BLOCK, D), BF16), pltpu.VMEM((MOE_BLOCK, D), F32),
                            pltpu.VMEM((MOE_BLOCK, D), BF16),
                            pltpu.SemaphoreType.DMA((2,)), pltpu.SemaphoreType.DMA(())]),
        out_shape=jax.ShapeDtypeStruct(xs.shape, BF16),
        input_output_aliases={7: 0},
        compiler_params=_cparams(2),
        name="moe_experts",
    )(ew, n_w, rows, xs, wgu, wgu, wd, jnp.zeros(xs.shape, BF16))


def _moe_combine_kernel(ys_ref, meta_ref, x_ref, lg_ref, lb_ref, out_a, out_b, *, alpha, prompt_tiles):
    ts = x_ref.shape[0]
    meta = meta_ref[...]
    d1, d2, g1, g2 = meta[0:1, :], meta[1:2, :], meta[2:3, :], meta[3:4, :]
    p_i = lax.broadcasted_iota(jnp.int32, (MOE_ROWS, ts), 0).astype(F32)
    back = (jnp.where(p_i == d1, g1, 0.0) + jnp.where(p_i == d2, g2, 0.0)).astype(BF16)
    f = lax.dot_general(back, ys_ref[...], _TN, preferred_element_type=F32)
    xn = _layer_norm_rows(alpha * x_ref[...] + f, lg_ref[...], lb_ref[...])
    if prompt_tiles is None:
        out_a[...] = xn
        out_b[...] = xn.astype(BF16)
    else:
        i = pl.program_id(0)

        @pl.when(i < prompt_tiles)
        def _():
            out_a[...] = xn

        @pl.when(i == prompt_tiles)
        def _():
            out_b[...] = xn[:out_b.shape[0], :]


def _moe_combine(ys, meta, x, lg, lb, alpha, split_at):
    T, D = x.shape
    nt = meta.shape[0]
    full = lambda a: pl.BlockSpec(a.shape, lambda i: (0,) * a.ndim)
    if split_at is None:
        prompt_tiles = None
        out_specs = [pl.BlockSpec((MOE_TOKENS, D), lambda i: (i, 0))] * 2
        out_shape = [jax.ShapeDtypeStruct((T, D), F32), jax.ShapeDtypeStruct((T, D), BF16)]
    else:
        prompt_tiles, ns = split_at // MOE_TOKENS, T - split_at
        assert split_at % MOE_TOKENS == 0 and 0 < ns <= MOE_TOKENS and nt == prompt_tiles + 1
        out_specs = [pl.BlockSpec((MOE_TOKENS, D), lambda i: (jnp.minimum(i, prompt_tiles - 1), 0)),
                     pl.BlockSpec((ns, D), lambda i: (0, 0))]
        out_shape = [jax.ShapeDtypeStruct((split_at, D), F32), jax.ShapeDtypeStruct((ns, D), F32)]
    return pl.pallas_call(
        functools.partial(_moe_combine_kernel, alpha=alpha, prompt_tiles=prompt_tiles),
        grid=(nt,),
        in_specs=[pl.BlockSpec((MOE_ROWS, D), lambda i: (i, 0)),
                  pl.BlockSpec((None, SUBLANES, MOE_TOKENS), lambda i: (i, 0, 0)),
                  pl.BlockSpec((MOE_TOKENS, D), lambda i: (i, 0)),
                  full(lg), full(lb)],
        out_specs=out_specs,
        out_shape=out_shape,
        compiler_params=_cparams(1),
        name="moe_combine",
    )(ys, meta, x, lg, lb)


def _moe(x, xb, w_router, wgu, wd, lg, lb, alpha, split_at=None):
    T = x.shape[0]
    nt = pl.cdiv(T, MOE_TOKENS)
    xs, meta, cnt = _moe_route(x, xb, w_router.T)
    n_pieces = (2 * T + nt * N_EXPERTS * (MOE_PIECE - 1)) // MOE_PIECE + 1
    n_blocks = pl.cdiv(n_pieces, MOE_SLOTS) + N_EXPERTS
    plan = _moe_plan(cnt[:, :, 0].astype(jnp.int32), n_blocks)
    ys = _moe_experts(xs, plan, wgu, wd, MOE_CHUNK)
    return _moe_combine(ys, meta, x, lg, lb, alpha, split_at)


def _split_w_in(w):
    a_end = N_GROUPS * 3 * ATT_WIDTH
    gq0, gk0, gv0 = a_end, a_end + GLA_DK, a_end + 2 * GLA_DK
    gr0 = gv0 + GLA_DV
    ga0 = gr0 + GLA_DV
    gta0 = ga0 + GLA_RANK
    used = GLA_DV + 2 * GLA_DK + GLA_RANK
    ga_pad = jnp.pad(w[:, ga0:gta0], ((0, 0), (0, GLA_PANEL - used)))
    panels = (w[:, :a_end],
              jnp.concatenate([w[:, gv0:gr0], w[:, gq0:gk0], w[:, gk0:gv0], ga_pad], axis=1),
              jnp.concatenate([w[:, gr0:ga0], w[:, gta0:]], axis=1))
    return tuple(_head_tail(p) for p in panels)


def _head_tail(w):
    head = w.astype(BF16)
    return head, (w - head.astype(F32)).astype(BF16)


def _kv_rows_kernel(kp_ref, vp_ref, ks_ref, vs_ref, op_ref, os_ref, *, keep):
    n = keep // RUNS
    ns = ks_ref.shape[0]
    for kv, (p_ref, s_ref) in enumerate(((kp_ref, ks_ref), (vp_ref, vs_ref))):
        for h in range(HEADS):
            cs = slice(h * HEAD_DIM, (h + 1) * HEAD_DIM)
            sub = kv * HEADS + h
            for r in range(RUNS):
                op_ref[pl.ds(2 * HEADS * r + sub, n, stride=2 * HEADS * RUNS), :] = \
                    p_ref[(r + 1) * RUN_LEN - n:(r + 1) * RUN_LEN, cs]
            os_ref[pl.ds(sub, ns, stride=2 * HEADS), :] = s_ref[:, cs]


def _new_kv_rows(a_mat, a_s, g, db, dec_seq):
    seq = a_mat.shape[0]
    keep = min(ATT_GROUPS[g][0], seq)
    ns = db * dec_seq
    assert keep <= CHUNK and keep % RUNS == 0 and seq % CHUNK == 0 and a_s.shape[0] == ns
    last = seq // CHUNK - 1
    spec = lambda rows, rb, c: pl.BlockSpec((rows, ATT_WIDTH), lambda i: (rb, 3 * g + c))
    kv_p, kv_s = pl.pallas_call(
        functools.partial(_kv_rows_kernel, keep=keep),
        grid=(1,),
        in_specs=[spec(CHUNK, last, 1), spec(CHUNK, last, 2), spec(ns, 0, 1), spec(ns, 0, 2)],
        out_specs=[pl.BlockSpec((keep * 2 * HEADS, HEAD_DIM), lambda i: (0, 0)),
                   pl.BlockSpec((ns * 2 * HEADS, HEAD_DIM), lambda i: (0, 0))],
        out_shape=[jax.ShapeDtypeStruct((keep * 2 * HEADS, HEAD_DIM), F32),
                   jax.ShapeDtypeStruct((ns * 2 * HEADS, HEAD_DIM), F32)],
        compiler_params=_cparams(1),
        name=f"kv_rows_g{g}",
    )(a_mat, a_mat, a_s, a_s)
    return kv_p.reshape(1, keep, 2, HEADS, HEAD_DIM), kv_s.reshape(db, dec_seq, 2, HEADS, HEAD_DIM)


def kernel(x_prompt, x_sample, cache_kv_w128, cache_kv_w512, cache_kv_w2048, state_gla,
           w_in, w_alpha2, b_alpha, gla_norm_g, w_pa, w_pb, w_out,
           ln_mix_g, ln_mix_b, ln_ffn_g, ln_ffn_b,
           w_ffn_gu, w_ffn_down, w_router, w_exp_gu, w_exp_down):
    depth = w_in.shape[0]
    bp, seq, D = x_prompt.shape
    db, dec_seq, _ = x_sample.shape
    assert bp == 1
    ns = db * dec_seq
    alpha = (2 * depth) ** 0.25
    caches = (cache_kv_w128, cache_kv_w512, cache_kv_w2048)

    x = _interleave_rows(x_prompt.reshape(seq, D))
    xb = x.astype(BF16)
    xs = x_sample.reshape(ns, D)
    kv_p = [[] for _ in range(N_GROUPS)]
    kv_s = [[] for _ in range(N_GROUPS)]
    st_p, st_s = [], []
    row2 = lambda v: v.reshape(1, -1)

    for l in range(depth):
        wa, wb, wg = _split_w_in(w_in[l])
        a_mat = _mm(xb, wa[0], PROJ_ROWS, PROJ_COLS_ATT)
        b_mat = _mm(xb, wb[0], PROJ_ROWS, GLA_PANEL)
        a_s = _mm(xs, wa, ns, PROJ_COLS_ATT)
        b_s = _mm(xs, wb, ns, GLA_PANEL)

        os_, ls_, os_s, ls_s = [], [], [], []
        for g in range(N_GROUPS):
            o, lse = _attn_prompt(a_mat, g)
            os_.append(o)
            ls_.append(lse)
            o, lse = _attn_sample(a_s, caches[g], l, g, dec_seq, precise=True)
            os_s.append(o)
            ls_s.append(lse)
            p_rows, s_rows = _new_kv_rows(a_mat, a_s, g, db, dec_seq)
            kv_p[g].append(p_rows)
            kv_s[g].append(s_rows)

        wa_pad = jnp.pad(w_alpha2[l], ((0, LANE - GLA_RANK), (0, 0)))
        og, s_p = _gla_prompt(b_mat, wa_pad, row2(b_alpha[l]))
        og_s, s_s = _gla_sample(b_s, wa_pad, row2(b_alpha[l]), state_gla, l, dec_seq, precise=True)
        st_p.append(s_p[None])
        st_s.append(s_s)

        norms = (row2(gla_norm_g[l]), row2(ln_mix_g[l]), row2(ln_mix_b[l]))
        x, xb = _post(os_, ls_, og, xb, x, wg[0], w_pa[l].astype(BF16), w_pb[l].astype(BF16), w_out[l].astype(BF16),
                      *norms, alpha, POST_ROWS)
        xs, _ = _post(os_s, ls_s, og_s, xs, xs, wg, w_pa[l], w_pb[l], w_out[l], *norms, alpha, ns)

        ffn_norm = (row2(ln_ffn_g[l]), row2(ln_ffn_b[l]), alpha)
        if l % 2 == 0:
            x, xb = _ffn(xb, x, w_ffn_gu[l // 2].astype(BF16), w_ffn_down[l // 2].astype(BF16),
                         *ffn_norm, FFN_ROWS, FFN_CHUNK)
            xs, _ = _ffn(xs, xs, w_ffn_gu[l // 2], w_ffn_down[l // 2], *ffn_norm, ns, SAMPLE_FFN_CHUNK, precise=True)
        else:
            moe_w = (w_router[l // 2], w_exp_gu[l // 2].astype(BF16), w_exp_down[l // 2].astype(BF16), *ffn_norm)
            x_all = jnp.concatenate([x, xs], axis=0)
            xb_all = jnp.concatenate([xb, xs.astype(BF16)], axis=0)
            if l == depth - 1:
                y_prompt, y_sample = _moe(x_all, xb_all, *moe_w, split_at=seq)
            else:
                x_all, xb_all = _moe(x_all, xb_all, *moe_w)
                x, xb, xs = x_all[:seq], xb_all[:seq], x_all[seq:]
    if depth % 2 == 1:
        y_prompt, y_sample = x, xs

    return (_deinterleave_rows(y_prompt).reshape(1, seq, D), y_sample.reshape(db, dec_seq, D),
            jnp.stack(kv_p[0], 0), jnp.stack(kv_s[0], 0),
            jnp.stack(kv_p[1], 0), jnp.stack(kv_s[1], 0),
            jnp.stack(kv_p[2], 0), jnp.stack(kv_s[2], 0),
            jnp.stack(st_p, 0), jnp.stack(st_s, 0))
```

```python
import functools

import jax
import jax.numpy as jnp
from jax import lax
from jax.experimental import pallas as pl
from jax.experimental.pallas import tpu as pltpu

F32 = jnp.float32
BF16 = jnp.bfloat16

ATT_GROUPS = ((128, 1), (512, 4), (2048, 16))
N_GROUPS = 3
HEADS = 4
HEAD_DIM = 128
ATT_WIDTH = HEADS * HEAD_DIM
BAND = 128
ATT_GROUP = 8
GLA_HEADS = 4
GLA_HK = 128
GLA_HV = 256
GLA_DK = GLA_HEADS * GLA_HK
GLA_DV = GLA_HEADS * GLA_HV
GLA_RANK = 16
GLA_TAU = 16.0
GLA_CHUNK = 128
GLA_SUB = 32
GLA_GROUP = 8
N_EXPERTS = 8
LN_EPS = 1e-5
RMS_EPS = 1e-6
NEG = -1e30
LOG2E = 1.4426950408889634
LN2 = 0.6931471805599453
LANE = 128
SUBLANES = 8
RUNS = 16
RUN_LEN = 128
CHUNK = RUNS * RUN_LEN
VMEM_LIMIT = 56 * 1024 * 1024

PROJ_ROWS = 1024
PROJ_COLS_ATT = 1536
GLA_PANEL = 2304
POST_ROWS = 256
FFN_ROWS = 512
FFN_CHUNK = 2816
MOE_CHUNK = 1792
SAMPLE_FFN_CHUNK = 256

_NT = (((1,), (1,)), ((), ()))
_TN = (((0,), (0,)), ((), ()))


def _alibi_slopes():
    n = N_GROUPS * HEADS
    return [[2.0 ** (-8.0 * (g * HEADS + h + 1) / n) for h in range(HEADS)] for g in range(N_GROUPS)]


def _cparams(n_axes):
    return pltpu.CompilerParams(dimension_semantics=("arbitrary",) * n_axes, vmem_limit_bytes=VMEM_LIMIT)


def _layer_norm_rows(r, g, b):
    mu = jnp.mean(r, axis=-1, keepdims=True)
    c = r - mu
    var = jnp.mean(c * c, axis=-1, keepdims=True)
    return c * lax.rsqrt(var + LN_EPS) * g + b


def _sigmoid(x):
    return 1.0 / (1.0 + jnp.exp(-x))


def _silu(x):
    return x * _sigmoid(x)


def _operand(x, precise):
    return x.astype(F32 if precise else BF16)


def _dot(a, b, precise, dims=None):
    def mm(u, v):
        if dims is None:
            return jnp.dot(u, v, preferred_element_type=F32)
        return lax.dot_general(u, v, dims, preferred_element_type=F32)

    if not precise:
        return mm(a.astype(BF16), b.astype(BF16))
    a = a.astype(F32)
    a_hi = a.astype(BF16)
    a_lo = (a - a_hi.astype(F32)).astype(BF16)
    if isinstance(b, tuple):
        b_hi, b_lo = b
    else:
        b = b.astype(F32)
        b_hi = b.astype(BF16)
        b_lo = (b - b_hi.astype(F32)).astype(BF16)
    return mm(a_hi, b_hi) + (mm(a_hi, b_lo) + mm(a_lo, b_hi))


def _interleave_rows(x):
    s, d = x.shape
    return x.reshape(s // CHUNK, RUN_LEN, RUNS, d).transpose(0, 2, 1, 3).reshape(s, d)


def _deinterleave_rows(x):
    s, d = x.shape
    return x.reshape(s // CHUNK, RUNS, RUN_LEN, d).transpose(0, 2, 1, 3).reshape(s, d)


def _mm_kernel(x_ref, w_ref, o_ref):
    o_ref[...] = _dot(x_ref[...], w_ref[...], False)


def _mm_precise_kernel(x_ref, wh_ref, wl_ref, o_ref):
    o_ref[...] = _dot(x_ref[...], (wh_ref[...], wl_ref[...]), True)


def _mm(x, w, tm, tn):
    precise = isinstance(w, tuple)
    ws = w if precise else (w,)
    M, K = x.shape
    N = ws[0].shape[1]
    assert N % tn == 0
    w_spec = pl.BlockSpec((K, tn), lambda j, i: (0, j))
    return pl.pallas_call(
        _mm_precise_kernel if precise else _mm_kernel,
        grid=(N // tn, pl.cdiv(M, tm)),
        in_specs=[pl.BlockSpec((tm, K), lambda j, i: (i, 0))] + [w_spec] * len(ws),
        out_specs=pl.BlockSpec((tm, tn), lambda j, i: (i, j)),
        out_shape=jax.ShapeDtypeStruct((M, N), F32),
        compiler_params=_cparams(2),
        name="in_proj_sample" if precise else "in_proj",
    )(x, *ws)


def _attn_prompt_kernel(slope_ref, q_ref, k_ref, v_ref, o_ref, l_ref, kp_ref, vp_ref, *, dil):
    h = pl.program_id(0)
    c = pl.program_id(1)
    G = RUNS // dil
    wq = BAND // G
    scale = HEAD_DIM ** -0.5

    @pl.when(c == 0)
    def _():
        kp_ref[...] = jnp.zeros_like(kp_ref)
        vp_ref[...] = jnp.zeros_like(vp_ref)

    qi = lax.broadcasted_iota(jnp.int32, (BAND, 2 * BAND), 0)
    ci = lax.broadcasted_iota(jnp.int32, (BAND, 2 * BAND), 1)
    e = ci % (2 * wq)
    rel = (G * (qi % wq) + qi // wq) - (G * (e - wq) + ci // (2 * wq))
    valid = (rel >= 0) & (rel <= BAND)
    bias = rel.astype(F32) * (-slope_ref[h] * (dil * LOG2E))
    bias_in = jnp.where(valid, bias, NEG)
    first_e = jnp.where(c > 0, 0, wq)
    bias_first = jnp.where(valid & (e >= first_e), bias, NEG)

    def keys(cur_ref, prev_ref, runs, m):
        if m == 0:
            parts = []
            for rho in runs:
                parts += [prev_ref[(rho + 1) * RUN_LEN - wq:(rho + 1) * RUN_LEN, :],
                          cur_ref[rho * RUN_LEN: rho * RUN_LEN + wq, :]]
        else:
            parts = [cur_ref[rho * RUN_LEN + wq * (m - 1): rho * RUN_LEN + wq * (m + 1), :] for rho in runs]
        return jnp.concatenate(parts, axis=0).astype(BF16)

    blocks = [([r + dil * v for v in range(G)], m) for r in range(dil) for m in range(G)]
    for g0 in range(0, len(blocks), ATT_GROUP):
        grp = blocks[g0:g0 + ATT_GROUP]
        q = [(jnp.concatenate([q_ref[rho * RUN_LEN + wq * m: rho * RUN_LEN + wq * (m + 1), :] for rho in runs],
                              axis=0) * (scale * LOG2E)).astype(BF16) for runs, m in grp]
        s = [lax.dot_general(q[t], keys(k_ref, kp_ref, runs, m), _NT, preferred_element_type=F32)
             + (bias_first if m == 0 else bias_in) for t, (runs, m) in enumerate(grp)]
        mx = [jnp.max(x, axis=-1, keepdims=True) for x in s]
        p = [jnp.exp2(s[t] - mx[t]) for t in range(len(grp))]
        l = [jnp.sum(x, axis=-1, keepdims=True) for x in p]
        o = [jnp.dot(p[t].astype(BF16), keys(v_ref, vp_ref, runs, m), preferred_element_type=F32) / l[t]
             for t, (runs, m) in enumerate(grp)]
        for t, (runs, m) in enumerate(grp):
            lse = jnp.broadcast_to(mx[t] * LN2 + jnp.log(l[t]), (BAND, HEAD_DIM))
            for idx, rho in enumerate(runs):
                rows = slice(rho * RUN_LEN + wq * m, rho * RUN_LEN + wq * (m + 1))
                o_ref[rows, :] = o[t][idx * wq:(idx + 1) * wq, :]
                l_ref[rows, :] = lse[idx * wq:(idx + 1) * wq, :]

    kp_ref[...] = k_ref[...]
    vp_ref[...] = v_ref[...]


def _attn_prompt(a_mat, g):
    T = seq = a_mat.shape[0]
    _, dil = ATT_GROUPS[g]
    assert seq % CHUNK == 0 and RUNS % dil == 0
    slopes = jnp.asarray(_alibi_slopes()[g], F32)

    def col(c):
        return pl.BlockSpec((CHUNK, HEAD_DIM), lambda h, n: (n, (3 * g + c) * HEADS + h))

    out_spec = pl.BlockSpec((CHUNK, HEAD_DIM), lambda h, n: (n, h))
    out_sds = jax.ShapeDtypeStruct((T, ATT_WIDTH), F32)
    return pl.pallas_call(
        functools.partial(_attn_prompt_kernel, dil=dil),
        grid=(HEADS, seq // CHUNK),
        in_specs=[pl.BlockSpec(memory_space=pltpu.SMEM), col(0), col(1), col(2)],
        out_specs=[out_spec, out_spec],
        out_shape=[out_sds, out_sds],
        scratch_shapes=[pltpu.VMEM((CHUNK, HEAD_DIM), F32), pltpu.VMEM((CHUNK, HEAD_DIM), F32)],
        compiler_params=_cparams(2),
        name=f"attn_prompt_g{g}",
    )(slopes, a_mat, a_mat, a_mat)


def _attn_sample_kernel(q_ref, k_ref, v_ref, c_ref, o_ref, l_ref, *, slopes, window, dil, dec_seq, nsub, precise):
    R = c_ref.shape[1]
    wb = window
    stride = wb // R
    rows8 = 2 * dec_seq
    scale = HEAD_DIM ** -0.5
    ncache = R * nsub

    ri = lax.broadcasted_iota(jnp.int32, (rows8, ncache), 0)
    ji = lax.broadcasted_iota(jnp.int32, (rows8, ncache), 1)
    t_row = ri % dec_seq
    c_true = (ji % R) * stride + ji // R
    dist_c = wb + t_row - c_true
    ok_c = ((dist_c & (dil - 1)) == 0) & (dist_c <= window)
    bias_c_base = jnp.where(ok_c, 0.0, NEG)
    dist_c_f = dist_c.astype(F32)

    rn = lax.broadcasted_iota(jnp.int32, (rows8, rows8), 0)
    jn = lax.broadcasted_iota(jnp.int32, (rows8, rows8), 1)
    dist_n = rn % dec_seq - jn % dec_seq
    ok_n = (rn // dec_seq == jn // dec_seq) & (dist_n >= 0) & ((dist_n & (dil - 1)) == 0)
    bias_n_base = jnp.where(ok_n, 0.0, NEG)
    dist_n_f = dist_n.astype(F32)
    row_seq = lax.broadcasted_iota(jnp.int32, (rows8, HEAD_DIM), 0) // dec_seq

    for h in range(HEADS):
        cs = slice(h * HEAD_DIM, (h + 1) * HEAD_DIM)
        q = q_ref[:, cs]
        kn = k_ref[:, cs]
        vn = v_ref[:, cs]
        s_n = _dot(q, kn, precise, _NT) * scale - dist_n_f * slopes[h] + bias_n_base
        o_h = jnp.zeros((rows8, HEAD_DIM), F32)
        l_h = jnp.zeros((rows8, HEAD_DIM), F32)
        for bb in range(2):
            kc = jnp.concatenate([c_ref[bb, :, u * 2 * HEADS + h, :] for u in range(nsub)], axis=0)
            vc = jnp.concatenate([c_ref[bb, :, u * 2 * HEADS + HEADS + h, :] for u in range(nsub)], axis=0)
            s_c = _dot(q, kc, precise, _NT) * scale - dist_c_f * slopes[h] + bias_c_base
            m = jnp.maximum(jnp.max(s_c, axis=-1, keepdims=True), jnp.max(s_n, axis=-1, keepdims=True))
            p_c = jnp.exp(s_c - m)
            p_n = jnp.exp(s_n - m)
            l = jnp.sum(p_c, axis=-1, keepdims=True) + jnp.sum(p_n, axis=-1, keepdims=True)
            o = (_dot(p_c, vc, precise) + _dot(p_n, vn, precise)) / l
            mine = row_seq == bb
            o_h = jnp.where(mine, o, o_h)
            l_h = jnp.where(mine, jnp.broadcast_to(m + jnp.log(l), (rows8, HEAD_DIM)), l_h)
        o_ref[:, cs] = o_h
        l_ref[:, cs] = l_h


def _attn_sample(a_s, cache, layer, g, dec_seq, precise):
    window, dil = ATT_GROUPS[g]
    depth, B, wb = cache.shape[0], cache.shape[1], cache.shape[2]
    assert wb == window and B % 2 == 0 and dec_seq == 4
    rows8 = 2 * dec_seq
    R = min(wb, BAND)
    stride = wb // R
    nsub = min(stride, dec_seq)
    cv = cache.reshape(depth, B, R, stride * 2 * HEADS, HEAD_DIM)
    c0 = 3 * g

    def new(c):
        return pl.BlockSpec((rows8, ATT_WIDTH), lambda i: (i, c0 + c))

    out_spec = pl.BlockSpec((rows8, ATT_WIDTH), lambda i: (i, 0))
    out_sds = jax.ShapeDtypeStruct((B * dec_seq, ATT_WIDTH), F32)
    return pl.pallas_call(
        functools.partial(_attn_sample_kernel, slopes=_alibi_slopes()[g], window=window, dil=dil,
                          dec_seq=dec_seq, nsub=nsub, precise=precise),
        grid=(B // 2,),
        in_specs=[new(0), new(1), new(2),
                  pl.BlockSpec((None, 2, R, nsub * 2 * HEADS, HEAD_DIM), lambda i: (layer, i, 0, 0, 0))],
        out_specs=[out_spec, out_spec],
        out_shape=[out_sds, out_sds],
        compiler_params=_cparams(1),
        name=f"attn_sample_g{g}",
    )(a_s, a_s, a_s, cv)


def _log_decay(ga, wa, ba):
    z = jnp.dot(ga, wa, preferred_element_type=F32, precision=lax.Precision.HIGHEST) + ba
    return (jnp.minimum(z, 0.0) - jnp.log(1.0 + jnp.exp(-jnp.abs(z)))) * (1.0 / GLA_TAU)


def _gla_prompt_kernel(q_ref, k_ref, v_ref, ga_ref, wa_ref, ba_ref, o_ref, sfin_ref, st_ref, la_ref,
                      qe_ref, upd_ref, ebl_ref):
    c = pl.program_id(1)
    C = GLA_CHUNK
    per_run = C // RUNS
    nsb = C // GLA_SUB
    sub_w = GLA_SUB // RUNS

    @pl.when(c == 0)
    def _():
        st_ref[...] = jnp.zeros_like(st_ref)

    la_ref[...] = _log_decay(ga_ref[...], wa_ref[...], ba_ref[...])

    r_i = lax.broadcasted_iota(jnp.int32, (C, C), 0)
    c_i = lax.broadcasted_iota(jnp.int32, (C, C), 1)
    tok_r = RUNS * (r_i % per_run) + r_i // per_run
    tok_c = RUNS * (c_i % per_run) + c_i // per_run
    tri = (tok_c <= tok_r).astype(F32)
    diag = (tok_r // GLA_SUB == tok_c // GLA_SUB) & (tok_c <= tok_r)
    rowblk = (lax.broadcasted_iota(jnp.int32, (C, GLA_HK), 0) % per_run) // sub_w
    last_run = (RUNS - 1) * per_run

    def gather(ref, j):
        return jnp.concatenate(
            [ref[r * RUN_LEN + j * per_run: r * RUN_LEN + (j + 1) * per_run, :] for r in range(RUNS)], axis=0)

    def scatter(ref, j, val, add):
        for r in range(RUNS):
            rows = slice(r * RUN_LEN + j * per_run, r * RUN_LEN + (j + 1) * per_run)
            piece = val[r * per_run:(r + 1) * per_run, :]
            ref[rows, :] = ref[rows, :] + piece if add else piece

    def local(js):
        def ref_rows(b, i):
            end_row = last_run + (i + 1) * sub_w - 1
            return b[end_row - sub_w:end_row - sub_w + 1, :], b[end_row:end_row + 1, :]

        b = [jnp.dot(tri, gather(la_ref, j), preferred_element_type=F32, precision=lax.Precision.HIGHEST) for j in js]
        q = [gather(q_ref, j) * (GLA_HK ** -0.5) for j in js]
        k = [gather(k_ref, j) for j in js]
        v = [gather(v_ref, j).astype(BF16) for j in js]
        qd, kd, ke, bstart = [], [], [], []
        for t, j in enumerate(js):
            qe_ref[j] = (q[t] * jnp.exp(b[t])).astype(BF16)
            bs = jnp.zeros_like(b[t])
            be = jnp.zeros_like(b[t])
            for i in range(nsb):
                prev_end, own_end = ref_rows(b[t], i)
                be = jnp.where(rowblk == i, own_end, be)
                if i > 0:
                    bs = jnp.where(rowblk == i, prev_end, bs)
            bstart.append(bs)
            qd.append(q[t] * jnp.exp(b[t] - bs))
            kd.append((k[t] * jnp.exp(bs - b[t])).astype(BF16))
            ke.append(k[t] * jnp.exp(be - b[t]))
        att = [jnp.where(diag, lax.dot_general(qd[t].astype(BF16), kd[t], _NT, preferred_element_type=F32), 0.0)
               for t in range(len(js))]
        for i in range(nsb - 1):
            qj, kej = [], []
            for t in range(len(js)):
                own_end = ref_rows(b[t], i)[1]
                dj = jnp.where(rowblk > i, jnp.exp(jnp.minimum(bstart[t] - own_end, 0.0)), 0.0)
                qj.append((qd[t] * dj).astype(BF16))
                kej.append(jnp.where(rowblk == i, ke[t], 0.0).astype(BF16))
            att = [att[t] + lax.dot_general(qj[t], kej[t], _NT, preferred_element_type=F32) for t in range(len(js))]
        intra = [jnp.dot(att[t].astype(BF16), v[t], preferred_element_type=F32) for t in range(len(js))]
        kdec = [(k[t] * jnp.exp(b[t][C - 1:C, :] - b[t])).astype(BF16) for t in range(len(js))]
        upd = [lax.dot_general(v[t], kdec[t], _TN, preferred_element_type=F32) for t in range(len(js))]
        for t, j in enumerate(js):
            scatter(o_ref, j, intra[t], add=False)
            upd_ref[j] = upd[t]
            ebl_ref[j] = jnp.broadcast_to(jnp.exp(b[t][C - 1:C, :]), (SUBLANES, GLA_HK))

    def recur(j):
        st = st_ref[...]
        scatter(o_ref, j, lax.dot_general(qe_ref[j], st.astype(BF16), _NT, preferred_element_type=F32), add=True)
        st_ref[...] = st * ebl_ref[j][0:1, :] + upd_ref[j]

    n_steps = CHUNK // C
    for j0 in range(0, n_steps, GLA_GROUP):
        local(list(range(j0, j0 + GLA_GROUP)))
    for j in range(n_steps):
        recur(j)

    @pl.when(c == pl.num_programs(1) - 1)
    def _():
        sfin_ref[...] = st_ref[...].T


def _gla_prompt(b, wa_pad, ba):
    T = seq = b.shape[0]
    assert seq % CHUNK == 0
    q0 = GLA_DV // GLA_HK
    k0 = q0 + GLA_DK // GLA_HK
    a0 = k0 + GLA_DK // GLA_HK
    og, sfin = pl.pallas_call(
        _gla_prompt_kernel,
        grid=(GLA_HEADS, seq // CHUNK),
        in_specs=[pl.BlockSpec((CHUNK, GLA_HK), lambda h, n: (n, q0 + h)),
                  pl.BlockSpec((CHUNK, GLA_HK), lambda h, n: (n, k0 + h)),
                  pl.BlockSpec((CHUNK, GLA_HV), lambda h, n: (n, h)),
                  pl.BlockSpec((CHUNK, LANE), lambda h, n: (n, a0)),
                  pl.BlockSpec((LANE, GLA_HK), lambda h, n: (0, h)),
                  pl.BlockSpec((1, GLA_HK), lambda h, n: (0, h))],
        out_specs=[pl.BlockSpec((CHUNK, GLA_HV), lambda h, n: (n, h)),
                   pl.BlockSpec((None, GLA_HK, GLA_HV), lambda h, n: (h, 0, 0))],
        out_shape=[jax.ShapeDtypeStruct((T, GLA_DV), F32),
                   jax.ShapeDtypeStruct((GLA_HEADS, GLA_HK, GLA_HV), F32)],
        scratch_shapes=[pltpu.VMEM((GLA_HV, GLA_HK), F32), pltpu.VMEM((CHUNK, GLA_HK), F32),
                        pltpu.VMEM((CHUNK // GLA_CHUNK, GLA_CHUNK, GLA_HK), BF16),
                        pltpu.VMEM((CHUNK // GLA_CHUNK, GLA_HV, GLA_HK), F32),
                        pltpu.VMEM((CHUNK // GLA_CHUNK, SUBLANES, GLA_HK), F32)],
        compiler_params=_cparams(2),
        name="gla_prompt",
    )(b, b, b, b, wa_pad, ba)
    return og, sfin


def _gla_sample_kernel(q_ref, k_ref, vlo_ref, vhi_ref, ga_ref, wa_ref, ba_ref, s0_ref, o_ref, snew_ref,
                       *, dec_seq, precise):
    rows8 = 2 * dec_seq
    r_i = lax.broadcasted_iota(jnp.int32, (rows8, rows8), 0)
    c_i = lax.broadcasted_iota(jnp.int32, (rows8, rows8), 1)
    same = (r_i // dec_seq == c_i // dec_seq) & (c_i <= r_i)
    tri = same.astype(F32)
    la = _log_decay(ga_ref[...], wa_ref[...], ba_ref[...])
    b_all = jnp.dot(tri, la, preferred_element_type=F32, precision=lax.Precision.HIGHEST)
    row_seq_k = lax.broadcasted_iota(jnp.int32, (rows8, GLA_HK), 0) // dec_seq
    row_seq_v = lax.broadcasted_iota(jnp.int32, (rows8, GLA_HV), 0) // dec_seq
    for h in range(GLA_HEADS):
        ks = slice(h * GLA_HK, (h + 1) * GLA_HK)
        vs = slice(h * GLA_HV, (h + 1) * GLA_HV)
        b = b_all[:, ks]
        q = q_ref[:, ks] * (GLA_HK ** -0.5)
        k = k_ref[:, ks]
        half = GLA_HEADS // 2
        v_src = vlo_ref if h < half else vhi_ref
        v = v_src[:, (h % half) * GLA_HV:(h % half + 1) * GLA_HV]
        qd = q * jnp.exp(b)
        kd = k * jnp.exp(-b)
        att = jnp.where(same, _dot(qd, kd, precise, _NT), 0.0)
        o = _dot(att, v, precise)
        for bb in range(2):
            st = s0_ref[bb, h].T
            o = o + jnp.where(row_seq_v == bb, _dot(qd, st, precise, _NT), 0.0)
            bl = b[(bb + 1) * dec_seq - 1:(bb + 1) * dec_seq, :]
            kdec = jnp.where(row_seq_k == bb, k * jnp.exp(bl - b), 0.0)
            snew_ref[bb, h] = (st * jnp.exp(bl) + _dot(v, kdec, precise, _TN)).T
        o_ref[:, vs] = o


def _gla_sample(b_s, wa_pad, ba, state, layer, dec_seq, precise):
    B = state.shape[1]
    rows8 = 2 * dec_seq
    q0 = GLA_DV // GLA_DK
    a0 = (GLA_DV + 2 * GLA_DK) // LANE
    assert q0 == 2
    blk = lambda c: pl.BlockSpec((rows8, GLA_DK), lambda i: (i, c))
    return pl.pallas_call(
        functools.partial(_gla_sample_kernel, dec_seq=dec_seq, precise=precise),
        grid=(B // 2,),
        in_specs=[blk(q0), blk(q0 + 1), blk(0), blk(1),
                  pl.BlockSpec((rows8, LANE), lambda i: (i, a0)),
                  pl.BlockSpec((LANE, GLA_DK), lambda i: (0, 0)),
                  pl.BlockSpec((1, GLA_DK), lambda i: (0, 0)),
                  pl.BlockSpec((None, 2, GLA_HEADS, GLA_HK, GLA_HV), lambda i: (layer, i, 0, 0, 0))],
        out_specs=[pl.BlockSpec((rows8, GLA_DV), lambda i: (i, 0)),
                   pl.BlockSpec((2, GLA_HEADS, GLA_HK, GLA_HV), lambda i: (i, 0, 0, 0))],
        out_shape=[jax.ShapeDtypeStruct((B * dec_seq, GLA_DV), F32), jax.ShapeDtypeStruct(state.shape[1:], F32)],
        compiler_params=_cparams(1),
        name="gla_sample",
    )(b_s, b_s, b_s, b_s, b_s, wa_pad, ba, state)


def _post_kernel(o1, o2, o3, l1, l2, l3, og_ref, xin_ref, x_ref, *rest, alpha, precise):
    n_gate = 2 if precise else 1
    wgate_refs = rest[:n_gate]
    wpa_ref, wpb_ref, wout_ref, gn_ref, lg_ref, lb_ref, xo_ref, xb_ref = rest[n_gate:]
    wgate = tuple(r[...] for r in wgate_refs) if precise else wgate_refs[0][...]
    gates = _dot(xin_ref[...], wgate, precise)
    D = x_ref.shape[1]
    gr, gate_a, gate_b = gates[:, :GLA_DV], gates[:, GLA_DV:GLA_DV + D], gates[:, GLA_DV + D:]
    m = jnp.maximum(jnp.maximum(l1[...], l2[...]), l3[...])
    e1 = jnp.exp(l1[...] - m)
    e2 = jnp.exp(l2[...] - m)
    e3 = jnp.exp(l3[...] - m)
    att = (e1 * o1[...] + e2 * o2[...] + e3 * o3[...]) / (e1 + e2 + e3)
    ya = _dot(att, wpa_ref[...], precise)
    parts = []
    for h in range(GLA_HEADS):
        vs = slice(h * GLA_HV, (h + 1) * GLA_HV)
        og = og_ref[:, vs]
        ms = jnp.mean(og * og, axis=-1, keepdims=True)
        parts.append(_operand(og * lax.rsqrt(ms + RMS_EPS) * gn_ref[:, vs] * _silu(gr[:, vs]), precise))
    y = jnp.concatenate(parts, axis=-1)
    yb = _dot(y, wpb_ref[...], precise)
    merged = _sigmoid(gate_a) * ya + _sigmoid(gate_b) * yb
    mix = _dot(merged, wout_ref[...], precise)
    xn = _layer_norm_rows(alpha * x_ref[...] + mix, lg_ref[...], lb_ref[...])
    xo_ref[...] = xn
    xb_ref[...] = xn.astype(BF16)


def _post(os_, ls_, og, xin, x, wgate, wpa, wpb, wout, gn, lg, lb, alpha, tm):
    precise = isinstance(wgate, tuple)
    wgates = wgate if precise else (wgate,)
    T, D = x.shape
    row = lambda w: pl.BlockSpec((tm, w), lambda i: (i, 0))
    full = lambda a: pl.BlockSpec(a.shape, lambda i: (0,) * a.ndim)
    resident = lambda a: pl.BlockSpec(a.shape, lambda i: (0,) * a.ndim, pipeline_mode=pl.Buffered(1))
    return pl.pallas_call(
        functools.partial(_post_kernel, alpha=alpha, precise=precise),
        grid=(pl.cdiv(T, tm),),
        in_specs=[row(ATT_WIDTH)] * 6 + [row(GLA_DV), row(D), row(D)] + [resident(w) for w in wgates]
                 + [resident(wpa), resident(wpb), resident(wout), full(gn), full(lg), full(lb)],
        out_specs=[row(D), row(D)],
        out_shape=[jax.ShapeDtypeStruct((T, D), F32), jax.ShapeDtypeStruct((T, D), BF16)],
        compiler_params=_cparams(1),
        name="post_mixer_sample" if precise else "post_mixer",
    )(*os_, *ls_, og, xin, x, *wgates, wpa, wpb, wout, gn, lg, lb)


def _ffn_kernel(xb_ref, x_ref, wg_ref, wu_ref, wd_ref, lg_ref, lb_ref, xo_ref, xbo_ref, acc_ref, *, alpha, precise):
    j = pl.program_id(1)
    xb = xb_ref[...]
    g = _dot(xb, wg_ref[...], precise)
    u = _dot(xb, wu_ref[...], precise)
    part = _dot(_silu(g) * u, wd_ref[...], precise)

    @pl.when(j == 0)
    def _():
        acc_ref[...] = part

    @pl.when(j > 0)
    def _():
        acc_ref[...] += part

    @pl.when(j == pl.num_programs(1) - 1)
    def _():
        xn = _layer_norm_rows(alpha * x_ref[...] + acc_ref[...], lg_ref[...], lb_ref[...])
        xo_ref[...] = xn
        xbo_ref[...] = xn.astype(BF16)


def _ffn(xb, x, wgu, wd, lg, lb, alpha, tm, tc, precise=False):
    T, D = x.shape
    dff = wd.shape[0]
    assert dff % tc == 0
    nj = dff // tc
    full = lambda a: pl.BlockSpec(a.shape, lambda i, j: (0,) * a.ndim)
    wmode = dict(pipeline_mode=pl.Buffered(1)) if nj == 1 else {}
    return pl.pallas_call(
        functools.partial(_ffn_kernel, alpha=alpha, precise=precise),
        grid=(pl.cdiv(T, tm), nj),
        in_specs=[pl.BlockSpec((tm, D), lambda i, j: (i, 0)),
                  pl.BlockSpec((tm, D), lambda i, j: (i, 0)),
                  pl.BlockSpec((D, tc), lambda i, j: (0, j), **wmode),
                  pl.BlockSpec((D, tc), lambda i, j: (0, nj + j), **wmode),
                  pl.BlockSpec((tc, D), lambda i, j: (j, 0), **wmode),
                  full(lg), full(lb)],
        out_specs=[pl.BlockSpec((tm, D), lambda i, j: (i, 0))] * 2,
        out_shape=[jax.ShapeDtypeStruct((T, D), F32), jax.ShapeDtypeStruct((T, D), BF16)],
        scratch_shapes=[pltpu.VMEM((tm, D), F32)],
        compiler_params=_cparams(2),
        name="ffn_dense_sample" if precise else "ffn_dense",
    )(xb, x, wgu, wgu, wd, lg, lb)


MOE_TOKENS = 512
MOE_PIECE = 16
MOE_ROWS = -(-(2 * MOE_TOKENS + N_EXPERTS * (MOE_PIECE - 1)) // MOE_PIECE) * MOE_PIECE
MOE_BLOCK = 1024
MOE_HALF = 512
MOE_SLOTS = MOE_BLOCK // MOE_PIECE


def _tile_rows(tile_ref, tail_ref, is_tail):
    ts, d = tile_ref.shape
    pad = jnp.zeros((ts - tail_ref.shape[0], d), tail_ref.dtype)
    tail = jnp.concatenate([tail_ref[...], pad], axis=0).astype(tile_ref.dtype)
    return jnp.where(is_tail, tail, tile_ref[...])


def _moe_route_kernel(x_ref, xb_ref, xtail_ref, wrt_ref, xs_ref, meta_ref, cnt_ref, *, prompt_tiles):
    i = pl.program_id(0)
    ts = x_ref.shape[0]
    is_tail = i == prompt_tiles
    x = _tile_rows(x_ref, xtail_ref, is_tail)
    xb = _tile_rows(xb_ref, xtail_ref, is_tail)
    lt = lax.dot_general(wrt_ref[...], x, _NT, preferred_element_type=F32, precision=lax.Precision.HIGHEST)
    sub = lax.broadcasted_iota(jnp.int32, (N_EXPERTS, ts), 0).astype(F32)
    tok_ok = lax.broadcasted_iota(jnp.int32, (1, ts), 1) < jnp.where(is_tail, xtail_ref.shape[0], ts)
    m1 = jnp.max(lt, axis=0, keepdims=True)
    i1 = jnp.min(jnp.where(lt == m1, sub, float(N_EXPERTS)), axis=0, keepdims=True)
    lt2 = jnp.where(sub == i1, NEG, lt)
    m2 = jnp.max(lt2, axis=0, keepdims=True)
    i2 = jnp.min(jnp.where(lt2 == m2, sub, float(N_EXPERTS)), axis=0, keepdims=True)
    e2 = jnp.exp(m2 - m1)
    g1 = 1.0 / (1.0 + e2)
    g2 = e2 / (1.0 + e2)
    oh1 = (sub == i1) & tok_ok
    oh2 = (sub == i2) & tok_ok
    assign = jnp.where(oh1 | oh2, 1.0, 0.0)
    before = (lax.broadcasted_iota(jnp.int32, (ts, ts), 0) < lax.broadcasted_iota(jnp.int32, (ts, ts), 1))
    rank = jnp.dot(assign.astype(BF16), jnp.where(before, 1.0, 0.0).astype(BF16), preferred_element_type=F32)
    cnt = jnp.sum(assign, axis=1, keepdims=True)
    padded = jnp.floor((cnt + (MOE_PIECE - 1)) * (1.0 / MOE_PIECE)) * MOE_PIECE
    lower = (lax.broadcasted_iota(jnp.int32, (N_EXPERTS, N_EXPERTS), 1)
             < lax.broadcasted_iota(jnp.int32, (N_EXPERTS, N_EXPERTS), 0)).astype(F32)
    seg_off = jnp.dot(lower, jnp.broadcast_to(padded, (N_EXPERTS, ts)), preferred_element_type=F32,
                      precision=lax.Precision.HIGHEST)
    dest = seg_off + rank
    d1 = jnp.where(tok_ok, jnp.sum(jnp.where(oh1, dest, 0.0), axis=0, keepdims=True), -1.0)
    d2 = jnp.where(tok_ok, jnp.sum(jnp.where(oh2, dest, 0.0), axis=0, keepdims=True), -1.0)
    p_i = lax.broadcasted_iota(jnp.int32, (MOE_ROWS, ts), 0).astype(F32)
    perm = jnp.where((p_i == d1) | (p_i == d2), 1.0, 0.0).astype(BF16)
    xs_ref[...] = jnp.dot(perm, xb, preferred_element_type=F32).astype(BF16)
    sub8 = lax.broadcasted_iota(jnp.int32, (SUBLANES, ts), 0)
    meta_ref[...] = jnp.where(sub8 == 0, d1, jnp.where(sub8 == 1, d2, jnp.where(sub8 == 2, g1,
                              jnp.where(sub8 == 3, g2, 0.0))))
    cnt_ref[...] = jnp.broadcast_to(cnt, (N_EXPERTS, LANE))


def _moe_route(x, xb, x_tail, wr_t):
    seq, D = x.shape
    assert seq % MOE_TOKENS == 0 and 0 < x_tail.shape[0] <= MOE_TOKENS
    npt = seq // MOE_TOKENS
    nt = npt + 1
    tile = pl.BlockSpec((MOE_TOKENS, D), lambda i: (jnp.minimum(i, npt - 1), 0))
    return pl.pallas_call(
        functools.partial(_moe_route_kernel, prompt_tiles=npt),
        grid=(nt,),
        in_specs=[tile, tile,
                  pl.BlockSpec(x_tail.shape, lambda i: (0, 0)),
                  pl.BlockSpec(wr_t.shape, lambda i: (0, 0))],
        out_specs=[pl.BlockSpec((MOE_ROWS, D), lambda i: (i, 0)),
                   pl.BlockSpec((None, SUBLANES, MOE_TOKENS), lambda i: (i, 0, 0)),
                   pl.BlockSpec((None, N_EXPERTS, LANE), lambda i: (i, 0, 0))],
        out_shape=[jax.ShapeDtypeStruct((nt * MOE_ROWS, D), BF16),
                   jax.ShapeDtypeStruct((nt, SUBLANES, MOE_TOKENS), F32),
                   jax.ShapeDtypeStruct((nt, N_EXPERTS, LANE), F32)],
        compiler_params=_cparams(1),
        name="moe_route",
    )(x, xb, x_tail, wr_t)


def _moe_plan(cnt, n_blocks):
    nt = cnt.shape[0]
    i32 = jnp.int32
    padded = (cnt + MOE_PIECE - 1) // MOE_PIECE * MOE_PIECE
    seg_row = jnp.cumsum(padded, axis=1) - padded + (jnp.arange(nt, dtype=i32) * MOE_ROWS)[:, None]
    n_seg = (padded // MOE_PIECE).T.reshape(-1)
    seg_row = seg_row.T.reshape(-1)
    seg_end = jnp.cumsum(n_seg)
    seg_start = seg_end - n_seg
    per_e = n_seg.reshape(N_EXPERTS, nt).sum(axis=1)
    e_end = jnp.cumsum(per_e)
    e_start = e_end - per_e
    blocks_e = (per_e + MOE_SLOTS - 1) // MOE_SLOTS
    blk_end = jnp.cumsum(blocks_e)
    blk_start = blk_end - blocks_e
    w = jnp.arange(n_blocks, dtype=i32)
    ew = jnp.minimum(jnp.sum((w[:, None] >= blk_end[None, :]).astype(i32), axis=1), N_EXPERTS - 1)
    is_e = ew[:, None] == jnp.arange(N_EXPERTS, dtype=i32)[None, :]
    pick = lambda v: jnp.sum(jnp.where(is_e, v[None, :], 0), axis=1)
    first = pick(e_start) + (w - pick(blk_start)) * MOE_SLOTS
    n_w = jnp.where(w < blk_end[-1], jnp.clip(pick(e_end) - first, 0, MOE_SLOTS), 0).astype(i32)
    slot = jnp.arange(MOE_SLOTS, dtype=i32)
    p = first[:, None] + slot[None, :]
    in_seg = (p[:, :, None] >= seg_start[None, None, :]) & (p[:, :, None] < seg_end[None, None, :])
    rows = jnp.sum(jnp.where(in_seg, (seg_row - MOE_PIECE * seg_start)[None, None, :], 0), axis=2) + MOE_PIECE * p
    rows = jnp.where(slot[None, :] < n_w[:, None], rows, 0).astype(i32)
    return ew.astype(i32), n_w, rows.reshape(-1)


def _moe_expert_kernel(ew_ref, n_ref, rows_ref, xs_hbm, wg_ref, wu_ref, wd_ref, ys_in, ys_hbm,
                       lhs_ref, acc_ref, out_ref, sem_in, sem_out):
    del ew_ref, ys_in
    w = pl.program_id(0)
    j = pl.program_id(1)
    n_w = pl.num_programs(0)
    last_j = pl.num_programs(1) - 1
    n = n_ref[w]
    slot = w % 2
    halves = [(s, slice(s * MOE_HALF, (s + 1) * MOE_HALF)) for s in range(MOE_BLOCK // MOE_HALF)]

    def piece_in(blk, q, sl):
        src = pl.multiple_of(rows_ref[blk * MOE_SLOTS + q], MOE_PIECE)
        dst = pl.multiple_of(q * MOE_PIECE, MOE_PIECE)
        return pltpu.make_async_copy(xs_hbm.at[pl.ds(src, MOE_PIECE), :],
                                     lhs_ref.at[sl, pl.ds(dst, MOE_PIECE), :], sem_in.at[sl])

    def piece_out(blk, q):
        dst = pl.multiple_of(rows_ref[blk * MOE_SLOTS + q], MOE_PIECE)
        src = pl.multiple_of(q * MOE_PIECE, MOE_PIECE)
        return pltpu.make_async_copy(out_ref.at[pl.ds(src, MOE_PIECE), :], ys_hbm.at[pl.ds(dst, MOE_PIECE), :], sem_out)

    def for_pieces(blk, fn):
        lax.fori_loop(0, n_ref[blk], lambda q, c: (fn(q), c)[1], 0)

    @pl.when(j == 0)
    def _():
        @pl.when(w == 0)
        def _():
            lhs_ref[...] = jnp.zeros_like(lhs_ref)
            for_pieces(0, lambda q: piece_in(0, q, 0).start())

        for_pieces(w, lambda q: piece_in(w, q, slot).wait())

        @pl.when(w + 1 < n_w)
        def _():
            for_pieces(w + 1, lambda q: piece_in(w + 1, q, 1 - slot).start())

    for s, rows in halves:
        @pl.when(n > s * (MOE_HALF // MOE_PIECE))
        def _(rows=rows):
            xb = lhs_ref[slot, rows, :]
            g = jnp.dot(xb, wg_ref[...], preferred_element_type=F32)
            u = jnp.dot(xb, wu_ref[...], preferred_element_type=F32)
            part = jnp.dot((_silu(g) * u).astype(BF16), wd_ref[...], preferred_element_type=F32)

            @pl.when(j == 0)
            def _():
                acc_ref[rows, :] = part

            @pl.when(j > 0)
            def _():
                acc_ref[rows, :] += part

    @pl.when(j == last_j)
    def _():
        @pl.when(w > 0)
        def _():
            for_pieces(w - 1, lambda q: piece_out(w - 1, q).wait())

        for s, rows in halves:
            @pl.when(n > s * (MOE_HALF // MOE_PIECE))
            def _(rows=rows):
                out_ref[rows, :] = acc_ref[rows, :].astype(BF16)

        for_pieces(w, lambda q: piece_out(w, q).start())

        @pl.when(w == n_w - 1)
        def _():
            for_pieces(w, lambda q: piece_out(w, q).wait())


def _moe_experts(xs, plan, wgu, wd, tc):
    ew, n_w, rows = plan
    n_blocks = ew.shape[0]
    D = xs.shape[1]
    dff = wd.shape[1]
    assert dff % tc == 0
    nj = dff // tc
    any_spec = pl.BlockSpec(memory_space=pl.ANY)
    return pl.pallas_call(
        _moe_expert_kernel,
        grid_spec=pltpu.PrefetchScalarGridSpec(
            num_scalar_prefetch=3,
            grid=(n_blocks, nj),
            in_specs=[any_spec,
                      pl.BlockSpec((None, D, tc), lambda w, j, ew, n, r: (ew[w], 0, j)),
                      pl.BlockSpec((None, D, tc), lambda w, j, ew, n, r: (ew[w], 0, nj + j)),
                      pl.BlockSpec((None, tc, D), lambda w, j, ew, n, r: (ew[w], j, 0)),
                      any_spec],
            out_specs=any_spec,
            scratch_shapes=[pltpu.VMEM((2, MOE_BLOCK, D), BF16), pltpu.VMEM((MOE_BLOCK, D), F32),
                            pltpu.VMEM((MOE_BLOCK, D), BF16),
                            pltpu.SemaphoreType.DMA((2,)), pltpu.SemaphoreType.DMA(())]),
        out_shape=jax.ShapeDtypeStruct(xs.shape, BF16),
        input_output_aliases={7: 0},
        compiler_params=_cparams(2),
        name="moe_experts",
    )(ew, n_w, rows, xs, wgu, wgu, wd, jnp.zeros(xs.shape, BF16))


def _moe_combine_kernel(ys_ref, meta_ref, x_ref, xtail_ref, lg_ref, lb_ref, out_x, out_xb, out_tail,
                        *, alpha, prompt_tiles):
    i = pl.program_id(0)
    ts = x_ref.shape[0]
    meta = meta_ref[...]
    d1, d2, g1, g2 = meta[0:1, :], meta[1:2, :], meta[2:3, :], meta[3:4, :]
    p_i = lax.broadcasted_iota(jnp.int32, (MOE_ROWS, ts), 0).astype(F32)
    back = (jnp.where(p_i == d1, g1, 0.0) + jnp.where(p_i == d2, g2, 0.0)).astype(BF16)
    f = lax.dot_general(back, ys_ref[...], _TN, preferred_element_type=F32)
    x = _tile_rows(x_ref, xtail_ref, i == prompt_tiles)
    xn = _layer_norm_rows(alpha * x + f, lg_ref[...], lb_ref[...])

    @pl.when(i < prompt_tiles)
    def _():
        out_x[...] = xn
        out_xb[...] = xn.astype(BF16)

    @pl.when(i == prompt_tiles)
    def _():
        out_tail[...] = xn[:out_tail.shape[0], :]


def _moe_combine(ys, meta, x, x_tail, lg, lb, alpha):
    seq, D = x.shape
    ns = x_tail.shape[0]
    npt = seq // MOE_TOKENS
    full = lambda a: pl.BlockSpec(a.shape, lambda i: (0,) * a.ndim)
    tile = pl.BlockSpec((MOE_TOKENS, D), lambda i: (jnp.minimum(i, npt - 1), 0))
    return pl.pallas_call(
        functools.partial(_moe_combine_kernel, alpha=alpha, prompt_tiles=npt),
        grid=(npt + 1,),
        in_specs=[pl.BlockSpec((MOE_ROWS, D), lambda i: (i, 0)),
                  pl.BlockSpec((None, SUBLANES, MOE_TOKENS), lambda i: (i, 0, 0)),
                  tile, full(x_tail), full(lg), full(lb)],
        out_specs=[tile, tile, pl.BlockSpec((ns, D), lambda i: (0, 0))],
        out_shape=[jax.ShapeDtypeStruct((seq, D), F32), jax.ShapeDtypeStruct((seq, D), BF16),
                   jax.ShapeDtypeStruct((ns, D), F32)],
        compiler_params=_cparams(1),
        name="moe_combine",
    )(ys, meta, x, x_tail, lg, lb)


def _moe(x, xb, x_tail, w_router, wgu, wd, lg, lb, alpha):
    T = x.shape[0] + x_tail.shape[0]
    nt = x.shape[0] // MOE_TOKENS + 1
    xs, meta, cnt = _moe_route(x, xb, x_tail, w_router.T)
    n_pieces = (2 * T + nt * N_EXPERTS * (MOE_PIECE - 1)) // MOE_PIECE + 1
    n_blocks = pl.cdiv(n_pieces, MOE_SLOTS) + N_EXPERTS
    plan = _moe_plan(cnt[:, :, 0].astype(jnp.int32), n_blocks)
    ys = _moe_experts(xs, plan, wgu, wd, MOE_CHUNK)
    return _moe_combine(ys, meta, x, x_tail, lg, lb, alpha)


def _split_w_in(w):
    a_end = N_GROUPS * 3 * ATT_WIDTH
    gq0, gk0, gv0 = a_end, a_end + GLA_DK, a_end + 2 * GLA_DK
    gr0 = gv0 + GLA_DV
    ga0 = gr0 + GLA_DV
    gta0 = ga0 + GLA_RANK
    used = GLA_DV + 2 * GLA_DK + GLA_RANK
    ga_pad = jnp.pad(w[:, ga0:gta0], ((0, 0), (0, GLA_PANEL - used)))
    panels = (w[:, :a_end],
              jnp.concatenate([w[:, gv0:gr0], w[:, gq0:gk0], w[:, gk0:gv0], ga_pad], axis=1),
              jnp.concatenate([w[:, gr0:ga0], w[:, gta0:]], axis=1))
    return tuple(_head_tail(p) for p in panels)


def _head_tail(w):
    head = w.astype(BF16)
    return head, (w - head.astype(F32)).astype(BF16)


def _kv_rows_kernel(kp_ref, vp_ref, ks_ref, vs_ref, op_ref, os_ref, *, keep):
    n = keep // RUNS
    ns = ks_ref.shape[0]
    for kv, (p_ref, s_ref) in enumerate(((kp_ref, ks_ref), (vp_ref, vs_ref))):
        for h in range(HEADS):
            cs = slice(h * HEAD_DIM, (h + 1) * HEAD_DIM)
            sub = kv * HEADS + h
            for r in range(RUNS):
                op_ref[pl.ds(2 * HEADS * r + sub, n, stride=2 * HEADS * RUNS), :] = p_ref[r, :, cs]
            os_ref[pl.ds(sub, ns, stride=2 * HEADS), :] = s_ref[:, cs]


def _new_kv_rows(a_mat, a_s, g, db, dec_seq):
    seq = a_mat.shape[0]
    keep = min(ATT_GROUPS[g][0], seq)
    ns = db * dec_seq
    assert keep <= CHUNK and keep % RUNS == 0 and seq % CHUNK == 0 and a_s.shape[0] == ns
    last = seq // CHUNK - 1
    n = keep // RUNS
    runs = a_mat.reshape(seq // RUN_LEN, RUN_LEN, a_mat.shape[1])
    tail = lambda c: pl.BlockSpec((RUNS, n, ATT_WIDTH), lambda i: (last, RUN_LEN // n - 1, 3 * g + c))
    spec = lambda rows, rb, c: pl.BlockSpec((rows, ATT_WIDTH), lambda i: (rb, 3 * g + c))
    kv_p, kv_s = pl.pallas_call(
        functools.partial(_kv_rows_kernel, keep=keep),
        grid=(1,),
        in_specs=[tail(1), tail(2), spec(ns, 0, 1), spec(ns, 0, 2)],
        out_specs=[pl.BlockSpec((keep * 2 * HEADS, HEAD_DIM), lambda i: (0, 0)),
                   pl.BlockSpec((ns * 2 * HEADS, HEAD_DIM), lambda i: (0, 0))],
        out_shape=[jax.ShapeDtypeStruct((keep * 2 * HEADS, HEAD_DIM), F32),
                   jax.ShapeDtypeStruct((ns * 2 * HEADS, HEAD_DIM), F32)],
        compiler_params=_cparams(1),
        name=f"kv_rows_g{g}",
    )(runs, runs, a_s, a_s)
    return kv_p.reshape(1, keep, 2, HEADS, HEAD_DIM), kv_s.reshape(db, dec_seq, 2, HEADS, HEAD_DIM)


def kernel(x_prompt, x_sample, cache_kv_w128, cache_kv_w512, cache_kv_w2048, state_gla,
           w_in, w_alpha2, b_alpha, gla_norm_g, w_pa, w_pb, w_out,
           ln_mix_g, ln_mix_b, ln_ffn_g, ln_ffn_b,
           w_ffn_gu, w_ffn_down, w_router, w_exp_gu, w_exp_down):
    depth = w_in.shape[0]
    bp, seq, D = x_prompt.shape
    db, dec_seq, _ = x_sample.shape
    assert bp == 1
    ns = db * dec_seq
    alpha = (2 * depth) ** 0.25
    caches = (cache_kv_w128, cache_kv_w512, cache_kv_w2048)

    x = _interleave_rows(x_prompt.reshape(seq, D))
    xb = x.astype(BF16)
    xs = x_sample.reshape(ns, D)
    kv_p = [[] for _ in range(N_GROUPS)]
    kv_s = [[] for _ in range(N_GROUPS)]
    st_p, st_s = [], []
    row2 = lambda v: v.reshape(1, -1)

    for l in range(depth):
        wa, wb, wg = _split_w_in(w_in[l])
        a_mat = _mm(xb, wa[0], PROJ_ROWS, PROJ_COLS_ATT)
        b_mat = _mm(xb, wb[0], PROJ_ROWS, GLA_PANEL)
        a_s = _mm(xs, wa, ns, PROJ_COLS_ATT)
        b_s = _mm(xs, wb, ns, GLA_PANEL)

        os_, ls_, os_s, ls_s = [], [], [], []
        for g in range(N_GROUPS):
            o, lse = _attn_prompt(a_mat, g)
            os_.append(o)
            ls_.append(lse)
            o, lse = _attn_sample(a_s, caches[g], l, g, dec_seq, precise=True)
            os_s.append(o)
            ls_s.append(lse)
            p_rows, s_rows = _new_kv_rows(a_mat, a_s, g, db, dec_seq)
            kv_p[g].append(p_rows)
            kv_s[g].append(s_rows)

        wa_pad = jnp.pad(w_alpha2[l], ((0, LANE - GLA_RANK), (0, 0)))
        og, s_p = _gla_prompt(b_mat, wa_pad, row2(b_alpha[l]))
        og_s, s_s = _gla_sample(b_s, wa_pad, row2(b_alpha[l]), state_gla, l, dec_seq, precise=True)
        st_p.append(s_p[None])
        st_s.append(s_s)

        norms = (row2(gla_norm_g[l]), row2(ln_mix_g[l]), row2(ln_mix_b[l]))
        x, xb = _post(os_, ls_, og, xb, x, wg[0], w_pa[l].astype(BF16), w_pb[l].astype(BF16), w_out[l].astype(BF16),
                      *norms, alpha, POST_ROWS)
        xs, _ = _post(os_s, ls_s, og_s, xs, xs, wg, w_pa[l], w_pb[l], w_out[l], *norms, alpha, ns)

        ffn_norm = (row2(ln_ffn_g[l]), row2(ln_ffn_b[l]), alpha)
        if l % 2 == 0:
            x, xb = _ffn(xb, x, w_ffn_gu[l // 2].astype(BF16), w_ffn_down[l // 2].astype(BF16),
                         *ffn_norm, FFN_ROWS, FFN_CHUNK)
            xs, _ = _ffn(xs, xs, w_ffn_gu[l // 2], w_ffn_down[l // 2], *ffn_norm, ns, SAMPLE_FFN_CHUNK, precise=True)
        else:
            x, xb, xs = _moe(x, xb, xs, w_router[l // 2], w_exp_gu[l // 2].astype(BF16),
                             w_exp_down[l // 2].astype(BF16), *ffn_norm)

    return (_deinterleave_rows(x).reshape(1, seq, D), xs.reshape(db, dec_seq, D),
            jnp.stack(kv_p[0], 0), jnp.stack(kv_s[0], 0),
            jnp.stack(kv_p[1], 0), jnp.stack(kv_s[1], 0),
            jnp.stack(kv_p[2], 0), jnp.stack(kv_s[2], 0),
            jnp.stack(st_p, 0), jnp.stack(st_s, 0))
```

```python
import functools

import jax
import jax.numpy as jnp
from jax import lax
from jax.experimental import pallas as pl
from jax.experimental.pallas import tpu as pltpu

F32 = jnp.float32
BF16 = jnp.bfloat16

ATT_GROUPS = ((128, 1), (512, 4), (2048, 16))
N_GROUPS = 3
HEADS = 4
HEAD_DIM = 128
ATT_WIDTH = HEADS * HEAD_DIM
BAND = 128
ATT_GROUP = 8
GLA_HEADS = 4
GLA_HK = 128
GLA_HV = 256
GLA_DK = GLA_HEADS * GLA_HK
GLA_DV = GLA_HEADS * GLA_HV
GLA_RANK = 16
GLA_TAU = 16.0
GLA_CHUNK = 128
GLA_SUB = 32
GLA_GROUP = 16
N_EXPERTS = 8
LN_EPS = 1e-5
RMS_EPS = 1e-6
NEG = -1e30
LOG2E = 1.4426950408889634
LN2 = 0.6931471805599453
LANE = 128
SUBLANES = 8
RUNS = 16
RUN_LEN = 128
CHUNK = RUNS * RUN_LEN
VMEM_LIMIT = 56 * 1024 * 1024

PROJ_ROWS = 1024
PROJ_COLS_ATT = 1536
GLA_PANEL = 2304
POST_ROWS = 256
FFN_ROWS = 512
FFN_CHUNK = 2816
MOE_CHUNK = 1792
SAMPLE_FFN_CHUNK = 256

_NT = (((1,), (1,)), ((), ()))
_TN = (((0,), (0,)), ((), ()))


def _alibi_slopes():
    n = N_GROUPS * HEADS
    return [[2.0 ** (-8.0 * (g * HEADS + h + 1) / n) for h in range(HEADS)] for g in range(N_GROUPS)]


def _cparams(n_axes):
    return pltpu.CompilerParams(dimension_semantics=("arbitrary",) * n_axes, vmem_limit_bytes=VMEM_LIMIT)


def _layer_norm_rows(r, g, b):
    mu = jnp.mean(r, axis=-1, keepdims=True)
    c = r - mu
    var = jnp.mean(c * c, axis=-1, keepdims=True)
    return c * lax.rsqrt(var + LN_EPS) * g + b


def _sigmoid(x):
    return 1.0 / (1.0 + jnp.exp(-x))


def _silu(x):
    return x * _sigmoid(x)


def _operand(x, precise):
    return x.astype(F32 if precise else BF16)


def _dot(a, b, precise, dims=None):
    def mm(u, v):
        if dims is None:
            return jnp.dot(u, v, preferred_element_type=F32)
        return lax.dot_general(u, v, dims, preferred_element_type=F32)

    if not precise:
        return mm(a.astype(BF16), b.astype(BF16))
    a = a.astype(F32)
    a_hi = a.astype(BF16)
    a_lo = (a - a_hi.astype(F32)).astype(BF16)
    if isinstance(b, tuple):
        b_hi, b_lo = b
    else:
        b = b.astype(F32)
        b_hi = b.astype(BF16)
        b_lo = (b - b_hi.astype(F32)).astype(BF16)
    return mm(a_hi, b_hi) + (mm(a_hi, b_lo) + mm(a_lo, b_hi))


def _interleave_rows(x):
    s, d = x.shape
    return x.reshape(s // CHUNK, RUN_LEN, RUNS, d).transpose(0, 2, 1, 3).reshape(s, d)


def _deinterleave_rows(x):
    s, d = x.shape
    return x.reshape(s // CHUNK, RUNS, RUN_LEN, d).transpose(0, 2, 1, 3).reshape(s, d)


def _mm_kernel(x_ref, w_ref, o_ref):
    o_ref[...] = _dot(x_ref[...], w_ref[...], False)


def _mm_precise_kernel(x_ref, wh_ref, wl_ref, o_ref):
    o_ref[...] = _dot(x_ref[...], (wh_ref[...], wl_ref[...]), True)


def _mm(x, w, tm, tn):
    precise = isinstance(w, tuple)
    ws = w if precise else (w,)
    M, K = x.shape
    N = ws[0].shape[1]
    assert N % tn == 0
    w_spec = pl.BlockSpec((K, tn), lambda j, i: (0, j))
    return pl.pallas_call(
        _mm_precise_kernel if precise else _mm_kernel,
        grid=(N // tn, pl.cdiv(M, tm)),
        in_specs=[pl.BlockSpec((tm, K), lambda j, i: (i, 0))] + [w_spec] * len(ws),
        out_specs=pl.BlockSpec((tm, tn), lambda j, i: (i, j)),
        out_shape=jax.ShapeDtypeStruct((M, N), F32),
        compiler_params=_cparams(2),
        name="in_proj_sample" if precise else "in_proj",
    )(x, *ws)


def _attn_prompt_kernel(slope_ref, q_ref, k_ref, v_ref, o_ref, l_ref, kp_ref, vp_ref, *, dil):
    h = pl.program_id(0)
    c = pl.program_id(1)
    G = RUNS // dil
    wq = BAND // G
    scale = HEAD_DIM ** -0.5

    @pl.when(c == 0)
    def _():
        kp_ref[...] = jnp.zeros_like(kp_ref)
        vp_ref[...] = jnp.zeros_like(vp_ref)

    qi = lax.broadcasted_iota(jnp.int32, (BAND, 2 * BAND), 0)
    ci = lax.broadcasted_iota(jnp.int32, (BAND, 2 * BAND), 1)
    e = ci % (2 * wq)
    rel = (G * (qi % wq) + qi // wq) - (G * (e - wq) + ci // (2 * wq))
    valid = (rel >= 0) & (rel <= BAND)
    bias = rel.astype(F32) * (-slope_ref[h] * (dil * LOG2E))
    bias_in = jnp.where(valid, bias, NEG)
    first_e = jnp.where(c > 0, 0, wq)
    bias_first = jnp.where(valid & (e >= first_e), bias, NEG)

    def keys(cur_ref, prev_ref, runs, m):
        if m == 0:
            parts = []
            for rho in runs:
                parts += [prev_ref[(rho + 1) * RUN_LEN - wq:(rho + 1) * RUN_LEN, :],
                          cur_ref[rho * RUN_LEN: rho * RUN_LEN + wq, :]]
        else:
            parts = [cur_ref[rho * RUN_LEN + wq * (m - 1): rho * RUN_LEN + wq * (m + 1), :] for rho in runs]
        return jnp.concatenate(parts, axis=0).astype(BF16)

    blocks = [([r + dil * v for v in range(G)], m) for r in range(dil) for m in range(G)]
    for g0 in range(0, len(blocks), ATT_GROUP):
        grp = blocks[g0:g0 + ATT_GROUP]
        q = [(jnp.concatenate([q_ref[rho * RUN_LEN + wq * m: rho * RUN_LEN + wq * (m + 1), :] for rho in runs],
                              axis=0) * (scale * LOG2E)).astype(BF16) for runs, m in grp]
        s = [lax.dot_general(q[t], keys(k_ref, kp_ref, runs, m), _NT, preferred_element_type=F32)
             + (bias_first if m == 0 else bias_in) for t, (runs, m) in enumerate(grp)]
        mx = [jnp.max(x, axis=-1, keepdims=True) for x in s]
        p = [jnp.exp2(s[t] - mx[t]) for t in range(len(grp))]
        l = [jnp.sum(x, axis=-1, keepdims=True) for x in p]
        o = [jnp.dot(p[t].astype(BF16), keys(v_ref, vp_ref, runs, m), preferred_element_type=F32) / l[t]
             for t, (runs, m) in enumerate(grp)]
        for t, (runs, m) in enumerate(grp):
            lse = jnp.broadcast_to(mx[t] * LN2 + jnp.log(l[t]), (BAND, HEAD_DIM))
            for idx, rho in enumerate(runs):
                rows = slice(rho * RUN_LEN + wq * m, rho * RUN_LEN + wq * (m + 1))
                o_ref[rows, :] = o[t][idx * wq:(idx + 1) * wq, :]
                l_ref[rows, :] = lse[idx * wq:(idx + 1) * wq, :]

    kp_ref[...] = k_ref[...]
    vp_ref[...] = v_ref[...]


def _attn_prompt(a_mat, g):
    T = seq = a_mat.shape[0]
    _, dil = ATT_GROUPS[g]
    assert seq % CHUNK == 0 and RUNS % dil == 0
    slopes = jnp.asarray(_alibi_slopes()[g], F32)

    def col(c):
        return pl.BlockSpec((CHUNK, HEAD_DIM), lambda h, n: (n, (3 * g + c) * HEADS + h))

    out_spec = pl.BlockSpec((CHUNK, HEAD_DIM), lambda h, n: (n, h))
    out_sds = jax.ShapeDtypeStruct((T, ATT_WIDTH), F32)
    return pl.pallas_call(
        functools.partial(_attn_prompt_kernel, dil=dil),
        grid=(HEADS, seq // CHUNK),
        in_specs=[pl.BlockSpec(memory_space=pltpu.SMEM), col(0), col(1), col(2)],
        out_specs=[out_spec, out_spec],
        out_shape=[out_sds, out_sds],
        scratch_shapes=[pltpu.VMEM((CHUNK, HEAD_DIM), F32), pltpu.VMEM((CHUNK, HEAD_DIM), F32)],
        compiler_params=_cparams(2),
        name=f"attn_prompt_g{g}",
    )(slopes, a_mat, a_mat, a_mat)


def _attn_sample_kernel(q_ref, k_ref, v_ref, c_ref, o_ref, l_ref, *, slopes, window, dil, dec_seq, nsub, precise):
    R = c_ref.shape[1]
    wb = window
    stride = wb // R
    rows8 = 2 * dec_seq
    scale = HEAD_DIM ** -0.5
    ncache = R * nsub

    ri = lax.broadcasted_iota(jnp.int32, (rows8, ncache), 0)
    ji = lax.broadcasted_iota(jnp.int32, (rows8, ncache), 1)
    t_row = ri % dec_seq
    c_true = (ji % R) * stride + ji // R
    dist_c = wb + t_row - c_true
    ok_c = ((dist_c & (dil - 1)) == 0) & (dist_c <= window)
    bias_c_base = jnp.where(ok_c, 0.0, NEG)
    dist_c_f = dist_c.astype(F32)

    rn = lax.broadcasted_iota(jnp.int32, (rows8, rows8), 0)
    jn = lax.broadcasted_iota(jnp.int32, (rows8, rows8), 1)
    dist_n = rn % dec_seq - jn % dec_seq
    ok_n = (rn // dec_seq == jn // dec_seq) & (dist_n >= 0) & ((dist_n & (dil - 1)) == 0)
    bias_n_base = jnp.where(ok_n, 0.0, NEG)
    dist_n_f = dist_n.astype(F32)
    row_seq = lax.broadcasted_iota(jnp.int32, (rows8, HEAD_DIM), 0) // dec_seq

    for h in range(HEADS):
        cs = slice(h * HEAD_DIM, (h + 1) * HEAD_DIM)
        q = q_ref[:, cs]
        kn = k_ref[:, cs]
        vn = v_ref[:, cs]
        s_n = _dot(q, kn, precise, _NT) * scale - dist_n_f * slopes[h] + bias_n_base
        o_h = jnp.zeros((rows8, HEAD_DIM), F32)
        l_h = jnp.zeros((rows8, HEAD_DIM), F32)
        for bb in range(2):
            kc = jnp.concatenate([c_ref[bb, :, u * 2 * HEADS + h, :] for u in range(nsub)], axis=0)
            vc = jnp.concatenate([c_ref[bb, :, u * 2 * HEADS + HEADS + h, :] for u in range(nsub)], axis=0)
            s_c = _dot(q, kc, precise, _NT) * scale - dist_c_f * slopes[h] + bias_c_base
            m = jnp.maximum(jnp.max(s_c, axis=-1, keepdims=True), jnp.max(s_n, axis=-1, keepdims=True))
            p_c = jnp.exp(s_c - m)
            p_n = jnp.exp(s_n - m)
            l = jnp.sum(p_c, axis=-1, keepdims=True) + jnp.sum(p_n, axis=-1, keepdims=True)
            o = (_dot(p_c, vc, precise) + _dot(p_n, vn, precise)) / l
            mine = row_seq == bb
            o_h = jnp.where(mine, o, o_h)
            l_h = jnp.where(mine, jnp.broadcast_to(m + jnp.log(l), (rows8, HEAD_DIM)), l_h)
        o_ref[:, cs] = o_h
        l_ref[:, cs] = l_h


def _attn_sample(a_s, cache, layer, g, dec_seq, precise):
    window, dil = ATT_GROUPS[g]
    depth, B, wb = cache.shape[0], cache.shape[1], cache.shape[2]
    assert wb == window and B % 2 == 0 and dec_seq == 4
    rows8 = 2 * dec_seq
    R = min(wb, BAND)
    stride = wb // R
    nsub = min(stride, dec_seq)
    cv = cache.reshape(depth, B, R, stride * 2 * HEADS, HEAD_DIM)
    c0 = 3 * g

    def new(c):
        return pl.BlockSpec((rows8, ATT_WIDTH), lambda i: (i, c0 + c))

    out_spec = pl.BlockSpec((rows8, ATT_WIDTH), lambda i: (i, 0))
    out_sds = jax.ShapeDtypeStruct((B * dec_seq, ATT_WIDTH), F32)
    return pl.pallas_call(
        functools.partial(_attn_sample_kernel, slopes=_alibi_slopes()[g], window=window, dil=dil,
                          dec_seq=dec_seq, nsub=nsub, precise=precise),
        grid=(B // 2,),
        in_specs=[new(0), new(1), new(2),
                  pl.BlockSpec((None, 2, R, nsub * 2 * HEADS, HEAD_DIM), lambda i: (layer, i, 0, 0, 0))],
        out_specs=[out_spec, out_spec],
        out_shape=[out_sds, out_sds],
        compiler_params=_cparams(1),
        name=f"attn_sample_g{g}",
    )(a_s, a_s, a_s, cv)


def _log_decay(ga, wa, ba):
    z = jnp.dot(ga, wa, preferred_element_type=F32, precision=lax.Precision.HIGHEST) + ba
    return (jnp.minimum(z, 0.0) - jnp.log(1.0 + jnp.exp(-jnp.abs(z)))) * (1.0 / GLA_TAU)


def _gla_prompt_kernel(q_ref, k_ref, v_ref, ga_ref, wa_ref, ba_ref, o_ref, sfin_ref, st_ref, la_ref,
                      qe_ref, upd_ref, ebl_ref):
    c = pl.program_id(1)
    C = GLA_CHUNK
    per_run = C // RUNS
    nsb = C // GLA_SUB
    sub_w = GLA_SUB // RUNS

    @pl.when(c == 0)
    def _():
        st_ref[...] = jnp.zeros_like(st_ref)

    la_ref[...] = _log_decay(ga_ref[...], wa_ref[...], ba_ref[...])

    r_i = lax.broadcasted_iota(jnp.int32, (C, C), 0)
    c_i = lax.broadcasted_iota(jnp.int32, (C, C), 1)
    tok_r = RUNS * (r_i % per_run) + r_i // per_run
    tok_c = RUNS * (c_i % per_run) + c_i // per_run
    tri = (tok_c <= tok_r).astype(F32)
    diag = (tok_r // GLA_SUB == tok_c // GLA_SUB) & (tok_c <= tok_r)
    rowblk = (lax.broadcasted_iota(jnp.int32, (C, GLA_HK), 0) % per_run) // sub_w
    last_run = (RUNS - 1) * per_run

    def gather(ref, j):
        return jnp.concatenate(
            [ref[r * RUN_LEN + j * per_run: r * RUN_LEN + (j + 1) * per_run, :] for r in range(RUNS)], axis=0)

    def scatter(ref, j, val, add):
        for r in range(RUNS):
            rows = slice(r * RUN_LEN + j * per_run, r * RUN_LEN + (j + 1) * per_run)
            piece = val[r * per_run:(r + 1) * per_run, :]
            ref[rows, :] = ref[rows, :] + piece if add else piece

    def local(js):
        def ref_rows(b, i):
            end_row = last_run + (i + 1) * sub_w - 1
            return b[end_row - sub_w:end_row - sub_w + 1, :], b[end_row:end_row + 1, :]

        b = [jnp.dot(tri, gather(la_ref, j), preferred_element_type=F32, precision=lax.Precision.HIGHEST) for j in js]
        q = [gather(q_ref, j) * (GLA_HK ** -0.5) for j in js]
        k = [gather(k_ref, j) for j in js]
        v = [gather(v_ref, j).astype(BF16) for j in js]
        qd, kd, ke, bstart = [], [], [], []
        for t, j in enumerate(js):
            qe_ref[j] = (q[t] * jnp.exp(b[t])).astype(BF16)
            bs = jnp.zeros_like(b[t])
            be = jnp.zeros_like(b[t])
            for i in range(nsb):
                prev_end, own_end = ref_rows(b[t], i)
                be = jnp.where(rowblk == i, own_end, be)
                if i > 0:
                    bs = jnp.where(rowblk == i, prev_end, bs)
            bstart.append(bs)
            qd.append(q[t] * jnp.exp(b[t] - bs))
            kd.append((k[t] * jnp.exp(bs - b[t])).astype(BF16))
            ke.append(k[t] * jnp.exp(be - b[t]))
        att = [jnp.where(diag, lax.dot_general(qd[t].astype(BF16), kd[t], _NT, preferred_element_type=F32), 0.0)
               for t in range(len(js))]
        for i in range(nsb - 1):
            qj, kej = [], []
            for t in range(len(js)):
                own_end = ref_rows(b[t], i)[1]
                dj = jnp.where(rowblk > i, jnp.exp(jnp.minimum(bstart[t] - own_end, 0.0)), 0.0)
                qj.append((qd[t] * dj).astype(BF16))
                kej.append(jnp.where(rowblk == i, ke[t], 0.0).astype(BF16))
            att = [att[t] + lax.dot_general(qj[t], kej[t], _NT, preferred_element_type=F32) for t in range(len(js))]
        intra = [jnp.dot(att[t].astype(BF16), v[t], preferred_element_type=F32) for t in range(len(js))]
        kdec = [(k[t] * jnp.exp(b[t][C - 1:C, :] - b[t])).astype(BF16) for t in range(len(js))]
        upd = [lax.dot_general(v[t], kdec[t], _TN, preferred_element_type=F32) for t in range(len(js))]
        for t, j in enumerate(js):
            scatter(o_ref, j, intra[t], add=False)
            upd_ref[j] = upd[t]
            ebl_ref[j] = jnp.broadcast_to(jnp.exp(b[t][C - 1:C, :]), (SUBLANES, GLA_HK))

    def recur(j):
        st = st_ref[...]
        scatter(o_ref, j, lax.dot_general(qe_ref[j], st.astype(BF16), _NT, preferred_element_type=F32), add=True)
        st_ref[...] = st * ebl_ref[j][0:1, :] + upd_ref[j]

    n_steps = CHUNK // C
    for j0 in range(0, n_steps, GLA_GROUP):
        local(list(range(j0, j0 + GLA_GROUP)))
    for j in range(n_steps):
        recur(j)

    @pl.when(c == pl.num_programs(1) - 1)
    def _():
        sfin_ref[...] = st_ref[...].T


def _gla_prompt(b, wa_pad, ba):
    T = seq = b.shape[0]
    assert seq % CHUNK == 0
    q0 = GLA_DV // GLA_HK
    k0 = q0 + GLA_DK // GLA_HK
    a0 = k0 + GLA_DK // GLA_HK
    og, sfin = pl.pallas_call(
        _gla_prompt_kernel,
        grid=(GLA_HEADS, seq // CHUNK),
        in_specs=[pl.BlockSpec((CHUNK, GLA_HK), lambda h, n: (n, q0 + h)),
                  pl.BlockSpec((CHUNK, GLA_HK), lambda h, n: (n, k0 + h)),
                  pl.BlockSpec((CHUNK, GLA_HV), lambda h, n: (n, h)),
                  pl.BlockSpec((CHUNK, LANE), lambda h, n: (n, a0)),
                  pl.BlockSpec((LANE, GLA_HK), lambda h, n: (0, h)),
                  pl.BlockSpec((1, GLA_HK), lambda h, n: (0, h))],
        out_specs=[pl.BlockSpec((CHUNK, GLA_HV), lambda h, n: (n, h)),
                   pl.BlockSpec((None, GLA_HK, GLA_HV), lambda h, n: (h, 0, 0))],
        out_shape=[jax.ShapeDtypeStruct((T, GLA_DV), F32),
                   jax.ShapeDtypeStruct((GLA_HEADS, GLA_HK, GLA_HV), F32)],
        scratch_shapes=[pltpu.VMEM((GLA_HV, GLA_HK), F32), pltpu.VMEM((CHUNK, GLA_HK), F32),
                        pltpu.VMEM((CHUNK // GLA_CHUNK, GLA_CHUNK, GLA_HK), BF16),
                        pltpu.VMEM((CHUNK // GLA_CHUNK, GLA_HV, GLA_HK), F32),
                        pltpu.VMEM((CHUNK // GLA_CHUNK, SUBLANES, GLA_HK), F32)],
        compiler_params=_cparams(2),
        name="gla_prompt",
    )(b, b, b, b, wa_pad, ba)
    return og, sfin


def _gla_sample_kernel(q_ref, k_ref, vlo_ref, vhi_ref, ga_ref, wa_ref, ba_ref, s0_ref, o_ref, snew_ref,
                       *, dec_seq, precise):
    rows8 = 2 * dec_seq
    r_i = lax.broadcasted_iota(jnp.int32, (rows8, rows8), 0)
    c_i = lax.broadcasted_iota(jnp.int32, (rows8, rows8), 1)
    same = (r_i // dec_seq == c_i // dec_seq) & (c_i <= r_i)
    tri = same.astype(F32)
    la = _log_decay(ga_ref[...], wa_ref[...], ba_ref[...])
    b_all = jnp.dot(tri, la, preferred_element_type=F32, precision=lax.Precision.HIGHEST)
    row_seq_k = lax.broadcasted_iota(jnp.int32, (rows8, GLA_HK), 0) // dec_seq
    row_seq_v = lax.broadcasted_iota(jnp.int32, (rows8, GLA_HV), 0) // dec_seq
    for h in range(GLA_HEADS):
        ks = slice(h * GLA_HK, (h + 1) * GLA_HK)
        vs = slice(h * GLA_HV, (h + 1) * GLA_HV)
        b = b_all[:, ks]
        q = q_ref[:, ks] * (GLA_HK ** -0.5)
        k = k_ref[:, ks]
        half = GLA_HEADS // 2
        v_src = vlo_ref if h < half else vhi_ref
        v = v_src[:, (h % half) * GLA_HV:(h % half + 1) * GLA_HV]
        qd = q * jnp.exp(b)
        kd = k * jnp.exp(-b)
        att = jnp.where(same, _dot(qd, kd, precise, _NT), 0.0)
        o = _dot(att, v, precise)
        for bb in range(2):
            st = s0_ref[bb, h].T
            o = o + jnp.where(row_seq_v == bb, _dot(qd, st, precise, _NT), 0.0)
            bl = b[(bb + 1) * dec_seq - 1:(bb + 1) * dec_seq, :]
            kdec = jnp.where(row_seq_k == bb, k * jnp.exp(bl - b), 0.0)
            snew_ref[bb, h] = (st * jnp.exp(bl) + _dot(v, kdec, precise, _TN)).T
        o_ref[:, vs] = o


def _gla_sample(b_s, wa_pad, ba, state, layer, dec_seq, precise):
    B = state.shape[1]
    rows8 = 2 * dec_seq
    q0 = GLA_DV // GLA_DK
    a0 = (GLA_DV + 2 * GLA_DK) // LANE
    assert q0 == 2
    blk = lambda c: pl.BlockSpec((rows8, GLA_DK), lambda i: (i, c))
    return pl.pallas_call(
        functools.partial(_gla_sample_kernel, dec_seq=dec_seq, precise=precise),
        grid=(B // 2,),
        in_specs=[blk(q0), blk(q0 + 1), blk(0), blk(1),
                  pl.BlockSpec((rows8, LANE), lambda i: (i, a0)),
                  pl.BlockSpec((LANE, GLA_DK), lambda i: (0, 0)),
                  pl.BlockSpec((1, GLA_DK), lambda i: (0, 0)),
                  pl.BlockSpec((None, 2, GLA_HEADS, GLA_HK, GLA_HV), lambda i: (layer, i, 0, 0, 0))],
        out_specs=[pl.BlockSpec((rows8, GLA_DV), lambda i: (i, 0)),
                   pl.BlockSpec((2, GLA_HEADS, GLA_HK, GLA_HV), lambda i: (i, 0, 0, 0))],
        out_shape=[jax.ShapeDtypeStruct((B * dec_seq, GLA_DV), F32), jax.ShapeDtypeStruct(state.shape[1:], F32)],
        compiler_params=_cparams(1),
        name="gla_sample",
    )(b_s, b_s, b_s, b_s, b_s, wa_pad, ba, state)


def _post_kernel(o1, o2, o3, l1, l2, l3, og_ref, xin_ref, x_ref, *rest, alpha, precise):
    n_gate = 2 if precise else 1
    wgate_refs = rest[:n_gate]
    wpa_ref, wpb_ref, wout_ref, gn_ref, lg_ref, lb_ref, xo_ref, xb_ref = rest[n_gate:]
    wgate = tuple(r[...] for r in wgate_refs) if precise else wgate_refs[0][...]
    gates = _dot(xin_ref[...], wgate, precise)
    D = x_ref.shape[1]
    gr, gate_a, gate_b = gates[:, :GLA_DV], gates[:, GLA_DV:GLA_DV + D], gates[:, GLA_DV + D:]
    m = jnp.maximum(jnp.maximum(l1[...], l2[...]), l3[...])
    e1 = jnp.exp(l1[...] - m)
    e2 = jnp.exp(l2[...] - m)
    e3 = jnp.exp(l3[...] - m)
    att = (e1 * o1[...] + e2 * o2[...] + e3 * o3[...]) / (e1 + e2 + e3)
    ya = _dot(att, wpa_ref[...], precise)
    parts = []
    for h in range(GLA_HEADS):
        vs = slice(h * GLA_HV, (h + 1) * GLA_HV)
        og = og_ref[:, vs]
        ms = jnp.mean(og * og, axis=-1, keepdims=True)
        parts.append(_operand(og * lax.rsqrt(ms + RMS_EPS) * gn_ref[:, vs] * _silu(gr[:, vs]), precise))
    y = jnp.concatenate(parts, axis=-1)
    yb = _dot(y, wpb_ref[...], precise)
    merged = _sigmoid(gate_a) * ya + _sigmoid(gate_b) * yb
    mix = _dot(merged, wout_ref[...], precise)
    xn = _layer_norm_rows(alpha * x_ref[...] + mix, lg_ref[...], lb_ref[...])
    xo_ref[...] = xn
    xb_ref[...] = xn.astype(BF16)


def _post(os_, ls_, og, xin, x, wgate, wpa, wpb, wout, gn, lg, lb, alpha, tm):
    precise = isinstance(wgate, tuple)
    wgates = wgate if precise else (wgate,)
    T, D = x.shape
    row = lambda w: pl.BlockSpec((tm, w), lambda i: (i, 0))
    full = lambda a: pl.BlockSpec(a.shape, lambda i: (0,) * a.ndim)
    resident = lambda a: pl.BlockSpec(a.shape, lambda i: (0,) * a.ndim, pipeline_mode=pl.Buffered(1))
    return pl.pallas_call(
        functools.partial(_post_kernel, alpha=alpha, precise=precise),
        grid=(pl.cdiv(T, tm),),
        in_specs=[row(ATT_WIDTH)] * 6 + [row(GLA_DV), row(D), row(D)] + [resident(w) for w in wgates]
                 + [resident(wpa), resident(wpb), resident(wout), full(gn), full(lg), full(lb)],
        out_specs=[row(D), row(D)],
        out_shape=[jax.ShapeDtypeStruct((T, D), F32), jax.ShapeDtypeStruct((T, D), BF16)],
        compiler_params=_cparams(1),
        name="post_mixer_sample" if precise else "post_mixer",
    )(*os_, *ls_, og, xin, x, *wgates, wpa, wpb, wout, gn, lg, lb)


def _ffn_kernel(xb_ref, x_ref, wg_ref, wu_ref, wd_ref, lg_ref, lb_ref, xo_ref, xbo_ref, acc_ref, *, alpha, precise):
    j = pl.program_id(1)
    xb = xb_ref[...]
    g = _dot(xb, wg_ref[...], precise)
    u = _dot(xb, wu_ref[...], precise)
    part = _dot(_silu(g) * u, wd_ref[...], precise)

    @pl.when(j == 0)
    def _():
        acc_ref[...] = part

    @pl.when(j > 0)
    def _():
        acc_ref[...] += part

    @pl.when(j == pl.num_programs(1) - 1)
    def _():
        xn = _layer_norm_rows(alpha * x_ref[...] + acc_ref[...], lg_ref[...], lb_ref[...])
        xo_ref[...] = xn
        xbo_ref[...] = xn.astype(BF16)


def _ffn(xb, x, wgu, wd, lg, lb, alpha, tm, tc, precise=False):
    T, D = x.shape
    dff = wd.shape[0]
    assert dff % tc == 0
    nj = dff // tc
    full = lambda a: pl.BlockSpec(a.shape, lambda i, j: (0,) * a.ndim)
    wmode = dict(pipeline_mode=pl.Buffered(1)) if nj == 1 else {}
    return pl.pallas_call(
        functools.partial(_ffn_kernel, alpha=alpha, precise=precise),
        grid=(pl.cdiv(T, tm), nj),
        in_specs=[pl.BlockSpec((tm, D), lambda i, j: (i, 0)),
                  pl.BlockSpec((tm, D), lambda i, j: (i, 0)),
                  pl.BlockSpec((D, tc), lambda i, j: (0, j), **wmode),
                  pl.BlockSpec((D, tc), lambda i, j: (0, nj + j), **wmode),
                  pl.BlockSpec((tc, D), lambda i, j: (j, 0), **wmode),
                  full(lg), full(lb)],
        out_specs=[pl.BlockSpec((tm, D), lambda i, j: (i, 0))] * 2,
        out_shape=[jax.ShapeDtypeStruct((T, D), F32), jax.ShapeDtypeStruct((T, D), BF16)],
        scratch_shapes=[pltpu.VMEM((tm, D), F32)],
        compiler_params=_cparams(2),
        name="ffn_dense_sample" if precise else "ffn_dense",
    )(xb, x, wgu, wgu, wd, lg, lb)


MOE_TOKENS = 512
MOE_PIECE = 16
MOE_ROWS = -(-(2 * MOE_TOKENS + N_EXPERTS * (MOE_PIECE - 1)) // MOE_PIECE) * MOE_PIECE
MOE_BLOCK = 1024
MOE_HALF = 512
MOE_SLOTS = MOE_BLOCK // MOE_PIECE


def _tile_rows(tile_ref, tail_ref, is_tail):
    ts, d = tile_ref.shape
    pad = jnp.zeros((ts - tail_ref.shape[0], d), tail_ref.dtype)
    tail = jnp.concatenate([tail_ref[...], pad], axis=0).astype(tile_ref.dtype)
    return jnp.where(is_tail, tail, tile_ref[...])


def _moe_route_kernel(x_ref, xb_ref, xtail_ref, wrt_ref, xs_ref, meta_ref, cnt_ref, *, prompt_tiles):
    i = pl.program_id(0)
    ts = x_ref.shape[0]
    is_tail = i == prompt_tiles
    x = _tile_rows(x_ref, xtail_ref, is_tail)
    xb = _tile_rows(xb_ref, xtail_ref, is_tail)
    lt = lax.dot_general(wrt_ref[...], x, _NT, preferred_element_type=F32, precision=lax.Precision.HIGHEST)
    sub = lax.broadcasted_iota(jnp.int32, (N_EXPERTS, ts), 0).astype(F32)
    tok_ok = lax.broadcasted_iota(jnp.int32, (1, ts), 1) < jnp.where(is_tail, xtail_ref.shape[0], ts)
    m1 = jnp.max(lt, axis=0, keepdims=True)
    i1 = jnp.min(jnp.where(lt == m1, sub, float(N_EXPERTS)), axis=0, keepdims=True)
    lt2 = jnp.where(sub == i1, NEG, lt)
    m2 = jnp.max(lt2, axis=0, keepdims=True)
    i2 = jnp.min(jnp.where(lt2 == m2, sub, float(N_EXPERTS)), axis=0, keepdims=True)
    e2 = jnp.exp(m2 - m1)
    g1 = 1.0 / (1.0 + e2)
    g2 = e2 / (1.0 + e2)
    oh1 = (sub == i1) & tok_ok
    oh2 = (sub == i2) & tok_ok
    assign = jnp.where(oh1 | oh2, 1.0, 0.0)
    before = (lax.broadcasted_iota(jnp.int32, (ts, ts), 0) < lax.broadcasted_iota(jnp.int32, (ts, ts), 1))
    rank = jnp.dot(assign.astype(BF16), jnp.where(before, 1.0, 0.0).astype(BF16), preferred_element_type=F32)
    cnt = jnp.sum(assign, axis=1, keepdims=True)
    padded = jnp.floor((cnt + (MOE_PIECE - 1)) * (1.0 / MOE_PIECE)) * MOE_PIECE
    lower = (lax.broadcasted_iota(jnp.int32, (N_EXPERTS, N_EXPERTS), 1)
             < lax.broadcasted_iota(jnp.int32, (N_EXPERTS, N_EXPERTS), 0)).astype(F32)
    seg_off = jnp.dot(lower, jnp.broadcast_to(padded, (N_EXPERTS, ts)), preferred_element_type=F32,
                      precision=lax.Precision.HIGHEST)
    dest = seg_off + rank
    d1 = jnp.where(tok_ok, jnp.sum(jnp.where(oh1, dest, 0.0), axis=0, keepdims=True), -1.0)
    d2 = jnp.where(tok_ok, jnp.sum(jnp.where(oh2, dest, 0.0), axis=0, keepdims=True), -1.0)
    p_i = lax.broadcasted_iota(jnp.int32, (MOE_ROWS, ts), 0).astype(F32)
    perm = jnp.where((p_i == d1) | (p_i == d2), 1.0, 0.0).astype(BF16)
    xs_ref[...] = jnp.dot(perm, xb, preferred_element_type=F32).astype(BF16)
    sub8 = lax.broadcasted_iota(jnp.int32, (SUBLANES, ts), 0)
    meta_ref[...] = jnp.where(sub8 == 0, d1, jnp.where(sub8 == 1, d2, jnp.where(sub8 == 2, g1,
                              jnp.where(sub8 == 3, g2, 0.0))))
    cnt_ref[...] = jnp.broadcast_to(cnt, (N_EXPERTS, LANE))


def _moe_route(x, xb, x_tail, wr_t):
    seq, D = x.shape
    assert seq % MOE_TOKENS == 0 and 0 < x_tail.shape[0] <= MOE_TOKENS
    npt = seq // MOE_TOKENS
    nt = npt + 1
    tile = pl.BlockSpec((MOE_TOKENS, D), lambda i: (jnp.minimum(i, npt - 1), 0))
    return pl.pallas_call(
        functools.partial(_moe_route_kernel, prompt_tiles=npt),
        grid=(nt,),
        in_specs=[tile, tile,
                  pl.BlockSpec(x_tail.shape, lambda i: (0, 0)),
                  pl.BlockSpec(wr_t.shape, lambda i: (0, 0))],
        out_specs=[pl.BlockSpec((MOE_ROWS, D), lambda i: (i, 0)),
                   pl.BlockSpec((None, SUBLANES, MOE_TOKENS), lambda i: (i, 0, 0)),
                   pl.BlockSpec((None, N_EXPERTS, LANE), lambda i: (i, 0, 0))],
        out_shape=[jax.ShapeDtypeStruct((nt * MOE_ROWS, D), BF16),
                   jax.ShapeDtypeStruct((nt, SUBLANES, MOE_TOKENS), F32),
                   jax.ShapeDtypeStruct((nt, N_EXPERTS, LANE), F32)],
        compiler_params=_cparams(1),
        name="moe_route",
    )(x, xb, x_tail, wr_t)


def _moe_plan(cnt, n_blocks):
    nt = cnt.shape[0]
    i32 = jnp.int32
    padded = (cnt + MOE_PIECE - 1) // MOE_PIECE * MOE_PIECE
    seg_row = jnp.cumsum(padded, axis=1) - padded + (jnp.arange(nt, dtype=i32) * MOE_ROWS)[:, None]
    n_seg = (padded // MOE_PIECE).T.reshape(-1)
    seg_row = seg_row.T.reshape(-1)
    seg_end = jnp.cumsum(n_seg)
    seg_start = seg_end - n_seg
    per_e = n_seg.reshape(N_EXPERTS, nt).sum(axis=1)
    e_end = jnp.cumsum(per_e)
    e_start = e_end - per_e
    blocks_e = (per_e + MOE_SLOTS - 1) // MOE_SLOTS
    blk_end = jnp.cumsum(blocks_e)
    blk_start = blk_end - blocks_e
    w = jnp.arange(n_blocks, dtype=i32)
    ew = jnp.minimum(jnp.sum((w[:, None] >= blk_end[None, :]).astype(i32), axis=1), N_EXPERTS - 1)
    is_e = ew[:, None] == jnp.arange(N_EXPERTS, dtype=i32)[None, :]
    pick = lambda v: jnp.sum(jnp.where(is_e, v[None, :], 0), axis=1)
    first = pick(e_start) + (w - pick(blk_start)) * MOE_SLOTS
    n_w = jnp.where(w < blk_end[-1], jnp.clip(pick(e_end) - first, 0, MOE_SLOTS), 0).astype(i32)
    slot = jnp.arange(MOE_SLOTS, dtype=i32)
    p = first[:, None] + slot[None, :]
    in_seg = (p[:, :, None] >= seg_start[None, None, :]) & (p[:, :, None] < seg_end[None, None, :])
    rows = jnp.sum(jnp.where(in_seg, (seg_row - MOE_PIECE * seg_start)[None, None, :], 0), axis=2) + MOE_PIECE * p
    rows = jnp.where(slot[None, :] < n_w[:, None], rows, 0).astype(i32)
    return ew.astype(i32), n_w, rows.reshape(-1)


def _moe_expert_kernel(ew_ref, n_ref, rows_ref, xs_hbm, wg_ref, wu_ref, wd_ref, ys_in, ys_hbm,
                       lhs_ref, acc_ref, out_ref, sem_in, sem_out):
    del ew_ref, ys_in
    w = pl.program_id(0)
    j = pl.program_id(1)
    n_w = pl.num_programs(0)
    last_j = pl.num_programs(1) - 1
    n = n_ref[w]
    slot = w % 2
    halves = [(s, slice(s * MOE_HALF, (s + 1) * MOE_HALF)) for s in range(MOE_BLOCK // MOE_HALF)]

    def piece_in(blk, q, sl):
        src = pl.multiple_of(rows_ref[blk * MOE_SLOTS + q], MOE_PIECE)
        dst = pl.multiple_of(q * MOE_PIECE, MOE_PIECE)
        return pltpu.make_async_copy(xs_hbm.at[pl.ds(src, MOE_PIECE), :],
                                     lhs_ref.at[sl, pl.ds(dst, MOE_PIECE), :], sem_in.at[sl])

    def piece_out(blk, q):
        dst = pl.multiple_of(rows_ref[blk * MOE_SLOTS + q], MOE_PIECE)
        src = pl.multiple_of(q * MOE_PIECE, MOE_PIECE)
        return pltpu.make_async_copy(out_ref.at[pl.ds(src, MOE_PIECE), :], ys_hbm.at[pl.ds(dst, MOE_PIECE), :], sem_out)

    def for_pieces(blk, fn):
        lax.fori_loop(0, n_ref[blk], lambda q, c: (fn(q), c)[1], 0)

    @pl.when(j == 0)
    def _():
        @pl.when(w == 0)
        def _():
            lhs_ref[...] = jnp.zeros_like(lhs_ref)
            for_pieces(0, lambda q: piece_in(0, q, 0).start())

        for_pieces(w, lambda q: piece_in(w, q, slot).wait())

        @pl.when(w + 1 < n_w)
        def _():
            for_pieces(w + 1, lambda q: piece_in(w + 1, q, 1 - slot).start())

    for s, rows in halves:
        @pl.when(n > s * (MOE_HALF // MOE_PIECE))
        def _(rows=rows):
            xb = lhs_ref[slot, rows, :]
            g = jnp.dot(xb, wg_ref[...], preferred_element_type=F32)
            u = jnp.dot(xb, wu_ref[...], preferred_element_type=F32)
            part = jnp.dot((_silu(g) * u).astype(BF16), wd_ref[...], preferred_element_type=F32)

            @pl.when(j == 0)
            def _():
                acc_ref[rows, :] = part

            @pl.when(j > 0)
            def _():
                acc_ref[rows, :] += part

    @pl.when(j == last_j)
    def _():
        @pl.when(w > 0)
        def _():
            for_pieces(w - 1, lambda q: piece_out(w - 1, q).wait())

        for s, rows in halves:
            @pl.when(n > s * (MOE_HALF // MOE_PIECE))
            def _(rows=rows):
                out_ref[rows, :] = acc_ref[rows, :].astype(BF16)

        for_pieces(w, lambda q: piece_out(w, q).start())

        @pl.when(w == n_w - 1)
        def _():
            for_pieces(w, lambda q: piece_out(w, q).wait())


def _moe_experts(xs, plan, wgu, wd, tc):
    ew, n_w, rows = plan
    n_blocks = ew.shape[0]
    D = xs.shape[1]
    dff = wd.shape[1]
    assert dff % tc == 0
    nj = dff // tc
    any_spec = pl.BlockSpec(memory_space=pl.ANY)
    return pl.pallas_call(
        _moe_expert_kernel,
        grid_spec=pltpu.PrefetchScalarGridSpec(
            num_scalar_prefetch=3,
            grid=(n_blocks, nj),
            in_specs=[any_spec,
                      pl.BlockSpec((None, D, tc), lambda w, j, ew, n, r: (ew[w], 0, j)),
                      pl.BlockSpec((None, D, tc), lambda w, j, ew, n, r: (ew[w], 0, nj + j)),
                      pl.BlockSpec((None, tc, D), lambda w, j, ew, n, r: (ew[w], j, 0)),
                      any_spec],
            out_specs=any_spec,
            scratch_shapes=[pltpu.VMEM((2, MOE_BLOCK, D), BF16), pltpu.VMEM((MOE_BLOCK, D), F32),
                            pltpu.VMEM((MOE_BLOCK, D), BF16),
                            pltpu.SemaphoreType.DMA((2,)), pltpu.SemaphoreType.DMA(())]),
        out_shape=jax.ShapeDtypeStruct(xs.shape, BF16),
        input_output_aliases={7: 0},
        compiler_params=_cparams(2),
        name="moe_experts",
    )(ew, n_w, rows, xs, wgu, wgu, wd, jnp.zeros(xs.shape, BF16))


def _moe_combine_kernel(ys_ref, meta_ref, x_ref, xtail_ref, lg_ref, lb_ref, out_x, out_xb, out_tail,
                        *, alpha, prompt_tiles):
    i = pl.program_id(0)
    ts = x_ref.shape[0]
    meta = meta_ref[...]
    d1, d2, g1, g2 = meta[0:1, :], meta[1:2, :], meta[2:3, :], meta[3:4, :]
    p_i = lax.broadcasted_iota(jnp.int32, (MOE_ROWS, ts), 0).astype(F32)
    back = (jnp.where(p_i == d1, g1, 0.0) + jnp.where(p_i == d2, g2, 0.0)).astype(BF16)
    f = lax.dot_general(back, ys_ref[...], _TN, preferred_element_type=F32)
    x = _tile_rows(x_ref, xtail_ref, i == prompt_tiles)
    xn = _layer_norm_rows(alpha * x + f, lg_ref[...], lb_ref[...])

    @pl.when(i < prompt_tiles)
    def _():
        out_x[...] = xn
        out_xb[...] = xn.astype(BF16)

    @pl.when(i == prompt_tiles)
    def _():
        out_tail[...] = xn[:out_tail.shape[0], :]


def _moe_combine(ys, meta, x, x_tail, lg, lb, alpha):
    seq, D = x.shape
    ns = x_tail.shape[0]
    npt = seq // MOE_TOKENS
    full = lambda a: pl.BlockSpec(a.shape, lambda i: (0,) * a.ndim)
    tile = pl.BlockSpec((MOE_TOKENS, D), lambda i: (jnp.minimum(i, npt - 1), 0))
    return pl.pallas_call(
        functools.partial(_moe_combine_kernel, alpha=alpha, prompt_tiles=npt),
        grid=(npt + 1,),
        in_specs=[pl.BlockSpec((MOE_ROWS, D), lambda i: (i, 0)),
                  pl.BlockSpec((None, SUBLANES, MOE_TOKENS), lambda i: (i, 0, 0)),
                  tile, full(x_tail), full(lg), full(lb)],
        out_specs=[tile, tile, pl.BlockSpec((ns, D), lambda i: (0, 0))],
        out_shape=[jax.ShapeDtypeStruct((seq, D), F32), jax.ShapeDtypeStruct((seq, D), BF16),
                   jax.ShapeDtypeStruct((ns, D), F32)],
        compiler_params=_cparams(1),
        name="moe_combine",
    )(ys, meta, x, x_tail, lg, lb)


def _moe(x, xb, x_tail, w_router, wgu, wd, lg, lb, alpha):
    T = x.shape[0] + x_tail.shape[0]
    nt = x.shape[0] // MOE_TOKENS + 1
    xs, meta, cnt = _moe_route(x, xb, x_tail, w_router.T)
    n_pieces = (2 * T + nt * N_EXPERTS * (MOE_PIECE - 1)) // MOE_PIECE + 1
    n_blocks = pl.cdiv(n_pieces, MOE_SLOTS) + N_EXPERTS
    plan = _moe_plan(cnt[:, :, 0].astype(jnp.int32), n_blocks)
    ys = _moe_experts(xs, plan, wgu, wd, MOE_CHUNK)
    return _moe_combine(ys, meta, x, x_tail, lg, lb, alpha)


def _split_w_in(w):
    a_end = N_GROUPS * 3 * ATT_WIDTH
    gq0, gk0, gv0 = a_end, a_end + GLA_DK, a_end + 2 * GLA_DK
    gr0 = gv0 + GLA_DV
    ga0 = gr0 + GLA_DV
    gta0 = ga0 + GLA_RANK
    used = GLA_DV + 2 * GLA_DK + GLA_RANK
    ga_pad = jnp.pad(w[:, ga0:gta0], ((0, 0), (0, GLA_PANEL - used)))
    panels = (w[:, :a_end],
              jnp.concatenate([w[:, gv0:gr0], w[:, gq0:gk0], w[:, gk0:gv0], ga_pad], axis=1),
              jnp.concatenate([w[:, gr0:ga0], w[:, gta0:]], axis=1))
    return tuple(_head_tail(p) for p in panels)


def _head_tail(w):
    head = w.astype(BF16)
    return head, (w - head.astype(F32)).astype(BF16)


def _kv_rows_kernel(kp_ref, vp_ref, ks_ref, vs_ref, op_ref, os_ref, *, keep):
    n = keep // RUNS
    ns = ks_ref.shape[0]
    for kv, (p_ref, s_ref) in enumerate(((kp_ref, ks_ref), (vp_ref, vs_ref))):
        for h in range(HEADS):
            cs = slice(h * HEAD_DIM, (h + 1) * HEAD_DIM)
            sub = kv * HEADS + h
            for r in range(RUNS):
                op_ref[pl.ds(2 * HEADS * r + sub, n, stride=2 * HEADS * RUNS), :] = p_ref[r, :, cs]
            os_ref[pl.ds(sub, ns, stride=2 * HEADS), :] = s_ref[:, cs]


def _new_kv_rows(a_mat, a_s, g, db, dec_seq):
    seq = a_mat.shape[0]
    keep = min(ATT_GROUPS[g][0], seq)
    ns = db * dec_seq
    assert keep <= CHUNK and keep % RUNS == 0 and seq % CHUNK == 0 and a_s.shape[0] == ns
    last = seq // CHUNK - 1
    n = keep // RUNS
    runs = a_mat.reshape(seq // RUN_LEN, RUN_LEN, a_mat.shape[1])
    tail = lambda c: pl.BlockSpec((RUNS, n, ATT_WIDTH), lambda i: (last, RUN_LEN // n - 1, 3 * g + c))
    spec = lambda rows, rb, c: pl.BlockSpec((rows, ATT_WIDTH), lambda i: (rb, 3 * g + c))
    kv_p, kv_s = pl.pallas_call(
        functools.partial(_kv_rows_kernel, keep=keep),
        grid=(1,),
        in_specs=[tail(1), tail(2), spec(ns, 0, 1), spec(ns, 0, 2)],
        out_specs=[pl.BlockSpec((keep * 2 * HEADS, HEAD_DIM), lambda i: (0, 0)),
                   pl.BlockSpec((ns * 2 * HEADS, HEAD_DIM), lambda i: (0, 0))],
        out_shape=[jax.ShapeDtypeStruct((keep * 2 * HEADS, HEAD_DIM), F32),
                   jax.ShapeDtypeStruct((ns * 2 * HEADS, HEAD_DIM), F32)],
        compiler_params=_cparams(1),
        name=f"kv_rows_g{g}",
    )(runs, runs, a_s, a_s)
    return kv_p.reshape(1, keep, 2, HEADS, HEAD_DIM), kv_s.reshape(db, dec_seq, 2, HEADS, HEAD_DIM)


def kernel(x_prompt, x_sample, cache_kv_w128, cache_kv_w512, cache_kv_w2048, state_gla,
           w_in, w_alpha2, b_alpha, gla_norm_g, w_pa, w_pb, w_out,
           ln_mix_g, ln_mix_b, ln_ffn_g, ln_ffn_b,
           w_ffn_gu, w_ffn_down, w_router, w_exp_gu, w_exp_down):
    depth = w_in.shape[0]
    bp, seq, D = x_prompt.shape
    db, dec_seq, _ = x_sample.shape
    assert bp == 1
    ns = db * dec_seq
    alpha = (2 * depth) ** 0.25
    caches = (cache_kv_w128, cache_kv_w512, cache_kv_w2048)

    x = _interleave_rows(x_prompt.reshape(seq, D))
    xb = x.astype(BF16)
    xs = x_sample.reshape(ns, D)
    kv_p = [[] for _ in range(N_GROUPS)]
    kv_s = [[] for _ in range(N_GROUPS)]
    st_p, st_s = [], []
    row2 = lambda v: v.reshape(1, -1)

    for l in range(depth):
        wa, wb, wg = _split_w_in(w_in[l])
        a_mat = _mm(xb, wa[0], PROJ_ROWS, PROJ_COLS_ATT)
        b_mat = _mm(xb, wb[0], PROJ_ROWS, GLA_PANEL)
        a_s = _mm(xs, wa, ns, PROJ_COLS_ATT)
        b_s = _mm(xs, wb, ns, GLA_PANEL)

        os_, ls_, os_s, ls_s = [], [], [], []
        for g in range(N_GROUPS):
            o, lse = _attn_prompt(a_mat, g)
            os_.append(o)
            ls_.append(lse)
            o, lse = _attn_sample(a_s, caches[g], l, g, dec_seq, precise=True)
            os_s.append(o)
            ls_s.append(lse)
            p_rows, s_rows = _new_kv_rows(a_mat, a_s, g, db, dec_seq)
            kv_p[g].append(p_rows)
            kv_s[g].append(s_rows)

        wa_pad = jnp.pad(w_alpha2[l], ((0, LANE - GLA_RANK), (0, 0)))
        og, s_p = _gla_prompt(b_mat, wa_pad, row2(b_alpha[l]))
        og_s, s_s = _gla_sample(b_s, wa_pad, row2(b_alpha[l]), state_gla, l, dec_seq, precise=True)
        st_p.append(s_p[None])
        st_s.append(s_s)

        norms = (row2(gla_norm_g[l]), row2(ln_mix_g[l]), row2(ln_mix_b[l]))
        x, xb = _post(os_, ls_, og, xb, x, wg[0], w_pa[l].astype(BF16), w_pb[l].astype(BF16), w_out[l].astype(BF16),
                      *norms, alpha, POST_ROWS)
        xs, _ = _post(os_s, ls_s, og_s, xs, xs, wg, w_pa[l], w_pb[l], w_out[l], *norms, alpha, ns)

        ffn_norm = (row2(ln_ffn_g[l]), row2(ln_ffn_b[l]), alpha)
        if l % 2 == 0:
            x, xb = _ffn(xb, x, w_ffn_gu[l // 2].astype(BF16), w_ffn_down[l // 2].astype(BF16),
                         *ffn_norm, FFN_ROWS, FFN_CHUNK)
            xs, _ = _ffn(xs, xs, w_ffn_gu[l // 2], w_ffn_down[l // 2], *ffn_norm, ns, SAMPLE_FFN_CHUNK, precise=True)
        else:
            x, xb, xs = _moe(x, xb, xs, w_router[l // 2], w_exp_gu[l // 2].astype(BF16),
                             w_exp_down[l // 2].astype(BF16), *ffn_norm)

    return (_deinterleave_rows(x).reshape(1, seq, D), xs.reshape(db, dec_seq, D),
            jnp.stack(kv_p[0], 0), jnp.stack(kv_s[0], 0),
            jnp.stack(kv_p[1], 0), jnp.stack(kv_s[1], 0),
            jnp.stack(kv_p[2], 0), jnp.stack(kv_s[2], 0),
            jnp.stack(st_p, 0), jnp.stack(st_s, 0))
```

```python
import functools

import jax
import jax.numpy as jnp
from jax import lax
from jax.experimental import pallas as pl
from jax.experimental.pallas import tpu as pltpu

F32 = jnp.float32
BF16 = jnp.bfloat16

ATT_GROUPS = ((128, 1), (512, 4), (2048, 16))
N_GROUPS = 3
HEADS = 4
HEAD_DIM = 128
ATT_WIDTH = HEADS * HEAD_DIM
BAND = 128
ATT_GROUP = 16
GLA_HEADS = 4
GLA_HK = 128
GLA_HV = 256
GLA_DK = GLA_HEADS * GLA_HK
GLA_DV = GLA_HEADS * GLA_HV
GLA_RANK = 16
GLA_TAU = 16.0
GLA_CHUNK = 128
GLA_SUB = 32
GLA_GROUP = 16
N_EXPERTS = 8
LN_EPS = 1e-5
RMS_EPS = 1e-6
NEG = -1e30
LOG2E = 1.4426950408889634
LN2 = 0.6931471805599453
LANE = 128
SUBLANES = 8
RUNS = 16
RUN_LEN = 128
CHUNK = RUNS * RUN_LEN
VMEM_LIMIT = 56 * 1024 * 1024

PROJ_ROWS = 1024
PROJ_COLS_ATT = 1536
GLA_PANEL = 2304
POST_ROWS = 256
FFN_ROWS = 512
FFN_CHUNK = 2816
MOE_CHUNK = 1792
SAMPLE_FFN_CHUNK = 256

_NT = (((1,), (1,)), ((), ()))
_TN = (((0,), (0,)), ((), ()))


def _alibi_slopes():
    n = N_GROUPS * HEADS
    return [[2.0 ** (-8.0 * (g * HEADS + h + 1) / n) for h in range(HEADS)] for g in range(N_GROUPS)]


def _cparams(n_axes):
    return pltpu.CompilerParams(dimension_semantics=("arbitrary",) * n_axes, vmem_limit_bytes=VMEM_LIMIT)


def _layer_norm_rows(r, g, b):
    mu = jnp.mean(r, axis=-1, keepdims=True)
    c = r - mu
    var = jnp.mean(c * c, axis=-1, keepdims=True)
    return c * lax.rsqrt(var + LN_EPS) * g + b


def _sigmoid(x):
    return 1.0 / (1.0 + jnp.exp(-x))


def _silu(x):
    return x * _sigmoid(x)


def _operand(x, precise):
    return x.astype(F32 if precise else BF16)


def _dot(a, b, precise, dims=None):
    def mm(u, v):
        if dims is None:
            return jnp.dot(u, v, preferred_element_type=F32)
        return lax.dot_general(u, v, dims, preferred_element_type=F32)

    if not precise:
        return mm(a.astype(BF16), b.astype(BF16))
    a = a.astype(F32)
    a_hi = a.astype(BF16)
    a_lo = (a - a_hi.astype(F32)).astype(BF16)
    if isinstance(b, tuple):
        b_hi, b_lo = b
    else:
        b = b.astype(F32)
        b_hi = b.astype(BF16)
        b_lo = (b - b_hi.astype(F32)).astype(BF16)
    return mm(a_hi, b_hi) + (mm(a_hi, b_lo) + mm(a_lo, b_hi))


def _interleave_rows(x):
    s, d = x.shape
    return x.reshape(s // CHUNK, RUN_LEN, RUNS, d).transpose(0, 2, 1, 3).reshape(s, d)


def _deinterleave_rows(x):
    s, d = x.shape
    return x.reshape(s // CHUNK, RUNS, RUN_LEN, d).transpose(0, 2, 1, 3).reshape(s, d)


def _mm_kernel(x_ref, w_ref, o_ref):
    o_ref[...] = _dot(x_ref[...], w_ref[...], False)


def _mm_precise_kernel(x_ref, wh_ref, wl_ref, o_ref):
    o_ref[...] = _dot(x_ref[...], (wh_ref[...], wl_ref[...]), True)


def _mm(x, w, tm, tn):
    precise = isinstance(w, tuple)
    ws = w if precise else (w,)
    M, K = x.shape
    N = ws[0].shape[1]
    assert N % tn == 0
    w_spec = pl.BlockSpec((K, tn), lambda j, i: (0, j))
    return pl.pallas_call(
        _mm_precise_kernel if precise else _mm_kernel,
        grid=(N // tn, pl.cdiv(M, tm)),
        in_specs=[pl.BlockSpec((tm, K), lambda j, i: (i, 0))] + [w_spec] * len(ws),
        out_specs=pl.BlockSpec((tm, tn), lambda j, i: (i, j)),
        out_shape=jax.ShapeDtypeStruct((M, N), F32),
        compiler_params=_cparams(2),
        name="in_proj_sample" if precise else "in_proj",
    )(x, *ws)


def _attn_prompt_kernel(slope_ref, q_ref, k_ref, v_ref, o_ref, l_ref, kp_ref, vp_ref, *, dil):
    h = pl.program_id(0)
    c = pl.program_id(1)
    G = RUNS // dil
    wq = BAND // G
    scale = HEAD_DIM ** -0.5

    @pl.when(c == 0)
    def _():
        kp_ref[...] = jnp.zeros_like(kp_ref)
        vp_ref[...] = jnp.zeros_like(vp_ref)

    qi = lax.broadcasted_iota(jnp.int32, (BAND, 2 * BAND), 0)
    ci = lax.broadcasted_iota(jnp.int32, (BAND, 2 * BAND), 1)
    e = ci % (2 * wq)
    rel = (G * (qi % wq) + qi // wq) - (G * (e - wq) + ci // (2 * wq))
    valid = (rel >= 0) & (rel <= BAND)
    bias = rel.astype(F32) * (-slope_ref[h] * (dil * LOG2E))
    bias_in = jnp.where(valid, bias, NEG)
    first_e = jnp.where(c > 0, 0, wq)
    bias_first = jnp.where(valid & (e >= first_e), bias, NEG)

    def keys(cur_ref, prev_ref, runs, m):
        if m == 0:
            parts = []
            for rho in runs:
                parts += [prev_ref[(rho + 1) * RUN_LEN - wq:(rho + 1) * RUN_LEN, :],
                          cur_ref[rho * RUN_LEN: rho * RUN_LEN + wq, :]]
        else:
            parts = [cur_ref[rho * RUN_LEN + wq * (m - 1): rho * RUN_LEN + wq * (m + 1), :] for rho in runs]
        return jnp.concatenate(parts, axis=0).astype(BF16)

    blocks = [([r + dil * v for v in range(G)], m) for r in range(dil) for m in range(G)]
    for g0 in range(0, len(blocks), ATT_GROUP):
        grp = blocks[g0:g0 + ATT_GROUP]
        q = [(jnp.concatenate([q_ref[rho * RUN_LEN + wq * m: rho * RUN_LEN + wq * (m + 1), :] for rho in runs],
                              axis=0) * (scale * LOG2E)).astype(BF16) for runs, m in grp]
        s = [lax.dot_general(q[t], keys(k_ref, kp_ref, runs, m), _NT, preferred_element_type=F32)
             + (bias_first if m == 0 else bias_in) for t, (runs, m) in enumerate(grp)]
        mx = [jnp.max(x, axis=-1, keepdims=True) for x in s]
        p = [jnp.exp2(s[t] - mx[t]) for t in range(len(grp))]
        l = [jnp.sum(x, axis=-1, keepdims=True) for x in p]
        o = [jnp.dot(p[t].astype(BF16), keys(v_ref, vp_ref, runs, m), preferred_element_type=F32) / l[t]
             for t, (runs, m) in enumerate(grp)]
        for t, (runs, m) in enumerate(grp):
            lse = jnp.broadcast_to(mx[t] * LN2 + jnp.log(l[t]), (BAND, HEAD_DIM))
            for idx, rho in enumerate(runs):
                rows = slice(rho * RUN_LEN + wq * m, rho * RUN_LEN + wq * (m + 1))
                o_ref[rows, :] = o[t][idx * wq:(idx + 1) * wq, :]
                l_ref[rows, :] = lse[idx * wq:(idx + 1) * wq, :]

    kp_ref[...] = k_ref[...]
    vp_ref[...] = v_ref[...]


def _attn_prompt(a_mat, g):
    T = seq = a_mat.shape[0]
    _, dil = ATT_GROUPS[g]
    assert seq % CHUNK == 0 and RUNS % dil == 0
    slopes = jnp.asarray(_alibi_slopes()[g], F32)

    def col(c):
        return pl.BlockSpec((CHUNK, HEAD_DIM), lambda h, n: (n, (3 * g + c) * HEADS + h))

    out_spec = pl.BlockSpec((CHUNK, HEAD_DIM), lambda h, n: (n, h))
    out_sds = jax.ShapeDtypeStruct((T, ATT_WIDTH), F32)
    return pl.pallas_call(
        functools.partial(_attn_prompt_kernel, dil=dil),
        grid=(HEADS, seq // CHUNK),
        in_specs=[pl.BlockSpec(memory_space=pltpu.SMEM), col(0), col(1), col(2)],
        out_specs=[out_spec, out_spec],
        out_shape=[out_sds, out_sds],
        scratch_shapes=[pltpu.VMEM((CHUNK, HEAD_DIM), F32), pltpu.VMEM((CHUNK, HEAD_DIM), F32)],
        compiler_params=_cparams(2),
        name=f"attn_prompt_g{g}",
    )(slopes, a_mat, a_mat, a_mat)


def _attn_sample_kernel(q_ref, k_ref, v_ref, c_ref, o_ref, l_ref, *, slopes, window, dil, dec_seq, nsub, precise):
    R = c_ref.shape[1]
    wb = window
    stride = wb // R
    rows8 = 2 * dec_seq
    scale = HEAD_DIM ** -0.5
    ncache = R * nsub

    ri = lax.broadcasted_iota(jnp.int32, (rows8, ncache), 0)
    ji = lax.broadcasted_iota(jnp.int32, (rows8, ncache), 1)
    t_row = ri % dec_seq
    c_true = (ji % R) * stride + ji // R
    dist_c = wb + t_row - c_true
    ok_c = ((dist_c & (dil - 1)) == 0) & (dist_c <= window)
    bias_c_base = jnp.where(ok_c, 0.0, NEG)
    dist_c_f = dist_c.astype(F32)

    rn = lax.broadcasted_iota(jnp.int32, (rows8, rows8), 0)
    jn = lax.broadcasted_iota(jnp.int32, (rows8, rows8), 1)
    dist_n = rn % dec_seq - jn % dec_seq
    ok_n = (rn // dec_seq == jn // dec_seq) & (dist_n >= 0) & ((dist_n & (dil - 1)) == 0)
    bias_n_base = jnp.where(ok_n, 0.0, NEG)
    dist_n_f = dist_n.astype(F32)
    row_seq = lax.broadcasted_iota(jnp.int32, (rows8, HEAD_DIM), 0) // dec_seq

    for h in range(HEADS):
        cs = slice(h * HEAD_DIM, (h + 1) * HEAD_DIM)
        q = q_ref[:, cs]
        kn = k_ref[:, cs]
        vn = v_ref[:, cs]
        s_n = _dot(q, kn, precise, _NT) * scale - dist_n_f * slopes[h] + bias_n_base
        o_h = jnp.zeros((rows8, HEAD_DIM), F32)
        l_h = jnp.zeros((rows8, HEAD_DIM), F32)
        for bb in range(2):
            kc = jnp.concatenate([c_ref[bb, :, u * 2 * HEADS + h, :] for u in range(nsub)], axis=0)
            vc = jnp.concatenate([c_ref[bb, :, u * 2 * HEADS + HEADS + h, :] for u in range(nsub)], axis=0)
            s_c = _dot(q, kc, precise, _NT) * scale - dist_c_f * slopes[h] + bias_c_base
            m = jnp.maximum(jnp.max(s_c, axis=-1, keepdims=True), jnp.max(s_n, axis=-1, keepdims=True))
            p_c = jnp.exp(s_c - m)
            p_n = jnp.exp(s_n - m)
            l = jnp.sum(p_c, axis=-1, keepdims=True) + jnp.sum(p_n, axis=-1, keepdims=True)
            o = (_dot(p_c, vc, precise) + _dot(p_n, vn, precise)) / l
            mine = row_seq == bb
            o_h = jnp.where(mine, o, o_h)
            l_h = jnp.where(mine, jnp.broadcast_to(m + jnp.log(l), (rows8, HEAD_DIM)), l_h)
        o_ref[:, cs] = o_h
        l_ref[:, cs] = l_h


def _attn_sample(a_s, cache, layer, g, dec_seq, precise):
    window, dil = ATT_GROUPS[g]
    depth, B, wb = cache.shape[0], cache.shape[1], cache.shape[2]
    assert wb == window and B % 2 == 0 and dec_seq == 4
    rows8 = 2 * dec_seq
    R = min(wb, BAND)
    stride = wb // R
    nsub = min(stride, dec_seq)
    cv = cache.reshape(depth, B, R, stride * 2 * HEADS, HEAD_DIM)
    c0 = 3 * g

    def new(c):
        return pl.BlockSpec((rows8, ATT_WIDTH), lambda i: (i, c0 + c))

    out_spec = pl.BlockSpec((rows8, ATT_WIDTH), lambda i: (i, 0))
    out_sds = jax.ShapeDtypeStruct((B * dec_seq, ATT_WIDTH), F32)
    return pl.pallas_call(
        functools.partial(_attn_sample_kernel, slopes=_alibi_slopes()[g], window=window, dil=dil,
                          dec_seq=dec_seq, nsub=nsub, precise=precise),
        grid=(B // 2,),
        in_specs=[new(0), new(1), new(2),
                  pl.BlockSpec((None, 2, R, nsub * 2 * HEADS, HEAD_DIM), lambda i: (layer, i, 0, 0, 0))],
        out_specs=[out_spec, out_spec],
        out_shape=[out_sds, out_sds],
        compiler_params=_cparams(1),
        name=f"attn_sample_g{g}",
    )(a_s, a_s, a_s, cv)


def _log_decay(ga, wa, ba):
    z = jnp.dot(ga, wa, preferred_element_type=F32, precision=lax.Precision.HIGHEST) + ba
    return (jnp.minimum(z, 0.0) - jnp.log(1.0 + jnp.exp(-jnp.abs(z)))) * (1.0 / GLA_TAU)


def _gla_prompt_kernel(q_ref, k_ref, v_ref, ga_ref, wa_ref, ba_ref, o_ref, sfin_ref, st_ref, la_ref,
                      qe_ref, upd_ref, ebl_ref):
    c = pl.program_id(1)
    C = GLA_CHUNK
    per_run = C // RUNS
    nsb = C // GLA_SUB
    sub_w = GLA_SUB // RUNS

    @pl.when(c == 0)
    def _():
        st_ref[...] = jnp.zeros_like(st_ref)

    la_ref[...] = _log_decay(ga_ref[...], wa_ref[...], ba_ref[...])

    r_i = lax.broadcasted_iota(jnp.int32, (C, C), 0)
    c_i = lax.broadcasted_iota(jnp.int32, (C, C), 1)
    tok_r = RUNS * (r_i % per_run) + r_i // per_run
    tok_c = RUNS * (c_i % per_run) + c_i // per_run
    tri = (tok_c <= tok_r).astype(F32)
    diag = (tok_r // GLA_SUB == tok_c // GLA_SUB) & (tok_c <= tok_r)
    rowblk = (lax.broadcasted_iota(jnp.int32, (C, GLA_HK), 0) % per_run) // sub_w
    last_run = (RUNS - 1) * per_run

    def gather(ref, j):
        return jnp.concatenate(
            [ref[r * RUN_LEN + j * per_run: r * RUN_LEN + (j + 1) * per_run, :] for r in range(RUNS)], axis=0)

    def scatter(ref, j, val, add):
        for r in range(RUNS):
            rows = slice(r * RUN_LEN + j * per_run, r * RUN_LEN + (j + 1) * per_run)
            piece = val[r * per_run:(r + 1) * per_run, :]
            ref[rows, :] = ref[rows, :] + piece if add else piece

    def local(js):
        def ref_rows(b, i):
            end_row = last_run + (i + 1) * sub_w - 1
            return b[end_row - sub_w:end_row - sub_w + 1, :], b[end_row:end_row + 1, :]

        b = [jnp.dot(tri, gather(la_ref, j), preferred_element_type=F32, precision=lax.Precision.HIGHEST) for j in js]
        q = [gather(q_ref, j) * (GLA_HK ** -0.5) for j in js]
        k = [gather(k_ref, j) for j in js]
        v = [gather(v_ref, j).astype(BF16) for j in js]
        qd, kd, ke, bstart = [], [], [], []
        for t, j in enumerate(js):
            qe_ref[j] = (q[t] * jnp.exp(b[t])).astype(BF16)
            bs = jnp.zeros_like(b[t])
            be = jnp.zeros_like(b[t])
            for i in range(nsb):
                prev_end, own_end = ref_rows(b[t], i)
                be = jnp.where(rowblk == i, own_end, be)
                if i > 0:
                    bs = jnp.where(rowblk == i, prev_end, bs)
            bstart.append(bs)
            qd.append(q[t] * jnp.exp(b[t] - bs))
            kd.append((k[t] * jnp.exp(bs - b[t])).astype(BF16))
            ke.append(k[t] * jnp.exp(be - b[t]))
        att = [jnp.where(diag, lax.dot_general(qd[t].astype(BF16), kd[t], _NT, preferred_element_type=F32), 0.0)
               for t in range(len(js))]
        for i in range(nsb - 1):
            qj, kej = [], []
            for t in range(len(js)):
                own_end = ref_rows(b[t], i)[1]
                dj = jnp.where(rowblk > i, jnp.exp(jnp.minimum(bstart[t] - own_end, 0.0)), 0.0)
                qj.append((qd[t] * dj).astype(BF16))
                kej.append(jnp.where(rowblk == i, ke[t], 0.0).astype(BF16))
            att = [att[t] + lax.dot_general(qj[t], kej[t], _NT, preferred_element_type=F32) for t in range(len(js))]
        intra = [jnp.dot(att[t].astype(BF16), v[t], preferred_element_type=F32) for t in range(len(js))]
        kdec = [(k[t] * jnp.exp(b[t][C - 1:C, :] - b[t])).astype(BF16) for t in range(len(js))]
        upd = [lax.dot_general(v[t], kdec[t], _TN, preferred_element_type=F32) for t in range(len(js))]
        for t, j in enumerate(js):
            scatter(o_ref, j, intra[t], add=False)
            upd_ref[j] = upd[t]
            ebl_ref[j] = jnp.broadcast_to(jnp.exp(b[t][C - 1:C, :]), (SUBLANES, GLA_HK))

    def recur(j):
        st = st_ref[...]
        scatter(o_ref, j, lax.dot_general(qe_ref[j], st.astype(BF16), _NT, preferred_element_type=F32), add=True)
        st_ref[...] = st * ebl_ref[j][0:1, :] + upd_ref[j]

    n_steps = CHUNK // C
    for j0 in range(0, n_steps, GLA_GROUP):
        local(list(range(j0, j0 + GLA_GROUP)))
    for j in range(n_steps):
        recur(j)

    @pl.when(c == pl.num_programs(1) - 1)
    def _():
        sfin_ref[...] = st_ref[...].T


def _gla_prompt(b, wa_pad, ba):
    T = seq = b.shape[0]
    assert seq % CHUNK == 0
    q0 = GLA_DV // GLA_HK
    k0 = q0 + GLA_DK // GLA_HK
    a0 = k0 + GLA_DK // GLA_HK
    og, sfin = pl.pallas_call(
        _gla_prompt_kernel,
        grid=(GLA_HEADS, seq // CHUNK),
        in_specs=[pl.BlockSpec((CHUNK, GLA_HK), lambda h, n: (n, q0 + h)),
                  pl.BlockSpec((CHUNK, GLA_HK), lambda h, n: (n, k0 + h)),
                  pl.BlockSpec((CHUNK, GLA_HV), lambda h, n: (n, h)),
                  pl.BlockSpec((CHUNK, LANE), lambda h, n: (n, a0)),
                  pl.BlockSpec((LANE, GLA_HK), lambda h, n: (0, h)),
                  pl.BlockSpec((1, GLA_HK), lambda h, n: (0, h))],
        out_specs=[pl.BlockSpec((CHUNK, GLA_HV), lambda h, n: (n, h)),
                   pl.BlockSpec((None, GLA_HK, GLA_HV), lambda h, n: (h, 0, 0))],
        out_shape=[jax.ShapeDtypeStruct((T, GLA_DV), F32),
                   jax.ShapeDtypeStruct((GLA_HEADS, GLA_HK, GLA_HV), F32)],
        scratch_shapes=[pltpu.VMEM((GLA_HV, GLA_HK), F32), pltpu.VMEM((CHUNK, GLA_HK), F32),
                        pltpu.VMEM((CHUNK // GLA_CHUNK, GLA_CHUNK, GLA_HK), BF16),
                        pltpu.VMEM((CHUNK // GLA_CHUNK, GLA_HV, GLA_HK), F32),
                        pltpu.VMEM((CHUNK // GLA_CHUNK, SUBLANES, GLA_HK), F32)],
        compiler_params=_cparams(2),
        name="gla_prompt",
    )(b, b, b, b, wa_pad, ba)
    return og, sfin


def _gla_sample_kernel(q_ref, k_ref, vlo_ref, vhi_ref, ga_ref, wa_ref, ba_ref, s0_ref, o_ref, snew_ref,
                       *, dec_seq, precise):
    rows8 = 2 * dec_seq
    r_i = lax.broadcasted_iota(jnp.int32, (rows8, rows8), 0)
    c_i = lax.broadcasted_iota(jnp.int32, (rows8, rows8), 1)
    same = (r_i // dec_seq == c_i // dec_seq) & (c_i <= r_i)
    tri = same.astype(F32)
    la = _log_decay(ga_ref[...], wa_ref[...], ba_ref[...])
    b_all = jnp.dot(tri, la, preferred_element_type=F32, precision=lax.Precision.HIGHEST)
    row_seq_k = lax.broadcasted_iota(jnp.int32, (rows8, GLA_HK), 0) // dec_seq
    row_seq_v = lax.broadcasted_iota(jnp.int32, (rows8, GLA_HV), 0) // dec_seq
    for h in range(GLA_HEADS):
        ks = slice(h * GLA_HK, (h + 1) * GLA_HK)
        vs = slice(h * GLA_HV, (h + 1) * GLA_HV)
        b = b_all[:, ks]
        q = q_ref[:, ks] * (GLA_HK ** -0.5)
        k = k_ref[:, ks]
        half = GLA_HEADS // 2
        v_src = vlo_ref if h < half else vhi_ref
        v = v_src[:, (h % half) * GLA_HV:(h % half + 1) * GLA_HV]
        qd = q * jnp.exp(b)
        kd = k * jnp.exp(-b)
        att = jnp.where(same, _dot(qd, kd, precise, _NT), 0.0)
        o = _dot(att, v, precise)
        for bb in range(2):
            st = s0_ref[bb, h].T
            o = o + jnp.where(row_seq_v == bb, _dot(qd, st, precise, _NT), 0.0)
            bl = b[(bb + 1) * dec_seq - 1:(bb + 1) * dec_seq, :]
            kdec = jnp.where(row_seq_k == bb, k * jnp.exp(bl - b), 0.0)
            snew_ref[bb, h] = (st * jnp.exp(bl) + _dot(v, kdec, precise, _TN)).T
        o_ref[:, vs] = o


def _gla_sample(b_s, wa_pad, ba, state, layer, dec_seq, precise):
    B = state.shape[1]
    rows8 = 2 * dec_seq
    q0 = GLA_DV // GLA_DK
    a0 = (GLA_DV + 2 * GLA_DK) // LANE
    assert q0 == 2
    blk = lambda c: pl.BlockSpec((rows8, GLA_DK), lambda i: (i, c))
    return pl.pallas_call(
        functools.partial(_gla_sample_kernel, dec_seq=dec_seq, precise=precise),
        grid=(B // 2,),
        in_specs=[blk(q0), blk(q0 + 1), blk(0), blk(1),
                  pl.BlockSpec((rows8, LANE), lambda i: (i, a0)),
                  pl.BlockSpec((LANE, GLA_DK), lambda i: (0, 0)),
                  pl.BlockSpec((1, GLA_DK), lambda i: (0, 0)),
                  pl.BlockSpec((None, 2, GLA_HEADS, GLA_HK, GLA_HV), lambda i: (layer, i, 0, 0, 0))],
        out_specs=[pl.BlockSpec((rows8, GLA_DV), lambda i: (i, 0)),
                   pl.BlockSpec((2, GLA_HEADS, GLA_HK, GLA_HV), lambda i: (i, 0, 0, 0))],
        out_shape=[jax.ShapeDtypeStruct((B * dec_seq, GLA_DV), F32), jax.ShapeDtypeStruct(state.shape[1:], F32)],
        compiler_params=_cparams(1),
        name="gla_sample",
    )(b_s, b_s, b_s, b_s, b_s, wa_pad, ba, state)


def _post_kernel(o1, o2, o3, l1, l2, l3, og_ref, xin_ref, x_ref, *rest, alpha, precise):
    n_gate = 2 if precise else 1
    wgate_refs = rest[:n_gate]
    wpa_ref, wpb_ref, wout_ref, gn_ref, lg_ref, lb_ref, xo_ref, xb_ref = rest[n_gate:]
    wgate = tuple(r[...] for r in wgate_refs) if precise else wgate_refs[0][...]
    gates = _dot(xin_ref[...], wgate, precise)
    D = x_ref.shape[1]
    gr, gate_a, gate_b = gates[:, :GLA_DV], gates[:, GLA_DV:GLA_DV + D], gates[:, GLA_DV + D:]
    m = jnp.maximum(jnp.maximum(l1[...], l2[...]), l3[...])
    e1 = jnp.exp(l1[...] - m)
    e2 = jnp.exp(l2[...] - m)
    e3 = jnp.exp(l3[...] - m)
    att = (e1 * o1[...] + e2 * o2[...] + e3 * o3[...]) / (e1 + e2 + e3)
    ya = _dot(att, wpa_ref[...], precise)
    parts = []
    for h in range(GLA_HEADS):
        vs = slice(h * GLA_HV, (h + 1) * GLA_HV)
        og = og_ref[:, vs]
        ms = jnp.mean(og * og, axis=-1, keepdims=True)
        parts.append(_operand(og * lax.rsqrt(ms + RMS_EPS) * gn_ref[:, vs] * _silu(gr[:, vs]), precise))
    y = jnp.concatenate(parts, axis=-1)
    yb = _dot(y, wpb_ref[...], precise)
    merged = _sigmoid(gate_a) * ya + _sigmoid(gate_b) * yb
    mix = _dot(merged, wout_ref[...], precise)
    xn = _layer_norm_rows(alpha * x_ref[...] + mix, lg_ref[...], lb_ref[...])
    xo_ref[...] = xn
    xb_ref[...] = xn.astype(BF16)


def _post(os_, ls_, og, xin, x, wgate, wpa, wpb, wout, gn, lg, lb, alpha, tm):
    precise = isinstance(wgate, tuple)
    wgates = wgate if precise else (wgate,)
    T, D = x.shape
    row = lambda w: pl.BlockSpec((tm, w), lambda i: (i, 0))
    full = lambda a: pl.BlockSpec(a.shape, lambda i: (0,) * a.ndim)
    resident = lambda a: pl.BlockSpec(a.shape, lambda i: (0,) * a.ndim, pipeline_mode=pl.Buffered(1))
    return pl.pallas_call(
        functools.partial(_post_kernel, alpha=alpha, precise=precise),
        grid=(pl.cdiv(T, tm),),
        in_specs=[row(ATT_WIDTH)] * 6 + [row(GLA_DV), row(D), row(D)] + [resident(w) for w in wgates]
                 + [resident(wpa), resident(wpb), resident(wout), full(gn), full(lg), full(lb)],
        out_specs=[row(D), row(D)],
        out_shape=[jax.ShapeDtypeStruct((T, D), F32), jax.ShapeDtypeStruct((T, D), BF16)],
        compiler_params=_cparams(1),
        name="post_mixer_sample" if precise else "post_mixer",
    )(*os_, *ls_, og, xin, x, *wgates, wpa, wpb, wout, gn, lg, lb)


def _ffn_kernel(xb_ref, x_ref, wg_ref, wu_ref, wd_ref, lg_ref, lb_ref, xo_ref, xbo_ref, acc_ref, *, alpha, precise):
    j = pl.program_id(1)
    xb = xb_ref[...]
    g = _dot(xb, wg_ref[...], precise)
    u = _dot(xb, wu_ref[...], precise)
    part = _dot(_silu(g) * u, wd_ref[...], precise)

    @pl.when(j == 0)
    def _():
        acc_ref[...] = part

    @pl.when(j > 0)
    def _():
        acc_ref[...] += part

    @pl.when(j == pl.num_programs(1) - 1)
    def _():
        xn = _layer_norm_rows(alpha * x_ref[...] + acc_ref[...], lg_ref[...], lb_ref[...])
        xo_ref[...] = xn
        xbo_ref[...] = xn.astype(BF16)


def _ffn(xb, x, wgu, wd, lg, lb, alpha, tm, tc, precise=False):
    T, D = x.shape
    dff = wd.shape[0]
    assert dff % tc == 0
    nj = dff // tc
    full = lambda a: pl.BlockSpec(a.shape, lambda i, j: (0,) * a.ndim)
    wmode = dict(pipeline_mode=pl.Buffered(1)) if nj == 1 else {}
    return pl.pallas_call(
        functools.partial(_ffn_kernel, alpha=alpha, precise=precise),
        grid=(pl.cdiv(T, tm), nj),
        in_specs=[pl.BlockSpec((tm, D), lambda i, j: (i, 0)),
                  pl.BlockSpec((tm, D), lambda i, j: (i, 0)),
                  pl.BlockSpec((D, tc), lambda i, j: (0, j), **wmode),
                  pl.BlockSpec((D, tc), lambda i, j: (0, nj + j), **wmode),
                  pl.BlockSpec((tc, D), lambda i, j: (j, 0), **wmode),
                  full(lg), full(lb)],
        out_specs=[pl.BlockSpec((tm, D), lambda i, j: (i, 0))] * 2,
        out_shape=[jax.ShapeDtypeStruct((T, D), F32), jax.ShapeDtypeStruct((T, D), BF16)],
        scratch_shapes=[pltpu.VMEM((tm, D), F32)],
        compiler_params=_cparams(2),
        name="ffn_dense_sample" if precise else "ffn_dense",
    )(xb, x, wgu, wgu, wd, lg, lb)


MOE_TOKENS = 512
MOE_PIECE = 16
MOE_ROWS = -(-(2 * MOE_TOKENS + N_EXPERTS * (MOE_PIECE - 1)) // MOE_PIECE) * MOE_PIECE
MOE_BLOCK = 1024
MOE_HALF = 512
MOE_SLOTS = MOE_BLOCK // MOE_PIECE


def _tile_rows(tile_ref, tail_ref, is_tail):
    ts, d = tile_ref.shape
    pad = jnp.zeros((ts - tail_ref.shape[0], d), tail_ref.dtype)
    tail = jnp.concatenate([tail_ref[...], pad], axis=0).astype(tile_ref.dtype)
    return jnp.where(is_tail, tail, tile_ref[...])


def _moe_route_kernel(x_ref, xb_ref, xtail_ref, wrt_ref, xs_ref, meta_ref, cnt_ref, *, prompt_tiles):
    i = pl.program_id(0)
    ts = x_ref.shape[0]
    is_tail = i == prompt_tiles
    x = _tile_rows(x_ref, xtail_ref, is_tail)
    xb = _tile_rows(xb_ref, xtail_ref, is_tail)
    lt = lax.dot_general(wrt_ref[...], x, _NT, preferred_element_type=F32, precision=lax.Precision.HIGHEST)
    sub = lax.broadcasted_iota(jnp.int32, (N_EXPERTS, ts), 0).astype(F32)
    tok_ok = lax.broadcasted_iota(jnp.int32, (1, ts), 1) < jnp.where(is_tail, xtail_ref.shape[0], ts)
    m1 = jnp.max(lt, axis=0, keepdims=True)
    i1 = jnp.min(jnp.where(lt == m1, sub, float(N_EXPERTS)), axis=0, keepdims=True)
    lt2 = jnp.where(sub == i1, NEG, lt)
    m2 = jnp.max(lt2, axis=0, keepdims=True)
    i2 = jnp.min(jnp.where(lt2 == m2, sub, float(N_EXPERTS)), axis=0, keepdims=True)
    e2 = jnp.exp(m2 - m1)
    g1 = 1.0 / (1.0 + e2)
    g2 = e2 / (1.0 + e2)
    oh1 = (sub == i1) & tok_ok
    oh2 = (sub == i2) & tok_ok
    assign = jnp.where(oh1 | oh2, 1.0, 0.0)
    before = (lax.broadcasted_iota(jnp.int32, (ts, ts), 0) < lax.broadcasted_iota(jnp.int32, (ts, ts), 1))
    rank = jnp.dot(assign.astype(BF16), jnp.where(before, 1.0, 0.0).astype(BF16), preferred_element_type=F32)
    cnt = jnp.sum(assign, axis=1, keepdims=True)
    padded = jnp.floor((cnt + (MOE_PIECE - 1)) * (1.0 / MOE_PIECE)) * MOE_PIECE
    lower = (lax.broadcasted_iota(jnp.int32, (N_EXPERTS, N_EXPERTS), 1)
             < lax.broadcasted_iota(jnp.int32, (N_EXPERTS, N_EXPERTS), 0)).astype(F32)
    seg_off = jnp.dot(lower, jnp.broadcast_to(padded, (N_EXPERTS, ts)), preferred_element_type=F32,
                      precision=lax.Precision.HIGHEST)
    dest = seg_off + rank
    d1 = jnp.where(tok_ok, jnp.sum(jnp.where(oh1, dest, 0.0), axis=0, keepdims=True), -1.0)
    d2 = jnp.where(tok_ok, jnp.sum(jnp.where(oh2, dest, 0.0), axis=0, keepdims=True), -1.0)
    p_i = lax.broadcasted_iota(jnp.int32, (MOE_ROWS, ts), 0).astype(F32)
    perm = jnp.where((p_i == d1) | (p_i == d2), 1.0, 0.0).astype(BF16)
    xs_ref[...] = jnp.dot(perm, xb, preferred_element_type=F32).astype(BF16)
    sub8 = lax.broadcasted_iota(jnp.int32, (SUBLANES, ts), 0)
    meta_ref[...] = jnp.where(sub8 == 0, d1, jnp.where(sub8 == 1, d2, jnp.where(sub8 == 2, g1,
                              jnp.where(sub8 == 3, g2, 0.0))))
    cnt_ref[...] = jnp.broadcast_to(cnt, (N_EXPERTS, LANE))


def _moe_route(x, xb, x_tail, wr_t):
    seq, D = x.shape
    assert seq % MOE_TOKENS == 0 and 0 < x_tail.shape[0] <= MOE_TOKENS
    npt = seq // MOE_TOKENS
    nt = npt + 1
    tile = pl.BlockSpec((MOE_TOKENS, D), lambda i: (jnp.minimum(i, npt - 1), 0))
    return pl.pallas_call(
        functools.partial(_moe_route_kernel, prompt_tiles=npt),
        grid=(nt,),
        in_specs=[tile, tile,
                  pl.BlockSpec(x_tail.shape, lambda i: (0, 0)),
                  pl.BlockSpec(wr_t.shape, lambda i: (0, 0))],
        out_specs=[pl.BlockSpec((MOE_ROWS, D), lambda i: (i, 0)),
                   pl.BlockSpec((None, SUBLANES, MOE_TOKENS), lambda i: (i, 0, 0)),
                   pl.BlockSpec((None, N_EXPERTS, LANE), lambda i: (i, 0, 0))],
        out_shape=[jax.ShapeDtypeStruct((nt * MOE_ROWS, D), BF16),
                   jax.ShapeDtypeStruct((nt, SUBLANES, MOE_TOKENS), F32),
                   jax.ShapeDtypeStruct((nt, N_EXPERTS, LANE), F32)],
        compiler_params=_cparams(1),
        name="moe_route",
    )(x, xb, x_tail, wr_t)


def _moe_plan(cnt, n_blocks):
    nt = cnt.shape[0]
    i32 = jnp.int32
    padded = (cnt + MOE_PIECE - 1) // MOE_PIECE * MOE_PIECE
    seg_row = jnp.cumsum(padded, axis=1) - padded + (jnp.arange(nt, dtype=i32) * MOE_ROWS)[:, None]
    n_seg = (padded // MOE_PIECE).T.reshape(-1)
    seg_row = seg_row.T.reshape(-1)
    seg_end = jnp.cumsum(n_seg)
    seg_start = seg_end - n_seg
    per_e = n_seg.reshape(N_EXPERTS, nt).sum(axis=1)
    e_end = jnp.cumsum(per_e)
    e_start = e_end - per_e
    blocks_e = (per_e + MOE_SLOTS - 1) // MOE_SLOTS
    blk_end = jnp.cumsum(blocks_e)
    blk_start = blk_end - blocks_e
    w = jnp.arange(n_blocks, dtype=i32)
    ew = jnp.minimum(jnp.sum((w[:, None] >= blk_end[None, :]).astype(i32), axis=1), N_EXPERTS - 1)
    is_e = ew[:, None] == jnp.arange(N_EXPERTS, dtype=i32)[None, :]
    pick = lambda v: jnp.sum(jnp.where(is_e, v[None, :], 0), axis=1)
    first = pick(e_start) + (w - pick(blk_start)) * MOE_SLOTS
    n_w = jnp.where(w < blk_end[-1], jnp.clip(pick(e_end) - first, 0, MOE_SLOTS), 0).astype(i32)
    slot = jnp.arange(MOE_SLOTS, dtype=i32)
    p = first[:, None] + slot[None, :]
    in_seg = (p[:, :, None] >= seg_start[None, None, :]) & (p[:, :, None] < seg_end[None, None, :])
    rows = jnp.sum(jnp.where(in_seg, (seg_row - MOE_PIECE * seg_start)[None, None, :], 0), axis=2) + MOE_PIECE * p
    rows = jnp.where(slot[None, :] < n_w[:, None], rows, 0).astype(i32)
    return ew.astype(i32), n_w, rows.reshape(-1)


def _moe_expert_kernel(ew_ref, n_ref, rows_ref, xs_hbm, wg_ref, wu_ref, wd_ref, ys_in, ys_hbm,
                       lhs_ref, acc_ref, out_ref, sem_in, sem_out):
    del ew_ref, ys_in
    w = pl.program_id(0)
    j = pl.program_id(1)
    n_w = pl.num_programs(0)
    last_j = pl.num_programs(1) - 1
    n = n_ref[w]
    slot = w % 2
    halves = [(s, slice(s * MOE_HALF, (s + 1) * MOE_HALF)) for s in range(MOE_BLOCK // MOE_HALF)]

    def piece_in(blk, q, sl):
        src = pl.multiple_of(rows_ref[blk * MOE_SLOTS + q], MOE_PIECE)
        dst = pl.multiple_of(q * MOE_PIECE, MOE_PIECE)
        return pltpu.make_async_copy(xs_hbm.at[pl.ds(src, MOE_PIECE), :],
                                     lhs_ref.at[sl, pl.ds(dst, MOE_PIECE), :], sem_in.at[sl])

    def piece_out(blk, q):
        dst = pl.multiple_of(rows_ref[blk * MOE_SLOTS + q], MOE_PIECE)
        src = pl.multiple_of(q * MOE_PIECE, MOE_PIECE)
        return pltpu.make_async_copy(out_ref.at[pl.ds(src, MOE_PIECE), :], ys_hbm.at[pl.ds(dst, MOE_PIECE), :], sem_out)

    def for_pieces(blk, fn):
        lax.fori_loop(0, n_ref[blk], lambda q, c: (fn(q), c)[1], 0)

    @pl.when(j == 0)
    def _():
        @pl.when(w == 0)
        def _():
            lhs_ref[...] = jnp.zeros_like(lhs_ref)
            for_pieces(0, lambda q: piece_in(0, q, 0).start())

        for_pieces(w, lambda q: piece_in(w, q, slot).wait())

        @pl.when(w + 1 < n_w)
        def _():
            for_pieces(w + 1, lambda q: piece_in(w + 1, q, 1 - slot).start())

    for s, rows in halves:
        @pl.when(n > s * (MOE_HALF // MOE_PIECE))
        def _(rows=rows):
            xb = lhs_ref[slot, rows, :]
            g = jnp.dot(xb, wg_ref[...], preferred_element_type=F32)
            u = jnp.dot(xb, wu_ref[...], preferred_element_type=F32)
            part = jnp.dot((_silu(g) * u).astype(BF16), wd_ref[...], preferred_element_type=F32)

            @pl.when(j == 0)
            def _():
                acc_ref[rows, :] = part

            @pl.when(j > 0)
            def _():
                acc_ref[rows, :] += part

    @pl.when(j == last_j)
    def _():
        @pl.when(w > 0)
        def _():
            for_pieces(w - 1, lambda q: piece_out(w - 1, q).wait())

        for s, rows in halves:
            @pl.when(n > s * (MOE_HALF // MOE_PIECE))
            def _(rows=rows):
                out_ref[rows, :] = acc_ref[rows, :].astype(BF16)

        for_pieces(w, lambda q: piece_out(w, q).start())

        @pl.when(w == n_w - 1)
        def _():
            for_pieces(w, lambda q: piece_out(w, q).wait())


def _moe_experts(xs, plan, wgu, wd, tc):
    ew, n_w, rows = plan
    n_blocks = ew.shape[0]
    D = xs.shape[1]
    dff = wd.shape[1]
    assert dff % tc == 0
    nj = dff // tc
    any_spec = pl.BlockSpec(memory_space=pl.ANY)
    return pl.pallas_call(
        _moe_expert_kernel,
        grid_spec=pltpu.PrefetchScalarGridSpec(
            num_scalar_prefetch=3,
            grid=(n_blocks, nj),
            in_specs=[any_spec,
                      pl.BlockSpec((None, D, tc), lambda w, j, ew, n, r: (ew[w], 0, j)),
                      pl.BlockSpec((None, D, tc), lambda w, j, ew, n, r: (ew[w], 0, nj + j)),
                      pl.BlockSpec((None, tc, D), lambda w, j, ew, n, r: (ew[w], j, 0)),
                      any_spec],
            out_specs=any_spec,
            scratch_shapes=[pltpu.VMEM((2, MOE_BLOCK, D), BF16), pltpu.VMEM((MOE_BLOCK, D), F32),
                            pltpu.VMEM((MOE_BLOCK, D), BF16),
                            pltpu.SemaphoreType.DMA((2,)), pltpu.SemaphoreType.DMA(())]),
        out_shape=jax.ShapeDtypeStruct(xs.shape, BF16),
        input_output_aliases={7: 0},
        compiler_params=_cparams(2),
        name="moe_experts",
    )(ew, n_w, rows, xs, wgu, wgu, wd, jnp.zeros(xs.shape, BF16))


def _moe_combine_kernel(ys_ref, meta_ref, x_ref, xtail_ref, lg_ref, lb_ref, out_x, out_xb, out_tail,
                        *, alpha, prompt_tiles):
    i = pl.program_id(0)
    ts = x_ref.shape[0]
    meta = meta_ref[...]
    d1, d2, g1, g2 = meta[0:1, :], meta[1:2, :], meta[2:3, :], meta[3:4, :]
    p_i = lax.broadcasted_iota(jnp.int32, (MOE_ROWS, ts), 0).astype(F32)
    back = (jnp.where(p_i == d1, g1, 0.0) + jnp.where(p_i == d2, g2, 0.0)).astype(BF16)
    f = lax.dot_general(back, ys_ref[...], _TN, preferred_element_type=F32)
    x = _tile_rows(x_ref, xtail_ref, i == prompt_tiles)
    xn = _layer_norm_rows(alpha * x + f, lg_ref[...], lb_ref[...])

    @pl.when(i < prompt_tiles)
    def _():
        out_x[...] = xn
        out_xb[...] = xn.astype(BF16)

    @pl.when(i == prompt_tiles)
    def _():
        out_tail[...] = xn[:out_tail.shape[0], :]


def _moe_combine(ys, meta, x, x_tail, lg, lb, alpha):
    seq, D = x.shape
    ns = x_tail.shape[0]
    npt = seq // MOE_TOKENS
    full = lambda a: pl.BlockSpec(a.shape, lambda i: (0,) * a.ndim)
    tile = pl.BlockSpec((MOE_TOKENS, D), lambda i: (jnp.minimum(i, npt - 1), 0))
    return pl.pallas_call(
        functools.partial(_moe_combine_kernel, alpha=alpha, prompt_tiles=npt),
        grid=(npt + 1,),
        in_specs=[pl.BlockSpec((MOE_ROWS, D), lambda i: (i, 0)),
                  pl.BlockSpec((None, SUBLANES, MOE_TOKENS), lambda i: (i, 0, 0)),
                  tile, full(x_tail), full(lg), full(lb)],
        out_specs=[tile, tile, pl.BlockSpec((ns, D), lambda i: (0, 0))],
        out_shape=[jax.ShapeDtypeStruct((seq, D), F32), jax.ShapeDtypeStruct((seq, D), BF16),
                   jax.ShapeDtypeStruct((ns, D), F32)],
        compiler_params=_cparams(1),
        name="moe_combine",
    )(ys, meta, x, x_tail, lg, lb)


def _moe(x, xb, x_tail, w_router, wgu, wd, lg, lb, alpha):
    T = x.shape[0] + x_tail.shape[0]
    nt = x.shape[0] // MOE_TOKENS + 1
    xs, meta, cnt = _moe_route(x, xb, x_tail, w_router.T)
    n_pieces = (2 * T + nt * N_EXPERTS * (MOE_PIECE - 1)) // MOE_PIECE + 1
    n_blocks = pl.cdiv(n_pieces, MOE_SLOTS) + N_EXPERTS
    plan = _moe_plan(cnt[:, :, 0].astype(jnp.int32), n_blocks)
    ys = _moe_experts(xs, plan, wgu, wd, MOE_CHUNK)
    return _moe_combine(ys, meta, x, x_tail, lg, lb, alpha)


def _split_w_in(w):
    a_end = N_GROUPS * 3 * ATT_WIDTH
    gq0, gk0, gv0 = a_end, a_end + GLA_DK, a_end + 2 * GLA_DK
    gr0 = gv0 + GLA_DV
    ga0 = gr0 + GLA_DV
    gta0 = ga0 + GLA_RANK
    used = GLA_DV + 2 * GLA_DK + GLA_RANK
    ga_pad = jnp.pad(w[:, ga0:gta0], ((0, 0), (0, GLA_PANEL - used)))
    panels = (w[:, :a_end],
              jnp.concatenate([w[:, gv0:gr0], w[:, gq0:gk0], w[:, gk0:gv0], ga_pad], axis=1),
              jnp.concatenate([w[:, gr0:ga0], w[:, gta0:]], axis=1))
    return tuple(_head_tail(p) for p in panels)


def _head_tail(w):
    head = w.astype(BF16)
    return head, (w - head.astype(F32)).astype(BF16)


def _kv_rows_kernel(kp_ref, vp_ref, ks_ref, vs_ref, op_ref, os_ref, *, keep):
    n = keep // RUNS
    ns = ks_ref.shape[0]
    for kv, (p_ref, s_ref) in enumerate(((kp_ref, ks_ref), (vp_ref, vs_ref))):
        for h in range(HEADS):
            cs = slice(h * HEAD_DIM, (h + 1) * HEAD_DIM)
            sub = kv * HEADS + h
            for r in range(RUNS):
                op_ref[pl.ds(2 * HEADS * r + sub, n, stride=2 * HEADS * RUNS), :] = p_ref[r, :, cs]
            os_ref[pl.ds(sub, ns, stride=2 * HEADS), :] = s_ref[:, cs]


def _new_kv_rows(a_mat, a_s, g, db, dec_seq):
    seq = a_mat.shape[0]
    keep = min(ATT_GROUPS[g][0], seq)
    ns = db * dec_seq
    assert keep <= CHUNK and keep % RUNS == 0 and seq % CHUNK == 0 and a_s.shape[0] == ns
    last = seq // CHUNK - 1
    n = keep // RUNS
    runs = a_mat.reshape(seq // RUN_LEN, RUN_LEN, a_mat.shape[1])
    tail = lambda c: pl.BlockSpec((RUNS, n, ATT_WIDTH), lambda i: (last, RUN_LEN // n - 1, 3 * g + c))
    spec = lambda rows, rb, c: pl.BlockSpec((rows, ATT_WIDTH), lambda i: (rb, 3 * g + c))
    kv_p, kv_s = pl.pallas_call(
        functools.partial(_kv_rows_kernel, keep=keep),
        grid=(1,),
        in_specs=[tail(1), tail(2), spec(ns, 0, 1), spec(ns, 0, 2)],
        out_specs=[pl.BlockSpec((keep * 2 * HEADS, HEAD_DIM), lambda i: (0, 0)),
                   pl.BlockSpec((ns * 2 * HEADS, HEAD_DIM), lambda i: (0, 0))],
        out_shape=[jax.ShapeDtypeStruct((keep * 2 * HEADS, HEAD_DIM), F32),
                   jax.ShapeDtypeStruct((ns * 2 * HEADS, HEAD_DIM), F32)],
        compiler_params=_cparams(1),
        name=f"kv_rows_g{g}",
    )(runs, runs, a_s, a_s)
    return kv_p.reshape(1, keep, 2, HEADS, HEAD_DIM), kv_s.reshape(db, dec_seq, 2, HEADS, HEAD_DIM)


def kernel(x_prompt, x_sample, cache_kv_w128, cache_kv_w512, cache_kv_w2048, state_gla,
           w_in, w_alpha2, b_alpha, gla_norm_g, w_pa, w_pb, w_out,
           ln_mix_g, ln_mix_b, ln_ffn_g, ln_ffn_b,
           w_ffn_gu, w_ffn_down, w_router, w_exp_gu, w_exp_down):
    depth = w_in.shape[0]
    bp, seq, D = x_prompt.shape
    db, dec_seq, _ = x_sample.shape
    assert bp == 1
    ns = db * dec_seq
    alpha = (2 * depth) ** 0.25
    caches = (cache_kv_w128, cache_kv_w512, cache_kv_w2048)

    x = _interleave_rows(x_prompt.reshape(seq, D))
    xb = x.astype(BF16)
    xs = x_sample.reshape(ns, D)
    kv_p = [[] for _ in range(N_GROUPS)]
    kv_s = [[] for _ in range(N_GROUPS)]
    st_p, st_s = [], []
    row2 = lambda v: v.reshape(1, -1)

    for l in range(depth):
        wa, wb, wg = _split_w_in(w_in[l])
        a_mat = _mm(xb, wa[0], PROJ_ROWS, PROJ_COLS_ATT)
        b_mat = _mm(xb, wb[0], PROJ_ROWS, GLA_PANEL)
        a_s = _mm(xs, wa, ns, PROJ_COLS_ATT)
        b_s = _mm(xs, wb, ns, GLA_PANEL)

        os_, ls_, os_s, ls_s = [], [], [], []
        for g in range(N_GROUPS):
            o, lse = _attn_prompt(a_mat, g)
            os_.append(o)
            ls_.append(lse)
            o, lse = _attn_sample(a_s, caches[g], l, g, dec_seq, precise=True)
            os_s.append(o)
            ls_s.append(lse)
            p_rows, s_rows = _new_kv_rows(a_mat, a_s, g, db, dec_seq)
            kv_p[g].append(p_rows)
            kv_s[g].append(s_rows)

        wa_pad = jnp.pad(w_alpha2[l], ((0, LANE - GLA_RANK), (0, 0)))
        og, s_p = _gla_prompt(b_mat, wa_pad, row2(b_alpha[l]))
        og_s, s_s = _gla_sample(b_s, wa_pad, row2(b_alpha[l]), state_gla, l, dec_seq, precise=True)
        st_p.append(s_p[None])
        st_s.append(s_s)

        norms = (row2(gla_norm_g[l]), row2(ln_mix_g[l]), row2(ln_mix_b[l]))
        x, xb = _post(os_, ls_, og, xb, x, wg[0], w_pa[l].astype(BF16), w_pb[l].astype(BF16), w_out[l].astype(BF16),
                      *norms, alpha, POST_ROWS)
        xs, _ = _post(os_s, ls_s, og_s, xs, xs, wg, w_pa[l], w_pb[l], w_out[l], *norms, alpha, ns)

        ffn_norm = (row2(ln_ffn_g[l]), row2(ln_ffn_b[l]), alpha)
        if l % 2 == 0:
            x, xb = _ffn(xb, x, w_ffn_gu[l // 2].astype(BF16), w_ffn_down[l // 2].astype(BF16),
                         *ffn_norm, FFN_ROWS, FFN_CHUNK)
            xs, _ = _ffn(xs, xs, w_ffn_gu[l // 2], w_ffn_down[l // 2], *ffn_norm, ns, SAMPLE_FFN_CHUNK, precise=True)
        else:
            x, xb, xs = _moe(x, xb, xs, w_router[l // 2], w_exp_gu[l // 2].astype(BF16),
                             w_exp_down[l // 2].astype(BF16), *ffn_norm)

    return (_deinterleave_rows(x).reshape(1, seq, D), xs.reshape(db, dec_seq, D),
            jnp.stack(kv_p[0], 0), jnp.stack(kv_s[0], 0),
            jnp.stack(kv_p[1], 0), jnp.stack(kv_s[1], 0),
            jnp.stack(kv_p[2], 0), jnp.stack(kv_s[2], 0),
            jnp.stack(st_p, 0), jnp.stack(st_s, 0))
```

```python
import functools

import jax
import jax.numpy as jnp
from jax import lax
from jax.experimental import pallas as pl
from jax.experimental.pallas import tpu as pltpu

F32 = jnp.float32
BF16 = jnp.bfloat16

ATT_GROUPS = ((128, 1), (512, 4), (2048, 16))
N_GROUPS = 3
HEADS = 4
HEAD_DIM = 128
ATT_WIDTH = HEADS * HEAD_DIM
BAND = 128
ATT_GROUP = 16
GLA_HEADS = 4
GLA_HK = 128
GLA_HV = 256
GLA_DK = GLA_HEADS * GLA_HK
GLA_DV = GLA_HEADS * GLA_HV
GLA_RANK = 16
GLA_TAU = 16.0
GLA_CHUNK = 128
GLA_SUB = 32
GLA_GROUP = 16
N_EXPERTS = 8
LN_EPS = 1e-5
RMS_EPS = 1e-6
NEG = -1e30
LOG2E = 1.4426950408889634
LN2 = 0.6931471805599453
LANE = 128
SUBLANES = 8
RUNS = 16
RUN_LEN = 128
CHUNK = RUNS * RUN_LEN
VMEM_LIMIT = 56 * 1024 * 1024

PROJ_ROWS = 1024
PROJ_COLS_ATT = 1536
GLA_PANEL = 2304
POST_ROWS = 256
FFN_ROWS = 512
FFN_CHUNK = 2816
MOE_CHUNK = 1792
SAMPLE_FFN_CHUNK = 256

_NT = (((1,), (1,)), ((), ()))
_TN = (((0,), (0,)), ((), ()))


def _alibi_slopes():
    n = N_GROUPS * HEADS
    return [[2.0 ** (-8.0 * (g * HEADS + h + 1) / n) for h in range(HEADS)] for g in range(N_GROUPS)]


def _cparams(n_axes):
    return pltpu.CompilerParams(dimension_semantics=("arbitrary",) * n_axes, vmem_limit_bytes=VMEM_LIMIT)


def _layer_norm_rows(r, g, b):
    mu = jnp.mean(r, axis=-1, keepdims=True)
    c = r - mu
    var = jnp.mean(c * c, axis=-1, keepdims=True)
    return c * lax.rsqrt(var + LN_EPS) * g + b


def _sigmoid(x):
    return 1.0 / (1.0 + jnp.exp(-x))


def _silu(x):
    return x * _sigmoid(x)


def _operand(x, precise):
    return x.astype(F32 if precise else BF16)


def _dot(a, b, precise, dims=None):
    def mm(u, v):
        if dims is None:
            return jnp.dot(u, v, preferred_element_type=F32)
        return lax.dot_general(u, v, dims, preferred_element_type=F32)

    if not precise:
        return mm(a.astype(BF16), b.astype(BF16))
    a = a.astype(F32)
    a_hi = a.astype(BF16)
    a_lo = (a - a_hi.astype(F32)).astype(BF16)
    if isinstance(b, tuple):
        b_hi, b_lo = b
    else:
        b = b.astype(F32)
        b_hi = b.astype(BF16)
        b_lo = (b - b_hi.astype(F32)).astype(BF16)
    return mm(a_hi, b_hi) + (mm(a_hi, b_lo) + mm(a_lo, b_hi))


def _interleave_rows(x):
    s, d = x.shape
    return x.reshape(s // CHUNK, RUN_LEN, RUNS, d).transpose(0, 2, 1, 3).reshape(s, d)


def _deinterleave_rows(x):
    s, d = x.shape
    return x.reshape(s // CHUNK, RUNS, RUN_LEN, d).transpose(0, 2, 1, 3).reshape(s, d)


def _mm_kernel(x_ref, w_ref, o_ref):
    o_ref[...] = _dot(x_ref[...], w_ref[...], False)


def _mm_precise_kernel(x_ref, wh_ref, wl_ref, o_ref):
    o_ref[...] = _dot(x_ref[...], (wh_ref[...], wl_ref[...]), True)


def _mm(x, w, tm, tn):
    precise = isinstance(w, tuple)
    ws = w if precise else (w,)
    M, K = x.shape
    N = ws[0].shape[1]
    assert N % tn == 0
    w_spec = pl.BlockSpec((K, tn), lambda j, i: (0, j))
    return pl.pallas_call(
        _mm_precise_kernel if precise else _mm_kernel,
        grid=(N // tn, pl.cdiv(M, tm)),
        in_specs=[pl.BlockSpec((tm, K), lambda j, i: (i, 0))] + [w_spec] * len(ws),
        out_specs=pl.BlockSpec((tm, tn), lambda j, i: (i, j)),
        out_shape=jax.ShapeDtypeStruct((M, N), F32),
        compiler_params=_cparams(2),
        name="in_proj_sample" if precise else "in_proj",
    )(x, *ws)


def _attn_prompt_kernel(slope_ref, q_ref, k_ref, v_ref, o_ref, l_ref, kp_ref, vp_ref, *, dil):
    h = pl.program_id(0)
    c = pl.program_id(1)
    G = RUNS // dil
    wq = BAND // G
    scale = HEAD_DIM ** -0.5

    @pl.when(c == 0)
    def _():
        kp_ref[...] = jnp.zeros_like(kp_ref)
        vp_ref[...] = jnp.zeros_like(vp_ref)

    qi = lax.broadcasted_iota(jnp.int32, (BAND, 2 * BAND), 0)
    ci = lax.broadcasted_iota(jnp.int32, (BAND, 2 * BAND), 1)
    e = ci % (2 * wq)
    rel = (G * (qi % wq) + qi // wq) - (G * (e - wq) + ci // (2 * wq))
    valid = (rel >= 0) & (rel <= BAND)
    bias = rel.astype(F32) * (-slope_ref[h] * (dil * LOG2E))
    bias_in = jnp.where(valid, bias, NEG)
    first_e = jnp.where(c > 0, 0, wq)
    bias_first = jnp.where(valid & (e >= first_e), bias, NEG)

    def keys(cur_ref, prev_ref, runs, m):
        if m == 0:
            parts = []
            for rho in runs:
                parts += [prev_ref[(rho + 1) * RUN_LEN - wq:(rho + 1) * RUN_LEN, :],
                          cur_ref[rho * RUN_LEN: rho * RUN_LEN + wq, :]]
        else:
            parts = [cur_ref[rho * RUN_LEN + wq * (m - 1): rho * RUN_LEN + wq * (m + 1), :] for rho in runs]
        return jnp.concatenate(parts, axis=0).astype(BF16)

    blocks = [([r + dil * v for v in range(G)], m) for r in range(dil) for m in range(G)]
    for g0 in range(0, len(blocks), ATT_GROUP):
        grp = blocks[g0:g0 + ATT_GROUP]
        q = [(jnp.concatenate([q_ref[rho * RUN_LEN + wq * m: rho * RUN_LEN + wq * (m + 1), :] for rho in runs],
                              axis=0) * (scale * LOG2E)).astype(BF16) for runs, m in grp]
        s = [lax.dot_general(q[t], keys(k_ref, kp_ref, runs, m), _NT, preferred_element_type=F32)
             + (bias_first if m == 0 else bias_in) for t, (runs, m) in enumerate(grp)]
        mx = [jnp.max(x, axis=-1, keepdims=True) for x in s]
        p = [jnp.exp2(s[t] - mx[t]) for t in range(len(grp))]
        l = [jnp.sum(x, axis=-1, keepdims=True) for x in p]
        o = [jnp.dot(p[t].astype(BF16), keys(v_ref, vp_ref, runs, m), preferred_element_type=F32) / l[t]
             for t, (runs, m) in enumerate(grp)]
        for t, (runs, m) in enumerate(grp):
            lse = jnp.broadcast_to(mx[t] * LN2 + jnp.log(l[t]), (BAND, HEAD_DIM))
            for idx, rho in enumerate(runs):
                rows = slice(rho * RUN_LEN + wq * m, rho * RUN_LEN + wq * (m + 1))
                o_ref[rows, :] = o[t][idx * wq:(idx + 1) * wq, :]
                l_ref[rows, :] = lse[idx * wq:(idx + 1) * wq, :]

    kp_ref[...] = k_ref[...]
    vp_ref[...] = v_ref[...]


def _attn_prompt(a_mat, g):
    T = seq = a_mat.shape[0]
    _, dil = ATT_GROUPS[g]
    assert seq % CHUNK == 0 and RUNS % dil == 0
    slopes = jnp.asarray(_alibi_slopes()[g], F32)

    def col(c):
        return pl.BlockSpec((CHUNK, HEAD_DIM), lambda h, n: (n, (3 * g + c) * HEADS + h))

    out_spec = pl.BlockSpec((CHUNK, HEAD_DIM), lambda h, n: (n, h))
    out_sds = jax.ShapeDtypeStruct((T, ATT_WIDTH), F32)
    return pl.pallas_call(
        functools.partial(_attn_prompt_kernel, dil=dil),
        grid=(HEADS, seq // CHUNK),
        in_specs=[pl.BlockSpec(memory_space=pltpu.SMEM), col(0), col(1), col(2)],
        out_specs=[out_spec, out_spec],
        out_shape=[out_sds, out_sds],
        scratch_shapes=[pltpu.VMEM((CHUNK, HEAD_DIM), F32), pltpu.VMEM((CHUNK, HEAD_DIM), F32)],
        compiler_params=_cparams(2),
        name=f"attn_prompt_g{g}",
    )(slopes, a_mat, a_mat, a_mat)


def _attn_sample_kernel(q_ref, k_ref, v_ref, c_ref, o_ref, l_ref, *, slopes, window, dil, dec_seq, nsub, precise):
    R = c_ref.shape[1]
    wb = window
    stride = wb // R
    rows8 = 2 * dec_seq
    scale = HEAD_DIM ** -0.5
    ncache = R * nsub

    ri = lax.broadcasted_iota(jnp.int32, (rows8, ncache), 0)
    ji = lax.broadcasted_iota(jnp.int32, (rows8, ncache), 1)
    t_row = ri % dec_seq
    c_true = (ji % R) * stride + ji // R
    dist_c = wb + t_row - c_true
    ok_c = ((dist_c & (dil - 1)) == 0) & (dist_c <= window)
    bias_c_base = jnp.where(ok_c, 0.0, NEG)
    dist_c_f = dist_c.astype(F32)

    rn = lax.broadcasted_iota(jnp.int32, (rows8, rows8), 0)
    jn = lax.broadcasted_iota(jnp.int32, (rows8, rows8), 1)
    dist_n = rn % dec_seq - jn % dec_seq
    ok_n = (rn // dec_seq == jn // dec_seq) & (dist_n >= 0) & ((dist_n & (dil - 1)) == 0)
    bias_n_base = jnp.where(ok_n, 0.0, NEG)
    dist_n_f = dist_n.astype(F32)
    row_seq = lax.broadcasted_iota(jnp.int32, (rows8, HEAD_DIM), 0) // dec_seq

    for h in range(HEADS):
        cs = slice(h * HEAD_DIM, (h + 1) * HEAD_DIM)
        q = q_ref[:, cs]
        kn = k_ref[:, cs]
        vn = v_ref[:, cs]
        s_n = _dot(q, kn, precise, _NT) * scale - dist_n_f * slopes[h] + bias_n_base
        o_h = jnp.zeros((rows8, HEAD_DIM), F32)
        l_h = jnp.zeros((rows8, HEAD_DIM), F32)
        for bb in range(2):
            kc = jnp.concatenate([c_ref[bb, :, u * 2 * HEADS + h, :] for u in range(nsub)], axis=0)
            vc = jnp.concatenate([c_ref[bb, :, u * 2 * HEADS + HEADS + h, :] for u in range(nsub)], axis=0)
            s_c = _dot(q, kc, precise, _NT) * scale - dist_c_f * slopes[h] + bias_c_base
            m = jnp.maximum(jnp.max(s_c, axis=-1, keepdims=True), jnp.max(s_n, axis=-1, keepdims=True))
            p_c = jnp.exp(s_c - m)
            p_n = jnp.exp(s_n - m)
            l = jnp.sum(p_c, axis=-1, keepdims=True) + jnp.sum(p_n, axis=-1, keepdims=True)
            o = (_dot(p_c, vc, precise) + _dot(p_n, vn, precise)) / l
            mine = row_seq == bb
            o_h = jnp.where(mine, o, o_h)
            l_h = jnp.where(mine, jnp.broadcast_to(m + jnp.log(l), (rows8, HEAD_DIM)), l_h)
        o_ref[:, cs] = o_h
        l_ref[:, cs] = l_h


def _attn_sample(a_s, cache, layer, g, dec_seq, precise):
    window, dil = ATT_GROUPS[g]
    depth, B, wb = cache.shape[0], cache.shape[1], cache.shape[2]
    assert wb == window and B % 2 == 0 and dec_seq == 4
    rows8 = 2 * dec_seq
    R = min(wb, BAND)
    stride = wb // R
    nsub = min(stride, dec_seq)
    cv = cache.reshape(depth, B, R, stride * 2 * HEADS, HEAD_DIM)
    c0 = 3 * g

    def new(c):
        return pl.BlockSpec((rows8, ATT_WIDTH), lambda i: (i, c0 + c))

    out_spec = pl.BlockSpec((rows8, ATT_WIDTH), lambda i: (i, 0))
    out_sds = jax.ShapeDtypeStruct((B * dec_seq, ATT_WIDTH), F32)
    return pl.pallas_call(
        functools.partial(_attn_sample_kernel, slopes=_alibi_slopes()[g], window=window, dil=dil,
                          dec_seq=dec_seq, nsub=nsub, precise=precise),
        grid=(B // 2,),
        in_specs=[new(0), new(1), new(2),
                  pl.BlockSpec((None, 2, R, nsub * 2 * HEADS, HEAD_DIM), lambda i: (layer, i, 0, 0, 0))],
        out_specs=[out_spec, out_spec],
        out_shape=[out_sds, out_sds],
        compiler_params=_cparams(1),
        name=f"attn_sample_g{g}",
    )(a_s, a_s, a_s, cv)


def _log_decay(ga, wa, ba):
    z = jnp.dot(ga, wa, preferred_element_type=F32, precision=lax.Precision.HIGHEST) + ba
    return (jnp.minimum(z, 0.0) - jnp.log(1.0 + jnp.exp(-jnp.abs(z)))) * (1.0 / GLA_TAU)


def _gla_prompt_kernel(q_ref, k_ref, v_ref, ga_ref, wa_ref, ba_ref, o_ref, sfin_ref, st_ref, la_ref,
                      qe_ref, upd_ref, ebl_ref):
    c = pl.program_id(1)
    C = GLA_CHUNK
    per_run = C // RUNS
    nsb = C // GLA_SUB
    sub_w = GLA_SUB // RUNS

    @pl.when(c == 0)
    def _():
        st_ref[...] = jnp.zeros_like(st_ref)

    la_ref[...] = _log_decay(ga_ref[...], wa_ref[...], ba_ref[...])

    r_i = lax.broadcasted_iota(jnp.int32, (C, C), 0)
    c_i = lax.broadcasted_iota(jnp.int32, (C, C), 1)
    tok_r = RUNS * (r_i % per_run) + r_i // per_run
    tok_c = RUNS * (c_i % per_run) + c_i // per_run
    tri = (tok_c <= tok_r).astype(F32)
    diag = (tok_r // GLA_SUB == tok_c // GLA_SUB) & (tok_c <= tok_r)
    rowblk = (lax.broadcasted_iota(jnp.int32, (C, GLA_HK), 0) % per_run) // sub_w
    last_run = (RUNS - 1) * per_run

    def gather(ref, j):
        return jnp.concatenate(
            [ref[r * RUN_LEN + j * per_run: r * RUN_LEN + (j + 1) * per_run, :] for r in range(RUNS)], axis=0)

    def scatter(ref, j, val, add):
        for r in range(RUNS):
            rows = slice(r * RUN_LEN + j * per_run, r * RUN_LEN + (j + 1) * per_run)
            piece = val[r * per_run:(r + 1) * per_run, :]
            ref[rows, :] = ref[rows, :] + piece if add else piece

    def local(js):
        def ref_rows(b, i):
            end_row = last_run + (i + 1) * sub_w - 1
            return b[end_row - sub_w:end_row - sub_w + 1, :], b[end_row:end_row + 1, :]

        b = [jnp.dot(tri, gather(la_ref, j), preferred_element_type=F32, precision=lax.Precision.HIGHEST) for j in js]
        q = [gather(q_ref, j) * (GLA_HK ** -0.5) for j in js]
        k = [gather(k_ref, j) for j in js]
        v = [gather(v_ref, j).astype(BF16) for j in js]
        qd, kd, ke, bstart = [], [], [], []
        for t, j in enumerate(js):
            qe_ref[j] = (q[t] * jnp.exp(b[t])).astype(BF16)
            bs = jnp.zeros_like(b[t])
            be = jnp.zeros_like(b[t])
            for i in range(nsb):
                prev_end, own_end = ref_rows(b[t], i)
                be = jnp.where(rowblk == i, own_end, be)
                if i > 0:
                    bs = jnp.where(rowblk == i, prev_end, bs)
            bstart.append(bs)
            qd.append(q[t] * jnp.exp(b[t] - bs))
            kd.append((k[t] * jnp.exp(bs - b[t])).astype(BF16))
            ke.append(k[t] * jnp.exp(be - b[t]))
        att = [jnp.where(diag, lax.dot_general(qd[t].astype(BF16), kd[t], _NT, preferred_element_type=F32), 0.0)
               for t in range(len(js))]
        for i in range(nsb - 1):
            qj, kej = [], []
            for t in range(len(js)):
                own_end = ref_rows(b[t], i)[1]
                dj = jnp.where(rowblk > i, jnp.exp(jnp.minimum(bstart[t] - own_end, 0.0)), 0.0)
                qj.append((qd[t] * dj).astype(BF16))
                kej.append(jnp.where(rowblk == i, ke[t], 0.0).astype(BF16))
            att = [att[t] + lax.dot_general(qj[t], kej[t], _NT, preferred_element_type=F32) for t in range(len(js))]
        intra = [jnp.dot(att[t].astype(BF16), v[t], preferred_element_type=F32) for t in range(len(js))]
        kdec = [(k[t] * jnp.exp(b[t][C - 1:C, :] - b[t])).astype(BF16) for t in range(len(js))]
        upd = [lax.dot_general(v[t], kdec[t], _TN, preferred_element_type=F32) for t in range(len(js))]
        for t, j in enumerate(js):
            scatter(o_ref, j, intra[t], add=False)
            upd_ref[j] = upd[t]
            ebl_ref[j] = jnp.broadcast_to(jnp.exp(b[t][C - 1:C, :]), (SUBLANES, GLA_HK))

    def recur(j):
        st = st_ref[...]
        scatter(o_ref, j, lax.dot_general(qe_ref[j], st.astype(BF16), _NT, preferred_element_type=F32), add=True)
        st_ref[...] = st * ebl_ref[j][0:1, :] + upd_ref[j]

    n_steps = CHUNK // C
    for j0 in range(0, n_steps, GLA_GROUP):
        local(list(range(j0, j0 + GLA_GROUP)))
    for j in range(n_steps):
        recur(j)

    @pl.when(c == pl.num_programs(1) - 1)
    def _():
        sfin_ref[...] = st_ref[...].T


def _gla_prompt(b, wa_pad, ba):
    T = seq = b.shape[0]
    assert seq % CHUNK == 0
    q0 = GLA_DV // GLA_HK
    k0 = q0 + GLA_DK // GLA_HK
    a0 = k0 + GLA_DK // GLA_HK
    og, sfin = pl.pallas_call(
        _gla_prompt_kernel,
        grid=(GLA_HEADS, seq // CHUNK),
        in_specs=[pl.BlockSpec((CHUNK, GLA_HK), lambda h, n: (n, q0 + h)),
                  pl.BlockSpec((CHUNK, GLA_HK), lambda h, n: (n, k0 + h)),
                  pl.BlockSpec((CHUNK, GLA_HV), lambda h, n: (n, h)),
                  pl.BlockSpec((CHUNK, LANE), lambda h, n: (n, a0)),
                  pl.BlockSpec((LANE, GLA_HK), lambda h, n: (0, h)),
                  pl.BlockSpec((1, GLA_HK), lambda h, n: (0, h))],
        out_specs=[pl.BlockSpec((CHUNK, GLA_HV), lambda h, n: (n, h)),
                   pl.BlockSpec((None, GLA_HK, GLA_HV), lambda h, n: (h, 0, 0))],
        out_shape=[jax.ShapeDtypeStruct((T, GLA_DV), F32),
                   jax.ShapeDtypeStruct((GLA_HEADS, GLA_HK, GLA_HV), F32)],
        scratch_shapes=[pltpu.VMEM((GLA_HV, GLA_HK), F32), pltpu.VMEM((CHUNK, GLA_HK), F32),
                        pltpu.VMEM((CHUNK // GLA_CHUNK, GLA_CHUNK, GLA_HK), BF16),
                        pltpu.VMEM((CHUNK // GLA_CHUNK, GLA_HV, GLA_HK), F32),
                        pltpu.VMEM((CHUNK // GLA_CHUNK, SUBLANES, GLA_HK), F32)],
        compiler_params=_cparams(2),
        name="gla_prompt",
    )(b, b, b, b, wa_pad, ba)
    return og, sfin


def _gla_sample_kernel(q_ref, k_ref, vlo_ref, vhi_ref, ga_ref, wa_ref, ba_ref, s0_ref, o_ref, snew_ref,
                       *, dec_seq, precise):
    rows8 = 2 * dec_seq
    r_i = lax.broadcasted_iota(jnp.int32, (rows8, rows8), 0)
    c_i = lax.broadcasted_iota(jnp.int32, (rows8, rows8), 1)
    same = (r_i // dec_seq == c_i // dec_seq) & (c_i <= r_i)
    tri = same.astype(F32)
    la = _log_decay(ga_ref[...], wa_ref[...], ba_ref[...])
    b_all = jnp.dot(tri, la, preferred_element_type=F32, precision=lax.Precision.HIGHEST)
    row_seq_k = lax.broadcasted_iota(jnp.int32, (rows8, GLA_HK), 0) // dec_seq
    row_seq_v = lax.broadcasted_iota(jnp.int32, (rows8, GLA_HV), 0) // dec_seq
    for h in range(GLA_HEADS):
        ks = slice(h * GLA_HK, (h + 1) * GLA_HK)
        vs = slice(h * GLA_HV, (h + 1) * GLA_HV)
        b = b_all[:, ks]
        q = q_ref[:, ks] * (GLA_HK ** -0.5)
        k = k_ref[:, ks]
        half = GLA_HEADS // 2
        v_src = vlo_ref if h < half else vhi_ref
        v = v_src[:, (h % half) * GLA_HV:(h % half + 1) * GLA_HV]
        qd = q * jnp.exp(b)
        kd = k * jnp.exp(-b)
        att = jnp.where(same, _dot(qd, kd, precise, _NT), 0.0)
        o = _dot(att, v, precise)
        for bb in range(2):
            st = s0_ref[bb, h].T
            o = o + jnp.where(row_seq_v == bb, _dot(qd, st, precise, _NT), 0.0)
            bl = b[(bb + 1) * dec_seq - 1:(bb + 1) * dec_seq, :]
            kdec = jnp.where(row_seq_k == bb, k * jnp.exp(bl - b), 0.0)
            snew_ref[bb, h] = (st * jnp.exp(bl) + _dot(v, kdec, precise, _TN)).T
        o_ref[:, vs] = o


def _gla_sample(b_s, wa_pad, ba, state, layer, dec_seq, precise):
    B = state.shape[1]
    rows8 = 2 * dec_seq
    q0 = GLA_DV // GLA_DK
    a0 = (GLA_DV + 2 * GLA_DK) // LANE
    assert q0 == 2
    blk = lambda c: pl.BlockSpec((rows8, GLA_DK), lambda i: (i, c))
    return pl.pallas_call(
        functools.partial(_gla_sample_kernel, dec_seq=dec_seq, precise=precise),
        grid=(B // 2,),
        in_specs=[blk(q0), blk(q0 + 1), blk(0), blk(1),
                  pl.BlockSpec((rows8, LANE), lambda i: (i, a0)),
                  pl.BlockSpec((LANE, GLA_DK), lambda i: (0, 0)),
                  pl.BlockSpec((1, GLA_DK), lambda i: (0, 0)),
                  pl.BlockSpec((None, 2, GLA_HEADS, GLA_HK, GLA_HV), lambda i: (layer, i, 0, 0, 0))],
        out_specs=[pl.BlockSpec((rows8, GLA_DV), lambda i: (i, 0)),
                   pl.BlockSpec((2, GLA_HEADS, GLA_HK, GLA_HV), lambda i: (i, 0, 0, 0))],
        out_shape=[jax.ShapeDtypeStruct((B * dec_seq, GLA_DV), F32), jax.ShapeDtypeStruct(state.shape[1:], F32)],
        compiler_params=_cparams(1),
        name="gla_sample",
    )(b_s, b_s, b_s, b_s, b_s, wa_pad, ba, state)


def _post_kernel(o1, o2, o3, l1, l2, l3, og_ref, xin_ref, x_ref, *rest, alpha, precise):
    n_gate = 2 if precise else 1
    wgate_refs = rest[:n_gate]
    wpa_ref, wpb_ref, wout_ref, gn_ref, lg_ref, lb_ref, xo_ref, xb_ref = rest[n_gate:]
    wgate = tuple(r[...] for r in wgate_refs) if precise else wgate_refs[0][...]
    gates = _dot(xin_ref[...], wgate, precise)
    D = x_ref.shape[1]
    gr, gate_a, gate_b = gates[:, :GLA_DV], gates[:, GLA_DV:GLA_DV + D], gates[:, GLA_DV + D:]
    m = jnp.maximum(jnp.maximum(l1[...], l2[...]), l3[...])
    e1 = jnp.exp(l1[...] - m)
    e2 = jnp.exp(l2[...] - m)
    e3 = jnp.exp(l3[...] - m)
    att = (e1 * o1[...] + e2 * o2[...] + e3 * o3[...]) / (e1 + e2 + e3)
    ya = _dot(att, wpa_ref[...], precise)
    parts = []
    for h in range(GLA_HEADS):
        vs = slice(h * GLA_HV, (h + 1) * GLA_HV)
        og = og_ref[:, vs]
        ms = jnp.mean(og * og, axis=-1, keepdims=True)
        parts.append(_operand(og * lax.rsqrt(ms + RMS_EPS) * gn_ref[:, vs] * _silu(gr[:, vs]), precise))
    y = jnp.concatenate(parts, axis=-1)
    yb = _dot(y, wpb_ref[...], precise)
    merged = _sigmoid(gate_a) * ya + _sigmoid(gate_b) * yb
    mix = _dot(merged, wout_ref[...], precise)
    xn = _layer_norm_rows(alpha * x_ref[...] + mix, lg_ref[...], lb_ref[...])
    xo_ref[...] = xn
    xb_ref[...] = xn.astype(BF16)


def _post(os_, ls_, og, xin, x, wgate, wpa, wpb, wout, gn, lg, lb, alpha, tm):
    precise = isinstance(wgate, tuple)
    wgates = wgate if precise else (wgate,)
    T, D = x.shape
    row = lambda w: pl.BlockSpec((tm, w), lambda i: (i, 0))
    full = lambda a: pl.BlockSpec(a.shape, lambda i: (0,) * a.ndim)
    resident = lambda a: pl.BlockSpec(a.shape, lambda i: (0,) * a.ndim, pipeline_mode=pl.Buffered(1))
    return pl.pallas_call(
        functools.partial(_post_kernel, alpha=alpha, precise=precise),
        grid=(pl.cdiv(T, tm),),
        in_specs=[row(ATT_WIDTH)] * 6 + [row(GLA_DV), row(D), row(D)] + [resident(w) for w in wgates]
                 + [resident(wpa), resident(wpb), resident(wout), full(gn), full(lg), full(lb)],
        out_specs=[row(D), row(D)],
        out_shape=[jax.ShapeDtypeStruct((T, D), F32), jax.ShapeDtypeStruct((T, D), BF16)],
        compiler_params=_cparams(1),
        name="post_mixer_sample" if precise else "post_mixer",
    )(*os_, *ls_, og, xin, x, *wgates, wpa, wpb, wout, gn, lg, lb)


def _ffn_kernel(xb_ref, x_ref, wg_ref, wu_ref, wd_ref, lg_ref, lb_ref, xo_ref, xbo_ref, acc_ref, *, alpha, precise):
    j = pl.program_id(1)
    xb = xb_ref[...]
    g = _dot(xb, wg_ref[...], precise)
    u = _dot(xb, wu_ref[...], precise)
    part = _dot(_silu(g) * u, wd_ref[...], precise)

    @pl.when(j == 0)
    def _():
        acc_ref[...] = part

    @pl.when(j > 0)
    def _():
        acc_ref[...] += part

    @pl.when(j == pl.num_programs(1) - 1)
    def _():
        xn = _layer_norm_rows(alpha * x_ref[...] + acc_ref[...], lg_ref[...], lb_ref[...])
        xo_ref[...] = xn
        xbo_ref[...] = xn.astype(BF16)


def _ffn(xb, x, wgu, wd, lg, lb, alpha, tm, tc, precise=False):
    T, D = x.shape
    dff = wd.shape[0]
    assert dff % tc == 0
    nj = dff // tc
    full = lambda a: pl.BlockSpec(a.shape, lambda i, j: (0,) * a.ndim)
    wmode = dict(pipeline_mode=pl.Buffered(1)) if nj == 1 else {}
    return pl.pallas_call(
        functools.partial(_ffn_kernel, alpha=alpha, precise=precise),
        grid=(pl.cdiv(T, tm), nj),
        in_specs=[pl.BlockSpec((tm, D), lambda i, j: (i, 0)),
                  pl.BlockSpec((tm, D), lambda i, j: (i, 0)),
                  pl.BlockSpec((D, tc), lambda i, j: (0, j), **wmode),
                  pl.BlockSpec((D, tc), lambda i, j: (0, nj + j), **wmode),
                  pl.BlockSpec((tc, D), lambda i, j: (j, 0), **wmode),
                  full(lg), full(lb)],
        out_specs=[pl.BlockSpec((tm, D), lambda i, j: (i, 0))] * 2,
        out_shape=[jax.ShapeDtypeStruct((T, D), F32), jax.ShapeDtypeStruct((T, D), BF16)],
        scratch_shapes=[pltpu.VMEM((tm, D), F32)],
        compiler_params=_cparams(2),
        name="ffn_dense_sample" if precise else "ffn_dense",
    )(xb, x, wgu, wgu, wd, lg, lb)


MOE_TOKENS = 512
MOE_PIECE = 16
MOE_ROWS = -(-(2 * MOE_TOKENS + N_EXPERTS * (MOE_PIECE - 1)) // MOE_PIECE) * MOE_PIECE
MOE_BLOCK = 1024
MOE_HALF = 512
MOE_SLOTS = MOE_BLOCK // MOE_PIECE


def _tile_rows(tile_ref, tail_ref, is_tail):
    ts, d = tile_ref.shape
    pad = jnp.zeros((ts - tail_ref.shape[0], d), tail_ref.dtype)
    tail = jnp.concatenate([tail_ref[...], pad], axis=0).astype(tile_ref.dtype)
    return jnp.where(is_tail, tail, tile_ref[...])


def _moe_route_kernel(x_ref, xb_ref, xtail_ref, wrt_ref, xs_ref, meta_ref, cnt_ref, *, prompt_tiles):
    i = pl.program_id(0)
    ts = x_ref.shape[0]
    is_tail = i == prompt_tiles
    x = _tile_rows(x_ref, xtail_ref, is_tail)
    xb = _tile_rows(xb_ref, xtail_ref, is_tail)
    lt = lax.dot_general(wrt_ref[...], x, _NT, preferred_element_type=F32, precision=lax.Precision.HIGHEST)
    sub = lax.broadcasted_iota(jnp.int32, (N_EXPERTS, ts), 0).astype(F32)
    tok_ok = lax.broadcasted_iota(jnp.int32, (1, ts), 1) < jnp.where(is_tail, xtail_ref.shape[0], ts)
    m1 = jnp.max(lt, axis=0, keepdims=True)
    i1 = jnp.min(jnp.where(lt == m1, sub, float(N_EXPERTS)), axis=0, keepdims=True)
    lt2 = jnp.where(sub == i1, NEG, lt)
    m2 = jnp.max(lt2, axis=0, keepdims=True)
    i2 = jnp.min(jnp.where(lt2 == m2, sub, float(N_EXPERTS)), axis=0, keepdims=True)
    e2 = jnp.exp(m2 - m1)
    g1 = 1.0 / (1.0 + e2)
    g2 = e2 / (1.0 + e2)
    oh1 = (sub == i1) & tok_ok
    oh2 = (sub == i2) & tok_ok
    assign = jnp.where(oh1 | oh2, 1.0, 0.0)
    before = (lax.broadcasted_iota(jnp.int32, (ts, ts), 0) < lax.broadcasted_iota(jnp.int32, (ts, ts), 1))
    rank = jnp.dot(assign.astype(BF16), jnp.where(before, 1.0, 0.0).astype(BF16), preferred_element_type=F32)
    cnt = jnp.sum(assign, axis=1, keepdims=True)
    padded = jnp.floor((cnt + (MOE_PIECE - 1)) * (1.0 / MOE_PIECE)) * MOE_PIECE
    lower = (lax.broadcasted_iota(jnp.int32, (N_EXPERTS, N_EXPERTS), 1)
             < lax.broadcasted_iota(jnp.int32, (N_EXPERTS, N_EXPERTS), 0)).astype(F32)
    seg_off = jnp.dot(lower, jnp.broadcast_to(padded, (N_EXPERTS, ts)), preferred_element_type=F32,
                      precision=lax.Precision.HIGHEST)
    dest = seg_off + rank
    d1 = jnp.where(tok_ok, jnp.sum(jnp.where(oh1, dest, 0.0), axis=0, keepdims=True), -1.0)
    d2 = jnp.where(tok_ok, jnp.sum(jnp.where(oh2, dest, 0.0), axis=0, keepdims=True), -1.0)
    p_i = lax.broadcasted_iota(jnp.int32, (MOE_ROWS, ts), 0).astype(F32)
    perm = jnp.where((p_i == d1) | (p_i == d2), 1.0, 0.0).astype(BF16)
    xs_ref[...] = jnp.dot(perm, xb, preferred_element_type=F32).astype(BF16)
    sub8 = lax.broadcasted_iota(jnp.int32, (SUBLANES, ts), 0)
    meta_ref[...] = jnp.where(sub8 == 0, d1, jnp.where(sub8 == 1, d2, jnp.where(sub8 == 2, g1,
                              jnp.where(sub8 == 3, g2, 0.0))))
    cnt_ref[...] = jnp.broadcast_to(cnt, (N_EXPERTS, LANE))


def _moe_route(x, xb, x_tail, wr_t):
    seq, D = x.shape
    assert seq % MOE_TOKENS == 0 and 0 < x_tail.shape[0] <= MOE_TOKENS
    npt = seq // MOE_TOKENS
    nt = npt + 1
    tile = pl.BlockSpec((MOE_TOKENS, D), lambda i: (jnp.minimum(i, npt - 1), 0))
    return pl.pallas_call(
        functools.partial(_moe_route_kernel, prompt_tiles=npt),
        grid=(nt,),
        in_specs=[tile, tile,
                  pl.BlockSpec(x_tail.shape, lambda i: (0, 0)),
                  pl.BlockSpec(wr_t.shape, lambda i: (0, 0))],
        out_specs=[pl.BlockSpec((MOE_ROWS, D), lambda i: (i, 0)),
                   pl.BlockSpec((None, SUBLANES, MOE_TOKENS), lambda i: (i, 0, 0)),
                   pl.BlockSpec((None, N_EXPERTS, LANE), lambda i: (i, 0, 0))],
        out_shape=[jax.ShapeDtypeStruct((nt * MOE_ROWS, D), BF16),
                   jax.ShapeDtypeStruct((nt, SUBLANES, MOE_TOKENS), F32),
                   jax.ShapeDtypeStruct((nt, N_EXPERTS, LANE), F32)],
        compiler_params=_cparams(1),
        name="moe_route",
    )(x, xb, x_tail, wr_t)


def _moe_plan(cnt, n_blocks):
    nt = cnt.shape[0]
    i32 = jnp.int32
    padded = (cnt + MOE_PIECE - 1) // MOE_PIECE * MOE_PIECE
    seg_row = jnp.cumsum(padded, axis=1) - padded + (jnp.arange(nt, dtype=i32) * MOE_ROWS)[:, None]
    n_seg = (padded // MOE_PIECE).T.reshape(-1)
    seg_row = seg_row.T.reshape(-1)
    seg_end = jnp.cumsum(n_seg)
    seg_start = seg_end - n_seg
    per_e = n_seg.reshape(N_EXPERTS, nt).sum(axis=1)
    e_end = jnp.cumsum(per_e)
    e_start = e_end - per_e
    blocks_e = (per_e + MOE_SLOTS - 1) // MOE_SLOTS
    blk_end = jnp.cumsum(blocks_e)
    blk_start = blk_end - blocks_e
    w = jnp.arange(n_blocks, dtype=i32)
    ew = jnp.minimum(jnp.sum((w[:, None] >= blk_end[None, :]).astype(i32), axis=1), N_EXPERTS - 1)
    is_e = ew[:, None] == jnp.arange(N_EXPERTS, dtype=i32)[None, :]
    pick = lambda v: jnp.sum(jnp.where(is_e, v[None, :], 0), axis=1)
    first = pick(e_start) + (w - pick(blk_start)) * MOE_SLOTS
    n_w = jnp.where(w < blk_end[-1], jnp.clip(pick(e_end) - first, 0, MOE_SLOTS), 0).astype(i32)
    slot = jnp.arange(MOE_SLOTS, dtype=i32)
    p = first[:, None] + slot[None, :]
    in_seg = (p[:, :, None] >= seg_start[None, None, :]) & (p[:, :, None] < seg_end[None, None, :])
    rows = jnp.sum(jnp.where(in_seg, (seg_row - MOE_PIECE * seg_start)[None, None, :], 0), axis=2) + MOE_PIECE * p
    rows = jnp.where(slot[None, :] < n_w[:, None], rows, 0).astype(i32)
    return ew.astype(i32), n_w, rows.reshape(-1)


def _moe_expert_kernel(ew_ref, n_ref, rows_ref, xs_hbm, wg_ref, wu_ref, wd_ref, ys_in, ys_hbm,
                       lhs_ref, acc_ref, out_ref, sem_in, sem_out):
    del ew_ref, ys_in
    w = pl.program_id(0)
    j = pl.program_id(1)
    n_w = pl.num_programs(0)
    last_j = pl.num_programs(1) - 1
    n = n_ref[w]
    slot = w % 2
    halves = [(s, slice(s * MOE_HALF, (s + 1) * MOE_HALF)) for s in range(MOE_BLOCK // MOE_HALF)]

    def piece_in(blk, q, sl):
        src = pl.multiple_of(rows_ref[blk * MOE_SLOTS + q], MOE_PIECE)
        dst = pl.multiple_of(q * MOE_PIECE, MOE_PIECE)
        return pltpu.make_async_copy(xs_hbm.at[pl.ds(src, MOE_PIECE), :],
                                     lhs_ref.at[sl, pl.ds(dst, MOE_PIECE), :], sem_in.at[sl])

    def piece_out(blk, q):
        dst = pl.multiple_of(rows_ref[blk * MOE_SLOTS + q], MOE_PIECE)
        src = pl.multiple_of(q * MOE_PIECE, MOE_PIECE)
        return pltpu.make_async_copy(out_ref.at[pl.ds(src, MOE_PIECE), :], ys_hbm.at[pl.ds(dst, MOE_PIECE), :], sem_out)

    def for_pieces(blk, fn):
        lax.fori_loop(0, n_ref[blk], lambda q, c: (fn(q), c)[1], 0)

    def start_pieces(blk, make):
        n_blk = n_ref[blk]

        def pair(p, c):
            make(2 * p).start(priority=0)

            @pl.when(2 * p + 1 < n_blk)
            def _():
                make(2 * p + 1).start(priority=1)
            return c

        lax.fori_loop(0, (n_blk + 1) // 2, pair, 0)

    @pl.when(j == 0)
    def _():
        @pl.when(w == 0)
        def _():
            lhs_ref[...] = jnp.zeros_like(lhs_ref)
            start_pieces(0, lambda q: piece_in(0, q, 0))

        for_pieces(w, lambda q: piece_in(w, q, slot).wait())

        @pl.when(w + 1 < n_w)
        def _():
            start_pieces(w + 1, lambda q: piece_in(w + 1, q, 1 - slot))

    for s, rows in halves:
        @pl.when(n > s * (MOE_HALF // MOE_PIECE))
        def _(rows=rows):
            xb = lhs_ref[slot, rows, :]
            g = jnp.dot(xb, wg_ref[...], preferred_element_type=F32)
            u = jnp.dot(xb, wu_ref[...], preferred_element_type=F32)
            part = jnp.dot((_silu(g) * u).astype(BF16), wd_ref[...], preferred_element_type=F32)

            @pl.when(j == 0)
            def _():
                acc_ref[rows, :] = part

            @pl.when(j > 0)
            def _():
                acc_ref[rows, :] += part

    @pl.when(j == last_j)
    def _():
        @pl.when(w > 0)
        def _():
            for_pieces(w - 1, lambda q: piece_out(w - 1, q).wait())

        for s, rows in halves:
            @pl.when(n > s * (MOE_HALF // MOE_PIECE))
            def _(rows=rows):
                out_ref[rows, :] = acc_ref[rows, :].astype(BF16)

        start_pieces(w, lambda q: piece_out(w, q))

        @pl.when(w == n_w - 1)
        def _():
            for_pieces(w, lambda q: piece_out(w, q).wait())


def _moe_experts(xs, plan, wgu, wd, tc):
    ew, n_w, rows = plan
    n_blocks = ew.shape[0]
    D = xs.shape[1]
    dff = wd.shape[1]
    assert dff % tc == 0
    nj = dff // tc
    any_spec = pl.BlockSpec(memory_space=pl.ANY)
    return pl.pallas_call(
        _moe_expert_kernel,
        grid_spec=pltpu.PrefetchScalarGridSpec(
            num_scalar_prefetch=3,
            grid=(n_blocks, nj),
            in_specs=[any_spec,
                      pl.BlockSpec((None, D, tc), lambda w, j, ew, n, r: (ew[w], 0, j)),
                      pl.BlockSpec((None, D, tc), lambda w, j, ew, n, r: (ew[w], 0, nj + j)),
                      pl.BlockSpec((None, tc, D), lambda w, j, ew, n, r: (ew[w], j, 0)),
                      any_spec],
            out_specs=any_spec,
            scratch_shapes=[pltpu.VMEM((2, MOE_BLOCK, D), BF16), pltpu.VMEM((MOE_BLOCK, D), F32),
                            pltpu.VMEM((MOE_BLOCK, D), BF16),
                            pltpu.SemaphoreType.DMA((2,)), pltpu.SemaphoreType.DMA(())]),
        out_shape=jax.ShapeDtypeStruct(xs.shape, BF16),
        input_output_aliases={7: 0},
        compiler_params=_cparams(2),
        name="moe_experts",
    )(ew, n_w, rows, xs, wgu, wgu, wd, jnp.zeros(xs.shape, BF16))


def _moe_combine_kernel(ys_ref, meta_ref, x_ref, xtail_ref, lg_ref, lb_ref, out_x, out_xb, out_tail,
                        *, alpha, prompt_tiles):
    i = pl.program_id(0)
    ts = x_ref.shape[0]
    meta = meta_ref[...]
    d1, d2, g1, g2 = meta[0:1, :], meta[1:2, :], meta[2:3, :], meta[3:4, :]
    p_i = lax.broadcasted_iota(jnp.int32, (MOE_ROWS, ts), 0).astype(F32)
    back = (jnp.where(p_i == d1, g1, 0.0) + jnp.where(p_i == d2, g2, 0.0)).astype(BF16)
    f = lax.dot_general(back, ys_ref[...], _TN, preferred_element_type=F32)
    x = _tile_rows(x_ref, xtail_ref, i == prompt_tiles)
    xn = _layer_norm_rows(alpha * x + f, lg_ref[...], lb_ref[...])

    @pl.when(i < prompt_tiles)
    def _():
        out_x[...] = xn
        out_xb[...] = xn.astype(BF16)

    @pl.when(i == prompt_tiles)
    def _():
        out_tail[...] = xn[:out_tail.shape[0], :]


def _moe_combine(ys, meta, x, x_tail, lg, lb, alpha):
    seq, D = x.shape
    ns = x_tail.shape[0]
    npt = seq // MOE_TOKENS
    full = lambda a: pl.BlockSpec(a.shape, lambda i: (0,) * a.ndim)
    tile = pl.BlockSpec((MOE_TOKENS, D), lambda i: (jnp.minimum(i, npt - 1), 0))
    return pl.pallas_call(
        functools.partial(_moe_combine_kernel, alpha=alpha, prompt_tiles=npt),
        grid=(npt + 1,),
        in_specs=[pl.BlockSpec((MOE_ROWS, D), lambda i: (i, 0)),
                  pl.BlockSpec((None, SUBLANES, MOE_TOKENS), lambda i: (i, 0, 0)),
                  tile, full(x_tail), full(lg), full(lb)],
        out_specs=[tile, tile, pl.BlockSpec((ns, D), lambda i: (0, 0))],
        out_shape=[jax.ShapeDtypeStruct((seq, D), F32), jax.ShapeDtypeStruct((seq, D), BF16),
                   jax.ShapeDtypeStruct((ns, D), F32)],
        compiler_params=_cparams(1),
        name="moe_combine",
    )(ys, meta, x, x_tail, lg, lb)


def _moe(x, xb, x_tail, w_router, wgu, wd, lg, lb, alpha):
    T = x.shape[0] + x_tail.shape[0]
    nt = x.shape[0] // MOE_TOKENS + 1
    xs, meta, cnt = _moe_route(x, xb, x_tail, w_router.T)
    n_pieces = (2 * T + nt * N_EXPERTS * (MOE_PIECE - 1)) // MOE_PIECE + 1
    n_blocks = pl.cdiv(n_pieces, MOE_SLOTS) + N_EXPERTS
    plan = _moe_plan(cnt[:, :, 0].astype(jnp.int32), n_blocks)
    ys = _moe_experts(xs, plan, wgu, wd, MOE_CHUNK)
    return _moe_combine(ys, meta, x, x_tail, lg, lb, alpha)


def _split_w_in(w):
    a_end = N_GROUPS * 3 * ATT_WIDTH
    gq0, gk0, gv0 = a_end, a_end + GLA_DK, a_end + 2 * GLA_DK
    gr0 = gv0 + GLA_DV
    ga0 = gr0 + GLA_DV
    gta0 = ga0 + GLA_RANK
    used = GLA_DV + 2 * GLA_DK + GLA_RANK
    ga_pad = jnp.pad(w[:, ga0:gta0], ((0, 0), (0, GLA_PANEL - used)))
    panels = (w[:, :a_end],
              jnp.concatenate([w[:, gv0:gr0], w[:, gq0:gk0], w[:, gk0:gv0], ga_pad], axis=1),
              jnp.concatenate([w[:, gr0:ga0], w[:, gta0:]], axis=1))
    return tuple(_head_tail(p) for p in panels)


def _head_tail(w):
    head = w.astype(BF16)
    return head, (w - head.astype(F32)).astype(BF16)


def _kv_rows_kernel(kp_ref, vp_ref, ks_ref, vs_ref, op_ref, os_ref, *, keep):
    n = keep // RUNS
    ns = ks_ref.shape[0]
    for kv, (p_ref, s_ref) in enumerate(((kp_ref, ks_ref), (vp_ref, vs_ref))):
        for h in range(HEADS):
            cs = slice(h * HEAD_DIM, (h + 1) * HEAD_DIM)
            sub = kv * HEADS + h
            for r in range(RUNS):
                op_ref[pl.ds(2 * HEADS * r + sub, n, stride=2 * HEADS * RUNS), :] = p_ref[r, :, cs]
            os_ref[pl.ds(sub, ns, stride=2 * HEADS), :] = s_ref[:, cs]


def _new_kv_rows(a_mat, a_s, g, db, dec_seq):
    seq = a_mat.shape[0]
    keep = min(ATT_GROUPS[g][0], seq)
    ns = db * dec_seq
    assert keep <= CHUNK and keep % RUNS == 0 and seq % CHUNK == 0 and a_s.shape[0] == ns
    last = seq // CHUNK - 1
    n = keep // RUNS
    runs = a_mat.reshape(seq // RUN_LEN, RUN_LEN, a_mat.shape[1])
    tail = lambda c: pl.BlockSpec((RUNS, n, ATT_WIDTH), lambda i: (last, RUN_LEN // n - 1, 3 * g + c))
    spec = lambda rows, rb, c: pl.BlockSpec((rows, ATT_WIDTH), lambda i: (rb, 3 * g + c))
    kv_p, kv_s = pl.pallas_call(
        functools.partial(_kv_rows_kernel, keep=keep),
        grid=(1,),
        in_specs=[tail(1), tail(2), spec(ns, 0, 1), spec(ns, 0, 2)],
        out_specs=[pl.BlockSpec((keep * 2 * HEADS, HEAD_DIM), lambda i: (0, 0)),
                   pl.BlockSpec((ns * 2 * HEADS, HEAD_DIM), lambda i: (0, 0))],
        out_shape=[jax.ShapeDtypeStruct((keep * 2 * HEADS, HEAD_DIM), F32),
                   jax.ShapeDtypeStruct((ns * 2 * HEADS, HEAD_DIM), F32)],
        compiler_params=_cparams(1),
        name=f"kv_rows_g{g}",
    )(runs, runs, a_s, a_s)
    return kv_p.reshape(1, keep, 2, HEADS, HEAD_DIM), kv_s.reshape(db, dec_seq, 2, HEADS, HEAD_DIM)


def kernel(x_prompt, x_sample, cache_kv_w128, cache_kv_w512, cache_kv_w2048, state_gla,
           w_in, w_alpha2, b_alpha, gla_norm_g, w_pa, w_pb, w_out,
           ln_mix_g, ln_mix_b, ln_ffn_g, ln_ffn_b,
           w_ffn_gu, w_ffn_down, w_router, w_exp_gu, w_exp_down):
    depth = w_in.shape[0]
    bp, seq, D = x_prompt.shape
    db, dec_seq, _ = x_sample.shape
    assert bp == 1
    ns = db * dec_seq
    alpha = (2 * depth) ** 0.25
    caches = (cache_kv_w128, cache_kv_w512, cache_kv_w2048)

    x = _interleave_rows(x_prompt.reshape(seq, D))
    xb = x.astype(BF16)
    xs = x_sample.reshape(ns, D)
    kv_p = [[] for _ in range(N_GROUPS)]
    kv_s = [[] for _ in range(N_GROUPS)]
    st_p, st_s = [], []
    row2 = lambda v: v.reshape(1, -1)

    for l in range(depth):
        wa, wb, wg = _split_w_in(w_in[l])
        a_mat = _mm(xb, wa[0], PROJ_ROWS, PROJ_COLS_ATT)
        b_mat = _mm(xb, wb[0], PROJ_ROWS, GLA_PANEL)
        a_s = _mm(xs, wa, ns, PROJ_COLS_ATT)
        b_s = _mm(xs, wb, ns, GLA_PANEL)

        os_, ls_, os_s, ls_s = [], [], [], []
        for g in range(N_GROUPS):
            o, lse = _attn_prompt(a_mat, g)
            os_.append(o)
            ls_.append(lse)
            o, lse = _attn_sample(a_s, caches[g], l, g, dec_seq, precise=True)
            os_s.append(o)
            ls_s.append(lse)
            p_rows, s_rows = _new_kv_rows(a_mat, a_s, g, db, dec_seq)
            kv_p[g].append(p_rows)
            kv_s[g].append(s_rows)

        wa_pad = jnp.pad(w_alpha2[l], ((0, LANE - GLA_RANK), (0, 0)))
        og, s_p = _gla_prompt(b_mat, wa_pad, row2(b_alpha[l]))
        og_s, s_s = _gla_sample(b_s, wa_pad, row2(b_alpha[l]), state_gla, l, dec_seq, precise=True)
        st_p.append(s_p[None])
        st_s.append(s_s)

        norms = (row2(gla_norm_g[l]), row2(ln_mix_g[l]), row2(ln_mix_b[l]))
        x, xb = _post(os_, ls_, og, xb, x, wg[0], w_pa[l].astype(BF16), w_pb[l].astype(BF16), w_out[l].astype(BF16),
                      *norms, alpha, POST_ROWS)
        xs, _ = _post(os_s, ls_s, og_s, xs, xs, wg, w_pa[l], w_pb[l], w_out[l], *norms, alpha, ns)

        ffn_norm = (row2(ln_ffn_g[l]), row2(ln_ffn_b[l]), alpha)
        if l % 2 == 0:
            x, xb = _ffn(xb, x, w_ffn_gu[l // 2].astype(BF16), w_ffn_down[l // 2].astype(BF16),
                         *ffn_norm, FFN_ROWS, FFN_CHUNK)
            xs, _ = _ffn(xs, xs, w_ffn_gu[l // 2], w_ffn_down[l // 2], *ffn_norm, ns, SAMPLE_FFN_CHUNK, precise=True)
        else:
            x, xb, xs = _moe(x, xb, xs, w_router[l // 2], w_exp_gu[l // 2].astype(BF16),
                             w_exp_down[l // 2].astype(BF16), *ffn_norm)

    return (_deinterleave_rows(x).reshape(1, seq, D), xs.reshape(db, dec_seq, D),
            jnp.stack(kv_p[0], 0), jnp.stack(kv_s[0], 0),
            jnp.stack(kv_p[1], 0), jnp.stack(kv_s[1], 0),
            jnp.stack(kv_p[2], 0), jnp.stack(kv_s[2], 0),
            jnp.stack(st_p, 0), jnp.stack(st_s, 0))
```
